```python
import jax, jax.numpy as jnp
from jax import lax
import numpy as np

D_MODEL = 1024
BATCH = 8
SEQ = 2048
DEPTH = 2
DEC_BATCH = 128
DEC_SEQ = 8
PAST_LEN = 16384
PAGE_SIZE = 128

D_PLE = 256
D_FF = 2816
GROUP_W = D_MODEL // 4
N_HEADS_PER_GROUP = 4
HEAD_DIM = GROUP_W // N_HEADS_PER_GROUP
MIX_IN_W = 8 * GROUP_W
CONV_A_WIDTH = 31
POOL_WINDOWS = (2, 4, 8, 16)
POOL_STATE = max(POOL_WINDOWS) - 1
CHUNK = 128
SHORT_CONV_WIDTH = 3
EPS = 1e-6

kernel_name = "hybrid_conv_pool_sgu_shortconv_decoder_step"


def _rms_norm(x, g):
    xf = x.astype(jnp.float32)
    y = xf * lax.rsqrt(jnp.mean(xf * xf, axis=-1, keepdims=True) + EPS)
    return (y * g.astype(jnp.float32)).astype(x.dtype)


def _head_layer_norm(x, g, b):
    shp = x.shape
    xf = x.astype(jnp.float32).reshape(shp[:-1] + (N_HEADS_PER_GROUP, HEAD_DIM))
    mu = jnp.mean(xf, axis=-1, keepdims=True)
    xc = xf - mu
    var = jnp.mean(xc * xc, axis=-1, keepdims=True)
    y = (xc * lax.rsqrt(var + EPS)).reshape(shp)
    return (y * g.astype(jnp.float32) + b.astype(jnp.float32)).astype(x.dtype)


def _swiglu(x, w_gate, w_up, w_down):
    return (jax.nn.silu(x @ w_gate) * (x @ w_up)) @ w_down


def _causal_depthwise_conv(x, prev, w):
    xp = jnp.concatenate([prev.astype(x.dtype), x], axis=1)
    y = lax.conv_general_dilated(xp, w[:, None, :].astype(x.dtype), window_strides=(1,),
                                 padding='VALID', dimension_numbers=('NWC', 'WIO', 'NWC'),
                                 feature_group_count=x.shape[-1])
    return y, xp[:, -(w.shape[0] - 1):]


def _conv_module(a_val, a_gate, prev, w, b, ng, nb):
    h = a_val * jax.nn.sigmoid(a_gate)
    c, new_state = _causal_depthwise_conv(h, prev, w)
    c = _head_layer_norm(c + b, ng, nb)
    return jax.nn.silu(c), new_state


def _multiscale_pool(z, prev, start, w_pool, scale):
    B_, L, _ = z.shape
    zp = jnp.concatenate([prev.astype(z.dtype), z], axis=1)
    cs = jnp.pad(jnp.cumsum(zp.astype(jnp.float32), axis=1), ((0, 0), (1, 0), (0, 0)))
    pos = start + jnp.arange(L)
    means = []
    for g, win in enumerate(POOL_WINDOWS):
        sl = slice(g * HEAD_DIM, (g + 1) * HEAD_DIM)
        hi = cs[:, POOL_STATE + 1:POOL_STATE + 1 + L, sl]
        lo = cs[:, POOL_STATE + 1 - win:POOL_STATE + 1 - win + L, sl]
        cnt = jnp.minimum(pos + 1, win).astype(jnp.float32)[None, :, None]
        means.append((hi - lo) / cnt)
    mean = jnp.stack(means, axis=2)
    d = (mean - z.astype(jnp.float32).reshape(B_, L, N_HEADS_PER_GROUP, HEAD_DIM)).astype(z.dtype)
    y = jnp.einsum('blgc,gcd->blgd', d, w_pool).reshape(B_, L, GROUP_W)
    return y * scale, zp[:, -POOL_STATE:]


def _spatial_gating(u, v, ng, nb, w_s, b_s):
    v = _head_layer_norm(v, ng, nb)
    B_, L, C = v.shape
    cs = min(L, CHUNK)
    Lp = -(-L // cs) * cs
    vb = jnp.pad(v, ((0, 0), (0, Lp - L), (0, 0))).reshape(B_, Lp // cs, cs, N_HEADS_PER_GROUP, HEAD_DIM)
    w = jnp.tril(w_s[:, :cs, :cs])
    s = jnp.einsum('hts,bnshc->bnthc', w, vb) + b_s[:, :cs].T[None, None, :, :, None]
    s = s.reshape(B_, Lp, C)[:, :L]
    return u * s, v


def _short_gated_conv(b_gate, c_gate, h, prev, w):
    q = c_gate * h
    y, new_state = _causal_depthwise_conv(q, prev, w)
    return b_gate * y, new_state


def _trunk(x, p, st_a, st_pool, st_sc, start, wt):
    B_ = x.shape[0]
    new_a, new_pool, new_sc, new_v = [], [], [], []
    for i in range(DEPTH):
        h = _rms_norm(x, wt['norm_ffn1'][i])
        x = x + 0.5 * _swiglu(h, wt['w_ffn1_gate'][i], wt['w_ffn1_up'][i], wt['w_ffn1_down'][i])

        h = _rms_norm(x, wt['norm_mix'][i])
        z = h @ wt['w_mix_in'][i]
        a_val, a_gate, zb, u, v, b_gate, c_gate, d_in = jnp.split(z, 8, axis=-1)
        prev_a = jnp.zeros((B_, CONV_A_WIDTH - 1, GROUP_W), x.dtype) if st_a is None else st_a[i]
        prev_p = jnp.zeros((B_, POOL_STATE, GROUP_W), x.dtype) if st_pool is None else st_pool[i]
        prev_s = jnp.zeros((B_, SHORT_CONV_WIDTH - 1, GROUP_W), x.dtype) if st_sc is None else st_sc[i]

        y_a, s_a = _conv_module(a_val, a_gate, prev_a, wt['conv_a_w'][i], wt['conv_a_b'][i],
                                wt['norm_a_g'][i], wt['norm_a_b'][i])
        y_b, s_p = _multiscale_pool(zb, prev_p, start, wt['pool_w'][i], wt['pool_scale'][i])
        y_c, v_n = _spatial_gating(jax.nn.gelu(u), jax.nn.gelu(v), wt['sgu_norm_g'][i],
                                   wt['sgu_norm_b'][i], wt['sgu_w'][i], wt['sgu_b'][i])
        y_d, s_s = _short_gated_conv(b_gate, c_gate, d_in, prev_s, wt['short_conv_w'][i])
        x = x + jnp.concatenate([y_a, y_b, y_c, y_d], axis=-1) @ wt['w_mix_out'][i]

        h = _rms_norm(x, wt['norm_ffn2'][i])
        x = x + 0.5 * _swiglu(h, wt['w_ffn2_gate'][i], wt['w_ffn2_up'][i], wt['w_ffn2_down'][i])

        gate = jax.nn.sigmoid(_rms_norm(x, wt['norm_ple'][i]) @ wt['w_ple_gate'][i])
        x = x + gate * (p[i] @ wt['w_ple_proj'][i])

        new_a.append(s_a); new_pool.append(s_p); new_sc.append(s_s); new_v.append(v_n)
    y = _rms_norm(x, wt['norm_final'])
    return y, jnp.stack(new_a), jnp.stack(new_pool), jnp.stack(new_sc), jnp.stack(new_v)


def setup_inputs(seed: int = 0) -> dict:
    key = jax.random.key(seed)
    ks = iter(jax.random.split(key, 40))

    def nrm(shape, scale):
        return scale * jax.random.normal(next(ks), shape, jnp.float32)

    def gain(shape):
        return 1.0 + nrm(shape, 0.01)

    L = DEPTH
    return {
        'x_prompt': nrm((BATCH, SEQ, D_MODEL), 1.0),
        'x_sample': nrm((DEC_BATCH, DEC_SEQ, D_MODEL), 1.0),
        'p_prompt': nrm((DEPTH, BATCH, SEQ, D_PLE), 1.0),
        'p_sample': nrm((DEPTH, DEC_BATCH, DEC_SEQ, D_PLE), 1.0),
        'state_conv_a': nrm((DEPTH, DEC_BATCH, CONV_A_WIDTH - 1, GROUP_W), 0.5),
        'state_pool': nrm((DEPTH, DEC_BATCH, POOL_STATE, GROUP_W), 1.0),
        'state_short_conv': nrm((DEPTH, DEC_BATCH, SHORT_CONV_WIDTH - 1, GROUP_W), 0.5),
        'norm_ffn1': gain((L, D_MODEL)),
        'w_ffn1_gate': nrm((L, D_MODEL, D_FF), D_MODEL ** -0.5),
        'w_ffn1_up': nrm((L, D_MODEL, D_FF), D_MODEL ** -0.5),
        'w_ffn1_down': nrm((L, D_FF, D_MODEL), D_FF ** -0.5),
        'norm_mix': gain((L, D_MODEL)),
        'w_mix_in': nrm((L, D_MODEL, MIX_IN_W), D_MODEL ** -0.5),
        'conv_a_w': nrm((L, CONV_A_WIDTH, GROUP_W), CONV_A_WIDTH ** -0.5),
        'conv_a_b': nrm((L, GROUP_W), 0.02),
        'norm_a_g': gain((L, GROUP_W)),
        'norm_a_b': nrm((L, GROUP_W), 0.02),
        'pool_w': nrm((L, N_HEADS_PER_GROUP, HEAD_DIM, HEAD_DIM), HEAD_DIM ** -0.5),
        'pool_scale': 1.0 + nrm((L, GROUP_W), 0.1),
        'sgu_norm_g': gain((L, GROUP_W)),
        'sgu_norm_b': nrm((L, GROUP_W), 0.02),
        'sgu_w': nrm((L, N_HEADS_PER_GROUP, CHUNK, CHUNK), CHUNK ** -0.5),
        'sgu_b': 1.0 + nrm((L, N_HEADS_PER_GROUP, CHUNK), 0.02),
        'short_conv_w': nrm((L, SHORT_CONV_WIDTH, GROUP_W), SHORT_CONV_WIDTH ** -0.5),
        'w_mix_out': nrm((L, D_MODEL, D_MODEL), D_MODEL ** -0.5),
        'norm_ffn2': gain((L, D_MODEL)),
        'w_ffn2_gate': nrm((L, D_MODEL, D_FF), D_MODEL ** -0.5),
        'w_ffn2_up': nrm((L, D_MODEL, D_FF), D_MODEL ** -0.5),
        'w_ffn2_down': nrm((L, D_FF, D_MODEL), D_FF ** -0.5),
        'norm_ple': gain((L, D_MODEL)),
        'w_ple_gate': nrm((L, D_MODEL, D_MODEL), D_MODEL ** -0.5),
        'w_ple_proj': nrm((L, D_PLE, D_MODEL), D_PLE ** -0.5),
        'norm_final': gain((D_MODEL,)),
    }


def reference(x_prompt, x_sample, p_prompt, p_sample, state_conv_a, state_pool, state_short_conv,
              norm_ffn1, w_ffn1_gate, w_ffn1_up, w_ffn1_down, norm_mix, w_mix_in,
              conv_a_w, conv_a_b, norm_a_g, norm_a_b, pool_w, pool_scale,
              sgu_norm_g, sgu_norm_b, sgu_w, sgu_b, short_conv_w, w_mix_out,
              norm_ffn2, w_ffn2_gate, w_ffn2_up, w_ffn2_down,
              norm_ple, w_ple_gate, w_ple_proj, norm_final):
    wt = dict(norm_ffn1=norm_ffn1, w_ffn1_gate=w_ffn1_gate, w_ffn1_up=w_ffn1_up, w_ffn1_down=w_ffn1_down,
              norm_mix=norm_mix, w_mix_in=w_mix_in, conv_a_w=conv_a_w, conv_a_b=conv_a_b,
              norm_a_g=norm_a_g, norm_a_b=norm_a_b, pool_w=pool_w, pool_scale=pool_scale,
              sgu_norm_g=sgu_norm_g, sgu_norm_b=sgu_norm_b, sgu_w=sgu_w, sgu_b=sgu_b,
              short_conv_w=short_conv_w, w_mix_out=w_mix_out, norm_ffn2=norm_ffn2,
              w_ffn2_gate=w_ffn2_gate, w_ffn2_up=w_ffn2_up, w_ffn2_down=w_ffn2_down,
              norm_ple=norm_ple, w_ple_gate=w_ple_gate, w_ple_proj=w_ple_proj, norm_final=norm_final)
    y_prompt, conv_a_prompt, pool_prompt, short_conv_prompt, _ = _trunk(
        x_prompt, p_prompt, None, None, None, 0, wt)
    y_sample, conv_a_sample, pool_sample, short_conv_sample, chunk_v_sample = _trunk(
        x_sample, p_sample, state_conv_a, state_pool, state_short_conv, PAST_LEN, wt)
    return (y_prompt, y_sample, conv_a_prompt, conv_a_sample, pool_prompt, pool_sample,
            short_conv_prompt, short_conv_sample, chunk_v_sample)
```

```python
import functools

import jax
import jax.numpy as jnp
from jax import lax
from jax.experimental import pallas as pl
from jax.experimental.pallas import tpu as pltpu

D_MODEL = 1024
D_PLE = 256
D_FF = 2816
GROUP_W = 256
N_HEADS = 4
HEAD_DIM = 64
MIX_IN_W = 8 * GROUP_W
CONV_A_WIDTH = 31
POOL_WINDOWS = (2, 4, 8, 16)
POOL_STATE = 15
CHUNK = 128
SHORT_CONV_WIDTH = 3
EPS = 1e-6
PAST_LEN = 16384

V7X_SUBLANES = 8
V7X_VMEM_LIMIT_BYTES = 56 * 1024 * 1024

FFN_TILE = 512
FFN_COL_CHUNK = 1408
MIX_TILE = 256
MIX_HALO = 32
MIX_ROWS = 64
SAMPLE_SEQS = 32

COL_A_VAL, COL_A_GATE, COL_ZB, COL_U, COL_V, COL_B_GATE, COL_C_GATE, COL_D_IN = (
    i * GROUP_W for i in range(8))

bf16 = jnp.bfloat16
f32 = jnp.float32


def _dot(a, b):
    return jnp.dot(a, b, preferred_element_type=f32)


def _rms(x, g):
    ms = jnp.mean(x * x, axis=-1, keepdims=True)
    return x * lax.rsqrt(ms + EPS) * g


def _group_mean(x, avg):
    hi = x.astype(bf16)
    lo = (x - hi.astype(f32)).astype(bf16)
    return _dot(hi, avg) + _dot(lo, avg)


def _head_layer_norm(x, g, b, avg):
    mu = _group_mean(x, avg)
    xc = x - mu
    var = _group_mean(xc * xc, avg)
    return xc * lax.rsqrt(var + EPS) * g + b


def _silu(x):
    return x * jax.nn.sigmoid(x)


def _pool_select(sums, lane_group):
    s2, s4, s8, s16 = sums
    return jnp.where(lane_group == 0, s2,
                     jnp.where(lane_group == 1, s4, jnp.where(lane_group == 2, s8, s16)))


def _ffn_body(x, g_ref, wg_ref, wu_ref, wd_ref, act_ref):
    h = _rms(x, g_ref[...]).astype(bf16)
    for c in range(D_FF // FFN_COL_CHUNK):
        cs = slice(c * FFN_COL_CHUNK, (c + 1) * FFN_COL_CHUNK)
        gate = _dot(h, wg_ref[:, cs])
        up = _dot(h, wu_ref[:, cs])
        act_ref[:, cs] = (_silu(gate) * up).astype(bf16)
    return x + 0.5 * _dot(act_ref[...], wd_ref[...])


def _ffn_kernel(x_ref, g_ref, wg_ref, wu_ref, wd_ref, o_ref, act_ref):
    o_ref[...] = _ffn_body(x_ref[...], g_ref, wg_ref, wu_ref, wd_ref, act_ref)


def _ffn_ple_kernel(x_ref, g_ref, wg_ref, wu_ref, wd_ref, p_ref, gp_ref, wpg_ref, wpp_ref,
                    gf_ref, o_ref, act_ref, *, final_norm):
    x = _ffn_body(x_ref[...], g_ref, wg_ref, wu_ref, wd_ref, act_ref)
    gate = jax.nn.sigmoid(_dot(_rms(x, gp_ref[...]).astype(bf16), wpg_ref[...]))
    x = x + gate * _dot(p_ref[...].astype(bf16), wpp_ref[...])
    if final_norm:
        x = _rms(x, gf_ref[...])
    o_ref[...] = x


def _resident(shape):
    return pl.BlockSpec(shape, lambda *_: (0,) * len(shape), pipeline_mode=pl.Buffered(1))


def _ffn_call(x, g, wg, wu, wd, ple=None, final_norm=False):
    m = x.shape[0]
    tile = min(FFN_TILE, m)
    row_spec = pl.BlockSpec((tile, D_MODEL), lambda i: (i, 0))
    in_specs = [row_spec, _resident((1, D_MODEL)), _resident((D_MODEL, D_FF)),
                _resident((D_MODEL, D_FF)), _resident((D_FF, D_MODEL))]
    args = [x, g, wg, wu, wd]
    if ple is None:
        body = _ffn_kernel
        name = "ffn"
    else:
        p, gp, wpg, wpp, gf = ple
        body = functools.partial(_ffn_ple_kernel, final_norm=final_norm)
        name = "ffn_ple"
        in_specs += [pl.BlockSpec((tile, D_PLE), lambda i: (i, 0)), _resident((1, D_MODEL)),
                     _resident((D_MODEL, D_MODEL)), _resident((D_PLE, D_MODEL)),
                     _resident((1, D_MODEL))]
        args += [p, gp, wpg, wpp, gf]
    return pl.pallas_call(
        body,
        grid=(m // tile,),
        in_specs=in_specs,
        out_specs=row_spec,
        out_shape=jax.ShapeDtypeStruct((m, D_MODEL), f32),
        scratch_shapes=[pltpu.VMEM((tile, D_FF), bf16)],
        compiler_params=pltpu.CompilerParams(
            dimension_semantics=("arbitrary",), vmem_limit_bytes=V7X_VMEM_LIMIT_BYTES),
        name=name,
    )(*args)


def _mixer_prompt_kernel(x_ref, gm_ref, wmi_ref, wmo_ref, caw_ref, cab_ref, nag_ref, nab_ref,
                         pw_ref, ps_ref, sng_ref, snb_ref, sw_ref, sb_ref, scw_ref, avg_ref,
                         o_ref, sa_ref, sp_ref, ss_ref,
                         z_scr, c_scr, d_scr, y_scr):
    tm, halo, rb = MIX_TILE, MIX_HALO, MIX_ROWS
    step = pl.program_id(1)

    @pl.when(step == 0)
    def _():
        z_scr[0:halo, :] = jnp.zeros((halo, MIX_IN_W), f32)

    h = _rms(x_ref[...], gm_ref[...]).astype(bf16)
    z_scr[halo:halo + tm, :] = _dot(h, wmi_ref[...])

    for r in range(0, tm, rb):
        rows = pl.ds(halo + r, rb)
        a_val = z_scr[rows, COL_A_VAL:COL_A_VAL + GROUP_W]
        a_gate = z_scr[rows, COL_A_GATE:COL_A_GATE + GROUP_W]
        z_scr[rows, COL_A_VAL:COL_A_VAL + GROUP_W] = a_val * jax.nn.sigmoid(a_gate)
        c_gate = z_scr[rows, COL_C_GATE:COL_C_GATE + GROUP_W]
        d_in = z_scr[rows, COL_D_IN:COL_D_IN + GROUP_W]
        z_scr[rows, COL_D_IN:COL_D_IN + GROUP_W] = c_gate * d_in

    lane_group = lax.broadcasted_iota(jnp.int32, (rb, GROUP_W), 1) // HEAD_DIM
    win = jnp.left_shift(2, lane_group)
    row_iota = lax.broadcasted_iota(jnp.int32, (rb, GROUP_W), 0)

    for r in range(0, tm, rb):
        acc = jnp.broadcast_to(cab_ref[...], (rb, GROUP_W))
        for k in range(CONV_A_WIDTH):
            tap = z_scr[pl.ds(r + halo - (CONV_A_WIDTH - 1) + k, rb),
                        COL_A_VAL:COL_A_VAL + GROUP_W]
            acc = acc + caw_ref[k:k + 1, :] * tap
        c_scr[r:r + rb, :] = acc

        zb = z_scr[pl.ds(halo + r, rb), COL_ZB:COL_ZB + GROUP_W]
        run = zb
        sums = []
        for j in range(1, max(POOL_WINDOWS)):
            run = run + z_scr[pl.ds(halo + r - j, rb), COL_ZB:COL_ZB + GROUP_W]
            if j + 1 in POOL_WINDOWS:
                sums.append(run)
        pos = step * tm + r + row_iota
        cnt = jnp.minimum(pos + 1, win).astype(f32)
        mean = _pool_select(sums, lane_group) / cnt
        d_scr[r:r + rb, :] = (mean - zb).astype(bf16)

        rows = pl.ds(halo + r, rb)
        q0 = z_scr[pl.ds(halo + r - 2, rb), COL_D_IN:COL_D_IN + GROUP_W]
        q1 = z_scr[pl.ds(halo + r - 1, rb), COL_D_IN:COL_D_IN + GROUP_W]
        q2 = z_scr[rows, COL_D_IN:COL_D_IN + GROUP_W]
        conv = scw_ref[0:1, :] * q0 + scw_ref[1:2, :] * q1 + scw_ref[2:3, :] * q2
        b_gate = z_scr[rows, COL_B_GATE:COL_B_GATE + GROUP_W]
        y_scr[r:r + rb, 3 * GROUP_W:4 * GROUP_W] = (b_gate * conv).astype(bf16)

    avg = avg_ref[...]
    ya = _head_layer_norm(c_scr[...], nag_ref[...], nab_ref[...], avg)
    y_scr[:, 0:GROUP_W] = _silu(ya).astype(bf16)
    y_scr[:, GROUP_W:2 * GROUP_W] = (_dot(d_scr[...], pw_ref[...]) * ps_ref[...]).astype(bf16)

    t_idx = lax.broadcasted_iota(jnp.int32, (CHUNK, N_HEADS * CHUNK), 0)
    s_idx = lax.broadcasted_iota(jnp.int32, (CHUNK, N_HEADS * CHUNK), 1) % CHUNK
    w_cat = jnp.where(s_idx <= t_idx, sw_ref[...], 0.0).astype(bf16)
    head_of_lane = lax.broadcasted_iota(jnp.int32, (CHUNK, GROUP_W), 1) // HEAD_DIM
    for c in range(tm // CHUNK):
        rows = pl.ds(halo + c * CHUNK, CHUNK)
        u = jax.nn.gelu(z_scr[rows, COL_U:COL_U + GROUP_W])
        v = jax.nn.gelu(z_scr[rows, COL_V:COL_V + GROUP_W])
        v = _head_layer_norm(v, sng_ref[...], snb_ref[...], avg)
        v_stack = jnp.concatenate(
            [jnp.where(head_of_lane == hd, v, 0.0).astype(bf16) for hd in range(N_HEADS)], axis=0)
        s = _dot(w_cat, v_stack) + sb_ref[...]
        y_scr[c * CHUNK:(c + 1) * CHUNK, 2 * GROUP_W:3 * GROUP_W] = (u * s).astype(bf16)

    o_ref[...] = x_ref[...] + _dot(y_scr[...], wmo_ref[...])

    @pl.when(step == pl.num_programs(1) - 1)
    def _():
        sa_ref[0] = z_scr[pl.ds(halo + tm - (CONV_A_WIDTH - 1), CONV_A_WIDTH - 1),
                          COL_A_VAL:COL_A_VAL + GROUP_W]
        sp_ref[0] = z_scr[pl.ds(halo + tm - POOL_STATE, POOL_STATE), COL_ZB:COL_ZB + GROUP_W]
        ss_ref[0] = z_scr[pl.ds(halo + tm - (SHORT_CONV_WIDTH - 1), SHORT_CONV_WIDTH - 1),
                          COL_D_IN:COL_D_IN + GROUP_W]

    for col in (COL_A_VAL, COL_ZB, COL_D_IN):
        z_scr[0:halo, col:col + GROUP_W] = z_scr[tm:tm + halo, col:col + GROUP_W]


def _mixer_prompt_call(x, batch, seq, wts):
    tm = MIX_TILE
    steps = seq // tm
    row_spec = pl.BlockSpec((tm, D_MODEL), lambda b, l: (b * steps + l, 0))

    def state_spec(rows):
        return pl.BlockSpec((1, rows, GROUP_W), lambda b, l: (b, 0, 0))

    in_specs = [row_spec] + [_resident(w.shape) for w in wts]
    return pl.pallas_call(
        _mixer_prompt_kernel,
        grid=(batch, steps),
        in_specs=in_specs,
        out_specs=[row_spec, state_spec(CONV_A_WIDTH - 1), state_spec(POOL_STATE),
                   state_spec(SHORT_CONV_WIDTH - 1)],
        out_shape=[jax.ShapeDtypeStruct((batch * seq, D_MODEL), f32),
                   jax.ShapeDtypeStruct((batch, CONV_A_WIDTH - 1, GROUP_W), f32),
                   jax.ShapeDtypeStruct((batch, POOL_STATE, GROUP_W), f32),
                   jax.ShapeDtypeStruct((batch, SHORT_CONV_WIDTH - 1, GROUP_W), f32)],
        scratch_shapes=[pltpu.VMEM((MIX_HALO + tm, MIX_IN_W), f32),
                        pltpu.VMEM((tm, GROUP_W), f32),
                        pltpu.VMEM((tm, GROUP_W), bf16),
                        pltpu.VMEM((tm, D_MODEL), bf16)],
        compiler_params=pltpu.CompilerParams(
            dimension_semantics=("arbitrary", "arbitrary"),
            vmem_limit_bytes=V7X_VMEM_LIMIT_BYTES),
        name="mixer_prompt",
    )(x, *wts)


def _mixer_sample_kernel(x_ref, gm_ref, wmi_ref, wmo_ref, caw_ref, cab_ref, nag_ref, nab_ref,
                         pw_ref, ps_ref, sng_ref, snb_ref, coef_ref, sb_ref, scw_ref, avg_ref,
                         sta_ref, stp_ref, sts_ref,
                         o_ref, sa_ref, sp_ref, ss_ref, v_ref,
                         xa, xp, xs, vs, *, dec_seq):
    nb = SAMPLE_SEQS
    rows = nb * dec_seq
    hist_a, hist_p, hist_s = CONV_A_WIDTH - 1, POOL_STATE, SHORT_CONV_WIDTH - 1

    x = x_ref[...]
    z = _dot(_rms(x, gm_ref[...]).astype(bf16), wmi_ref[...])

    def col(c):
        return z[:, c:c + GROUP_W]

    def seqs(a):
        return a.reshape(nb, dec_seq, GROUP_W)

    avg = avg_ref[...]

    xa[:, 0:hist_a, :] = sta_ref[...]
    xa[:, hist_a:hist_a + dec_seq, :] = seqs(col(COL_A_VAL) * jax.nn.sigmoid(col(COL_A_GATE)))
    acc = jnp.broadcast_to(cab_ref[...].reshape(1, 1, GROUP_W), (nb, dec_seq, GROUP_W))
    for k in range(CONV_A_WIDTH):
        acc = acc + caw_ref[k:k + 1, :].reshape(1, 1, GROUP_W) * xa[:, k:k + dec_seq, :]
    sa_ref[...] = xa[:, dec_seq:dec_seq + hist_a, :]
    ya = _head_layer_norm(acc.reshape(rows, GROUP_W), nag_ref[...], nab_ref[...], avg)
    ya = _silu(ya).astype(bf16)

    zb = col(COL_ZB)
    xp[:, 0:hist_p, :] = stp_ref[...]
    xp[:, hist_p:hist_p + dec_seq, :] = seqs(zb)
    run = xp[:, hist_p:hist_p + dec_seq, :]
    sums = []
    for j in range(1, max(POOL_WINDOWS)):
        run = run + xp[:, hist_p - j:hist_p - j + dec_seq, :]
        if j + 1 in POOL_WINDOWS:
            sums.append(run)
    sp_ref[...] = xp[:, dec_seq:dec_seq + hist_p, :]
    lane_group = lax.broadcasted_iota(jnp.int32, (nb, dec_seq, GROUP_W), 2) // HEAD_DIM
    pos = PAST_LEN + lax.broadcasted_iota(jnp.int32, (nb, dec_seq, GROUP_W), 1)
    cnt = jnp.minimum(pos + 1, jnp.left_shift(2, lane_group)).astype(f32)
    mean = (_pool_select(sums, lane_group) / cnt).reshape(rows, GROUP_W)
    yb = (_dot((mean - zb).astype(bf16), pw_ref[...]) * ps_ref[...]).astype(bf16)

    u = jax.nn.gelu(col(COL_U))
    v = _head_layer_norm(jax.nn.gelu(col(COL_V)), sng_ref[...], snb_ref[...], avg)
    v_ref[...] = seqs(v)
    vs[...] = seqs(v)
    t_idx = lax.broadcasted_iota(jnp.int32, (dec_seq, GROUP_W), 0)
    s = jnp.broadcast_to(sb_ref[...].reshape(1, dec_seq, GROUP_W), (nb, dec_seq, GROUP_W))
    for j in range(dec_seq):
        w_j = jnp.where(t_idx >= j, coef_ref[j], 0.0)
        s = s + w_j.reshape(1, dec_seq, GROUP_W) * vs[:, j:j + 1, :]
    yc = (u * s.reshape(rows, GROUP_W)).astype(bf16)

    xs[:, 0:hist_s, :] = sts_ref[...]
    xs[:, hist_s:hist_s + dec_seq, :] = seqs(col(COL_C_GATE) * col(COL_D_IN))
    conv = jnp.zeros((nb, dec_seq, GROUP_W), f32)
    for k in range(SHORT_CONV_WIDTH):
        conv = conv + scw_ref[k:k + 1, :].reshape(1, 1, GROUP_W) * xs[:, k:k + dec_seq, :]
    ss_ref[...] = xs[:, dec_seq:dec_seq + hist_s, :]
    yd = (col(COL_B_GATE) * conv.reshape(rows, GROUP_W)).astype(bf16)

    y = jnp.concatenate([ya, yb, yc, yd], axis=1)
    o_ref[...] = x + _dot(y, wmo_ref[...])


def _mixer_sample_call(x, st_a, st_p, st_s, dec_batch, dec_seq, wts):
    nb = SAMPLE_SEQS
    rows = nb * dec_seq
    row_spec = pl.BlockSpec((rows, D_MODEL), lambda i: (i, 0))

    def state_spec(r):
        return pl.BlockSpec((nb, r, GROUP_W), lambda i: (i, 0, 0))

    hist_a, hist_p, hist_s = CONV_A_WIDTH - 1, POOL_STATE, SHORT_CONV_WIDTH - 1
    in_specs = ([row_spec] + [_resident(w.shape) for w in wts]
                + [state_spec(hist_a), state_spec(hist_p), state_spec(hist_s)])
    return pl.pallas_call(
        functools.partial(_mixer_sample_kernel, dec_seq=dec_seq),
        grid=(dec_batch // nb,),
        in_specs=in_specs,
        out_specs=[row_spec, state_spec(hist_a), state_spec(hist_p), state_spec(hist_s),
                   state_spec(dec_seq)],
        out_shape=[jax.ShapeDtypeStruct((dec_batch * dec_seq, D_MODEL), f32),
                   jax.ShapeDtypeStruct((dec_batch, hist_a, GROUP_W), f32),
                   jax.ShapeDtypeStruct((dec_batch, hist_p, GROUP_W), f32),
                   jax.ShapeDtypeStruct((dec_batch, hist_s, GROUP_W), f32),
                   jax.ShapeDtypeStruct((dec_batch, dec_seq, GROUP_W), f32)],
        scratch_shapes=[pltpu.VMEM((nb, hist_a + dec_seq, GROUP_W), f32),
                        pltpu.VMEM((nb, hist_p + dec_seq, GROUP_W), f32),
                        pltpu.VMEM((nb, hist_s + dec_seq, GROUP_W), f32),
                        pltpu.VMEM((nb, dec_seq, GROUP_W), f32)],
        compiler_params=pltpu.CompilerParams(
            dimension_semantics=("arbitrary",), vmem_limit_bytes=V7X_VMEM_LIMIT_BYTES),
        name="mixer_sample",
    )(x, *wts, st_a, st_p, st_s)


def _row(v):
    return v.reshape(1, -1)


def _block_diag(w):
    g, c, d = w.shape
    eye = jnp.eye(g, dtype=w.dtype)
    return (eye[:, None, :, None] * w[:, :, None, :]).reshape(g * c, g * d)


def kernel(x_prompt, x_sample, p_prompt, p_sample, state_conv_a, state_pool, state_short_conv,
           norm_ffn1, w_ffn1_gate, w_ffn1_up, w_ffn1_down, norm_mix, w_mix_in,
           conv_a_w, conv_a_b, norm_a_g, norm_a_b, pool_w, pool_scale,
           sgu_norm_g, sgu_norm_b, sgu_w, sgu_b, short_conv_w, w_mix_out,
           norm_ffn2, w_ffn2_gate, w_ffn2_up, w_ffn2_down,
           norm_ple, w_ple_gate, w_ple_proj, norm_final):
    depth = w_mix_in.shape[0]
    batch, seq, _ = x_prompt.shape
    dec_batch, dec_seq, _ = x_sample.shape
    assert seq % MIX_TILE == 0 and MIX_TILE % CHUNK == 0 and dec_seq <= CHUNK
    assert dec_batch % SAMPLE_SEQS == 0 and dec_seq == V7X_SUBLANES
    assert (batch * seq) % FFN_TILE == 0 and D_FF % FFN_COL_CHUNK == 0

    avg = _block_diag(jnp.full((N_HEADS, HEAD_DIM, HEAD_DIM), 1.0 / HEAD_DIM, f32)).astype(bf16)
    head_of_channel = jnp.arange(GROUP_W) // HEAD_DIM

    xp = x_prompt.reshape(batch * seq, D_MODEL)
    xs = x_sample.reshape(dec_batch * dec_seq, D_MODEL)
    prompt_states, sample_states = [], []
    for i in range(depth):
        ffn1 = (_row(norm_ffn1[i]), w_ffn1_gate[i].astype(bf16), w_ffn1_up[i].astype(bf16),
                w_ffn1_down[i].astype(bf16))
        ffn2 = (_row(norm_ffn2[i]), w_ffn2_gate[i].astype(bf16), w_ffn2_up[i].astype(bf16),
                w_ffn2_down[i].astype(bf16))
        ple_w = (_row(norm_ple[i]), w_ple_gate[i].astype(bf16), w_ple_proj[i].astype(bf16),
                 _row(norm_final))
        mix_common = (_row(norm_mix[i]), w_mix_in[i].astype(bf16), w_mix_out[i].astype(bf16),
                      conv_a_w[i], _row(conv_a_b[i]), _row(norm_a_g[i]), _row(norm_a_b[i]),
                      _block_diag(pool_w[i]).astype(bf16), _row(pool_scale[i]),
                      _row(sgu_norm_g[i]), _row(sgu_norm_b[i]))
        sw_prompt = jnp.transpose(sgu_w[i], (1, 0, 2)).reshape(CHUNK, N_HEADS * CHUNK)
        sb_prompt = sgu_b[i].T[:, head_of_channel]
        sw_sample = jnp.transpose(sgu_w[i][:, :dec_seq, :dec_seq], (2, 1, 0))[:, :, head_of_channel]
        sb_sample = sgu_b[i][:, :dec_seq].T[:, head_of_channel]
        tail = (short_conv_w[i], avg)
        last = i == depth - 1

        xp = _ffn_call(xp, *ffn1)
        xp, sa, sp, ss = _mixer_prompt_call(
            xp, batch, seq, mix_common + (sw_prompt, sb_prompt) + tail)
        xp = _ffn_call(xp, *ffn2, ple=(p_prompt[i].reshape(batch * seq, D_PLE),) + ple_w,
                       final_norm=last)
        prompt_states.append((sa, sp, ss))

        xs = _ffn_call(xs, *ffn1)
        xs, sa, sp, ss, sv = _mixer_sample_call(
            xs, state_conv_a[i], state_pool[i], state_short_conv[i], dec_batch, dec_seq,
            mix_common + (sw_sample, sb_sample) + tail)
        xs = _ffn_call(xs, *ffn2, ple=(p_sample[i].reshape(dec_batch * dec_seq, D_PLE),) + ple_w,
                       final_norm=last)
        sample_states.append((sa, sp, ss, sv))

    def stack(states, j):
        return jnp.stack([s[j] for s in states])

    return (xp.reshape(batch, seq, D_MODEL), xs.reshape(dec_batch, dec_seq, D_MODEL),
            stack(prompt_states, 0), stack(sample_states, 0),
            stack(prompt_states, 1), stack(sample_states, 1),
            stack(prompt_states, 2), stack(sample_states, 2),
            stack(sample_states, 3))
```

```python
import functools

import jax
import jax.numpy as jnp
from jax import lax
from jax.experimental import pallas as pl
from jax.experimental.pallas import tpu as pltpu

D_MODEL = 1024
D_PLE = 256
D_FF = 2816
GROUP_W = 256
N_HEADS = 4
HEAD_DIM = 64
MIX_IN_W = 8 * GROUP_W
CONV_A_WIDTH = 31
POOL_WINDOWS = (2, 4, 8, 16)
POOL_STATE = 15
CHUNK = 128
SHORT_CONV_WIDTH = 3
EPS = 1e-6
PAST_LEN = 16384

V7X_SUBLANES = 8
V7X_VMEM_LIMIT_BYTES = 56 * 1024 * 1024

FFN_TILE = 512
FFN_COL_SPLITS = (0, 1536, D_FF)
MIX_TILE = 256
MIX_HALO = 32
MIX_ROWS = 64
SAMPLE_SEQS = 32

COL_A_VAL, COL_A_GATE, COL_ZB, COL_U, COL_V, COL_B_GATE, COL_C_GATE, COL_D_IN = (
    i * GROUP_W for i in range(8))

bf16 = jnp.bfloat16
f32 = jnp.float32


def _dot(a, b):
    return jnp.dot(a, b, preferred_element_type=f32)


def _rms(x, g):
    ms = jnp.mean(x * x, axis=-1, keepdims=True)
    return x * lax.rsqrt(ms + EPS) * g


def _group_mean(x, avg):
    hi = x.astype(bf16)
    lo = (x - hi.astype(f32)).astype(bf16)
    return _dot(hi, avg) + _dot(lo, avg)


def _head_layer_norm(x, g, b, avg):
    mu = _group_mean(x, avg)
    xc = x - mu
    var = _group_mean(xc * xc, avg)
    return xc * lax.rsqrt(var + EPS) * g + b


def _silu(x):
    return x * jax.nn.sigmoid(x)


def _pool_select(sums, lane_group):
    s2, s4, s8, s16 = sums
    return jnp.where(lane_group == 0, s2,
                     jnp.where(lane_group == 1, s4, jnp.where(lane_group == 2, s8, s16)))


def _ffn_body(x, g_ref, wg_ref, wu_ref, wd_ref, act_ref):
    h = _rms(x, g_ref[...]).astype(bf16)
    for lo, hi in zip(FFN_COL_SPLITS[:-1], FFN_COL_SPLITS[1:]):
        cs = slice(lo, hi)
        gate = _dot(h, wg_ref[:, cs])
        up = _dot(h, wu_ref[:, cs])
        act_ref[:, cs] = (_silu(gate) * up).astype(bf16)
    return x + 0.5 * _dot(act_ref[...], wd_ref[...])


def _ffn_kernel(x_ref, g_ref, wg_ref, wu_ref, wd_ref, o_ref, act_ref):
    o_ref[...] = _ffn_body(x_ref[...], g_ref, wg_ref, wu_ref, wd_ref, act_ref)


def _ffn_ple_kernel(x_ref, g_ref, wg_ref, wu_ref, wd_ref, p_ref, gp_ref, wpg_ref, wpp_ref,
                    gf_ref, o_ref, act_ref, *, final_norm):
    x = _ffn_body(x_ref[...], g_ref, wg_ref, wu_ref, wd_ref, act_ref)
    gate = jax.nn.sigmoid(_dot(_rms(x, gp_ref[...]).astype(bf16), wpg_ref[...]))
    x = x + gate * _dot(p_ref[...].astype(bf16), wpp_ref[...])
    if final_norm:
        x = _rms(x, gf_ref[...])
    o_ref[...] = x


def _resident(w, layer=None):
    if layer is None:
        return pl.BlockSpec(w.shape, lambda *_: (0,) * w.ndim, pipeline_mode=pl.Buffered(1))
    return pl.BlockSpec((None,) + w.shape[1:], lambda *_: (layer,) + (0,) * (w.ndim - 1),
                        pipeline_mode=pl.Buffered(1))


def _ffn_call(x, layer, ffn_w, ple=None, final_norm=False):
    m = x.shape[0]
    tile = min(FFN_TILE, m)
    row_spec = pl.BlockSpec((tile, D_MODEL), lambda i: (i, 0))
    in_specs = [row_spec] + [_resident(w, layer) for w in ffn_w]
    args = [x, *ffn_w]
    if ple is None:
        body = _ffn_kernel
        name = "ffn"
    else:
        p, ple_w, gf = ple
        body = functools.partial(_ffn_ple_kernel, final_norm=final_norm)
        name = "ffn_ple"
        in_specs += ([pl.BlockSpec((None, tile, D_PLE), lambda i: (layer, i, 0))]
                     + [_resident(w, layer) for w in ple_w] + [_resident(gf)])
        args += [p, *ple_w, gf]
    return pl.pallas_call(
        body,
        grid=(m // tile,),
        in_specs=in_specs,
        out_specs=row_spec,
        out_shape=jax.ShapeDtypeStruct((m, D_MODEL), f32),
        scratch_shapes=[pltpu.VMEM((tile, D_FF), bf16)],
        compiler_params=pltpu.CompilerParams(
            dimension_semantics=("arbitrary",), vmem_limit_bytes=V7X_VMEM_LIMIT_BYTES),
        name=name,
    )(*args)


def _mixer_prompt_kernel(x_ref, gm_ref, wmi_ref, wmo_ref, caw_ref, cab_ref, nag_ref, nab_ref,
                         pw_ref, ps_ref, sng_ref, snb_ref, sw_ref, sb_ref, scw_ref, avg_ref,
                         o_ref, sa_ref, sp_ref, ss_ref,
                         z_scr, c_scr, d_scr, y_scr):
    tm, halo, rb = MIX_TILE, MIX_HALO, MIX_ROWS
    step = pl.program_id(1)

    @pl.when(step == 0)
    def _():
        z_scr[0:halo, :] = jnp.zeros((halo, MIX_IN_W), f32)

    h = _rms(x_ref[...], gm_ref[...]).astype(bf16)
    z_scr[halo:halo + tm, :] = _dot(h, wmi_ref[...])

    for r in range(0, tm, rb):
        rows = pl.ds(halo + r, rb)
        a_val = z_scr[rows, COL_A_VAL:COL_A_VAL + GROUP_W]
        a_gate = z_scr[rows, COL_A_GATE:COL_A_GATE + GROUP_W]
        z_scr[rows, COL_A_VAL:COL_A_VAL + GROUP_W] = a_val * jax.nn.sigmoid(a_gate)
        c_gate = z_scr[rows, COL_C_GATE:COL_C_GATE + GROUP_W]
        d_in = z_scr[rows, COL_D_IN:COL_D_IN + GROUP_W]
        z_scr[rows, COL_D_IN:COL_D_IN + GROUP_W] = c_gate * d_in

    lane_group = lax.broadcasted_iota(jnp.int32, (rb, GROUP_W), 1) // HEAD_DIM
    win = jnp.left_shift(2, lane_group)
    row_iota = lax.broadcasted_iota(jnp.int32, (rb, GROUP_W), 0)

    for r in range(0, tm, rb):
        acc = jnp.broadcast_to(cab_ref[...], (rb, GROUP_W))
        for k in range(CONV_A_WIDTH):
            tap = z_scr[pl.ds(r + halo - (CONV_A_WIDTH - 1) + k, rb),
                        COL_A_VAL:COL_A_VAL + GROUP_W]
            acc = acc + caw_ref[k:k + 1, :] * tap
        c_scr[r:r + rb, :] = acc

        zb = z_scr[pl.ds(halo + r, rb), COL_ZB:COL_ZB + GROUP_W]
        run = zb
        sums = []
        for j in range(1, max(POOL_WINDOWS)):
            run = run + z_scr[pl.ds(halo + r - j, rb), COL_ZB:COL_ZB + GROUP_W]
            if j + 1 in POOL_WINDOWS:
                sums.append(run)
        pos = step * tm + r + row_iota
        cnt = jnp.minimum(pos + 1, win).astype(f32)
        mean = _pool_select(sums, lane_group) / cnt
        d_scr[r:r + rb, :] = (mean - zb).astype(bf16)

        rows = pl.ds(halo + r, rb)
        q0 = z_scr[pl.ds(halo + r - 2, rb), COL_D_IN:COL_D_IN + GROUP_W]
        q1 = z_scr[pl.ds(halo + r - 1, rb), COL_D_IN:COL_D_IN + GROUP_W]
        q2 = z_scr[rows, COL_D_IN:COL_D_IN + GROUP_W]
        conv = scw_ref[0:1, :] * q0 + scw_ref[1:2, :] * q1 + scw_ref[2:3, :] * q2
        b_gate = z_scr[rows, COL_B_GATE:COL_B_GATE + GROUP_W]
        y_scr[r:r + rb, 3 * GROUP_W:4 * GROUP_W] = (b_gate * conv).astype(bf16)

    avg = avg_ref[...]
    ya = _head_layer_norm(c_scr[...], nag_ref[...], nab_ref[...], avg)
    y_scr[:, 0:GROUP_W] = _silu(ya).astype(bf16)
    y_scr[:, GROUP_W:2 * GROUP_W] = (_dot(d_scr[...], pw_ref[...]) * ps_ref[...]).astype(bf16)

    t_idx = lax.broadcasted_iota(jnp.int32, (CHUNK, N_HEADS * CHUNK), 0)
    s_idx = lax.broadcasted_iota(jnp.int32, (CHUNK, N_HEADS * CHUNK), 1) % CHUNK
    w_cat = jnp.where(s_idx <= t_idx, sw_ref[...], 0.0).astype(bf16)
    head_of_lane = lax.broadcasted_iota(jnp.int32, (CHUNK, GROUP_W), 1) // HEAD_DIM
    for c in range(tm // CHUNK):
        rows = pl.ds(halo + c * CHUNK, CHUNK)
        u = jax.nn.gelu(z_scr[rows, COL_U:COL_U + GROUP_W])
        v = jax.nn.gelu(z_scr[rows, COL_V:COL_V + GROUP_W])
        v = _head_layer_norm(v, sng_ref[...], snb_ref[...], avg)
        v_stack = jnp.concatenate(
            [jnp.where(head_of_lane == hd, v, 0.0).astype(bf16) for hd in range(N_HEADS)], axis=0)
        s = _dot(w_cat, v_stack) + sb_ref[...]
        y_scr[c * CHUNK:(c + 1) * CHUNK, 2 * GROUP_W:3 * GROUP_W] = (u * s).astype(bf16)

    o_ref[...] = x_ref[...] + _dot(y_scr[...], wmo_ref[...])

    @pl.when(step == pl.num_programs(1) - 1)
    def _():
        sa_ref[0] = z_scr[pl.ds(halo + tm - (CONV_A_WIDTH - 1), CONV_A_WIDTH - 1),
                          COL_A_VAL:COL_A_VAL + GROUP_W]
        sp_ref[0] = z_scr[pl.ds(halo + tm - POOL_STATE, POOL_STATE), COL_ZB:COL_ZB + GROUP_W]
        ss_ref[0] = z_scr[pl.ds(halo + tm - (SHORT_CONV_WIDTH - 1), SHORT_CONV_WIDTH - 1),
                          COL_D_IN:COL_D_IN + GROUP_W]

    for col in (COL_A_VAL, COL_ZB, COL_D_IN):
        z_scr[0:halo, col:col + GROUP_W] = z_scr[tm:tm + halo, col:col + GROUP_W]


def _mixer_prompt_call(x, batch, seq, layer, wts, avg):
    tm = MIX_TILE
    steps = seq // tm
    row_spec = pl.BlockSpec((tm, D_MODEL), lambda b, l: (b * steps + l, 0))

    def state_spec(rows):
        return pl.BlockSpec((1, rows, GROUP_W), lambda b, l: (b, 0, 0))

    in_specs = [row_spec] + [_resident(w, layer) for w in wts] + [_resident(avg)]
    return pl.pallas_call(
        _mixer_prompt_kernel,
        grid=(batch, steps),
        in_specs=in_specs,
        out_specs=[row_spec, state_spec(CONV_A_WIDTH - 1), state_spec(POOL_STATE),
                   state_spec(SHORT_CONV_WIDTH - 1)],
        out_shape=[jax.ShapeDtypeStruct((batch * seq, D_MODEL), f32),
                   jax.ShapeDtypeStruct((batch, CONV_A_WIDTH - 1, GROUP_W), f32),
                   jax.ShapeDtypeStruct((batch, POOL_STATE, GROUP_W), f32),
                   jax.ShapeDtypeStruct((batch, SHORT_CONV_WIDTH - 1, GROUP_W), f32)],
        scratch_shapes=[pltpu.VMEM((MIX_HALO + tm, MIX_IN_W), f32),
                        pltpu.VMEM((tm, GROUP_W), f32),
                        pltpu.VMEM((tm, GROUP_W), bf16),
                        pltpu.VMEM((tm, D_MODEL), bf16)],
        compiler_params=pltpu.CompilerParams(
            dimension_semantics=("arbitrary", "arbitrary"),
            vmem_limit_bytes=V7X_VMEM_LIMIT_BYTES),
        name="mixer_prompt",
    )(x, *wts, avg)


def _mixer_sample_kernel(x_ref, gm_ref, wmi_ref, wmo_ref, caw_ref, cab_ref, nag_ref, nab_ref,
                         pw_ref, ps_ref, sng_ref, snb_ref, coef_ref, sb_ref, scw_ref, avg_ref,
                         sta_ref, stp_ref, sts_ref,
                         o_ref, sa_ref, sp_ref, ss_ref, v_ref,
                         xa, xp, xs, vs, *, dec_seq):
    nb = SAMPLE_SEQS
    rows = nb * dec_seq
    hist_a, hist_p, hist_s = CONV_A_WIDTH - 1, POOL_STATE, SHORT_CONV_WIDTH - 1

    x = x_ref[...]
    z = _dot(_rms(x, gm_ref[...]).astype(bf16), wmi_ref[...])

    def col(c):
        return z[:, c:c + GROUP_W]

    def seqs(a):
        return a.reshape(nb, dec_seq, GROUP_W)

    avg = avg_ref[...]

    xa[:, 0:hist_a, :] = sta_ref[...]
    xa[:, hist_a:hist_a + dec_seq, :] = seqs(col(COL_A_VAL) * jax.nn.sigmoid(col(COL_A_GATE)))
    acc = jnp.broadcast_to(cab_ref[...].reshape(1, 1, GROUP_W), (nb, dec_seq, GROUP_W))
    for k in range(CONV_A_WIDTH):
        acc = acc + caw_ref[k:k + 1, :].reshape(1, 1, GROUP_W) * xa[:, k:k + dec_seq, :]
    sa_ref[...] = xa[:, dec_seq:dec_seq + hist_a, :]
    ya = _head_layer_norm(acc.reshape(rows, GROUP_W), nag_ref[...], nab_ref[...], avg)
    ya = _silu(ya).astype(bf16)

    zb = col(COL_ZB)
    xp[:, 0:hist_p, :] = stp_ref[...]
    xp[:, hist_p:hist_p + dec_seq, :] = seqs(zb)
    run = xp[:, hist_p:hist_p + dec_seq, :]
    sums = []
    for j in range(1, max(POOL_WINDOWS)):
        run = run + xp[:, hist_p - j:hist_p - j + dec_seq, :]
        if j + 1 in POOL_WINDOWS:
            sums.append(run)
    sp_ref[...] = xp[:, dec_seq:dec_seq + hist_p, :]
    lane_group = lax.broadcasted_iota(jnp.int32, (nb, dec_seq, GROUP_W), 2) // HEAD_DIM
    pos = PAST_LEN + lax.broadcasted_iota(jnp.int32, (nb, dec_seq, GROUP_W), 1)
    cnt = jnp.minimum(pos + 1, jnp.left_shift(2, lane_group)).astype(f32)
    mean = (_pool_select(sums, lane_group) / cnt).reshape(rows, GROUP_W)
    yb = (_dot((mean - zb).astype(bf16), pw_ref[...]) * ps_ref[...]).astype(bf16)

    u = jax.nn.gelu(col(COL_U))
    v = _head_layer_norm(jax.nn.gelu(col(COL_V)), sng_ref[...], snb_ref[...], avg)
    v_ref[...] = seqs(v)
    vs[...] = seqs(v)
    t_idx = lax.broadcasted_iota(jnp.int32, (dec_seq, GROUP_W), 0)
    s = jnp.broadcast_to(sb_ref[...].reshape(1, dec_seq, GROUP_W), (nb, dec_seq, GROUP_W))
    for j in range(dec_seq):
        w_j = jnp.where(t_idx >= j, coef_ref[j], 0.0)
        s = s + w_j.reshape(1, dec_seq, GROUP_W) * vs[:, j:j + 1, :]
    yc = (u * s.reshape(rows, GROUP_W)).astype(bf16)

    xs[:, 0:hist_s, :] = sts_ref[...]
    xs[:, hist_s:hist_s + dec_seq, :] = seqs(col(COL_C_GATE) * col(COL_D_IN))
    conv = jnp.zeros((nb, dec_seq, GROUP_W), f32)
    for k in range(SHORT_CONV_WIDTH):
        conv = conv + scw_ref[k:k + 1, :].reshape(1, 1, GROUP_W) * xs[:, k:k + dec_seq, :]
    ss_ref[...] = xs[:, dec_seq:dec_seq + hist_s, :]
    yd = (col(COL_B_GATE) * conv.reshape(rows, GROUP_W)).astype(bf16)

    y = jnp.concatenate([ya, yb, yc, yd], axis=1)
    o_ref[...] = x + _dot(y, wmo_ref[...])


def _mixer_sample_call(x, st_a, st_p, st_s, dec_batch, dec_seq, layer, wts, avg):
    nb = SAMPLE_SEQS
    rows = nb * dec_seq
    row_spec = pl.BlockSpec((rows, D_MODEL), lambda i: (i, 0))

    def state_spec(r):
        return pl.BlockSpec((nb, r, GROUP_W), lambda i: (i, 0, 0))

    def state_in_spec(r):
        return pl.BlockSpec((None, nb, r, GROUP_W), lambda i: (layer, i, 0, 0))

    hist_a, hist_p, hist_s = CONV_A_WIDTH - 1, POOL_STATE, SHORT_CONV_WIDTH - 1
    in_specs = ([row_spec] + [_resident(w, layer) for w in wts] + [_resident(avg)]
                + [state_in_spec(hist_a), state_in_spec(hist_p), state_in_spec(hist_s)])
    return pl.pallas_call(
        functools.partial(_mixer_sample_kernel, dec_seq=dec_seq),
        grid=(dec_batch // nb,),
        in_specs=in_specs,
        out_specs=[row_spec, state_spec(hist_a), state_spec(hist_p), state_spec(hist_s),
                   state_spec(dec_seq)],
        out_shape=[jax.ShapeDtypeStruct((dec_batch * dec_seq, D_MODEL), f32),
                   jax.ShapeDtypeStruct((dec_batch, hist_a, GROUP_W), f32),
                   jax.ShapeDtypeStruct((dec_batch, hist_p, GROUP_W), f32),
                   jax.ShapeDtypeStruct((dec_batch, hist_s, GROUP_W), f32),
                   jax.ShapeDtypeStruct((dec_batch, dec_seq, GROUP_W), f32)],
        scratch_shapes=[pltpu.VMEM((nb, hist_a + dec_seq, GROUP_W), f32),
                        pltpu.VMEM((nb, hist_p + dec_seq, GROUP_W), f32),
                        pltpu.VMEM((nb, hist_s + dec_seq, GROUP_W), f32),
                        pltpu.VMEM((nb, dec_seq, GROUP_W), f32)],
        compiler_params=pltpu.CompilerParams(
            dimension_semantics=("arbitrary",), vmem_limit_bytes=V7X_VMEM_LIMIT_BYTES),
        name="mixer_sample",
    )(x, *wts, avg, st_a, st_p, st_s)


def _rows(v):
    return v.reshape(v.shape[0], 1, v.shape[-1])


def _block_diag(w):
    n, g, c, d = w.shape
    eye = jnp.eye(g, dtype=w.dtype)
    return (eye[None, :, None, :, None] * w[:, :, :, None, :]).reshape(n, g * c, g * d)


def kernel(x_prompt, x_sample, p_prompt, p_sample, state_conv_a, state_pool, state_short_conv,
           norm_ffn1, w_ffn1_gate, w_ffn1_up, w_ffn1_down, norm_mix, w_mix_in,
           conv_a_w, conv_a_b, norm_a_g, norm_a_b, pool_w, pool_scale,
           sgu_norm_g, sgu_norm_b, sgu_w, sgu_b, short_conv_w, w_mix_out,
           norm_ffn2, w_ffn2_gate, w_ffn2_up, w_ffn2_down,
           norm_ple, w_ple_gate, w_ple_proj, norm_final):
    depth = w_mix_in.shape[0]
    batch, seq, _ = x_prompt.shape
    dec_batch, dec_seq, _ = x_sample.shape
    assert seq % MIX_TILE == 0 and MIX_TILE % CHUNK == 0 and dec_seq <= CHUNK
    assert dec_batch % SAMPLE_SEQS == 0 and dec_seq == V7X_SUBLANES
    assert (batch * seq) % FFN_TILE == 0 and all(c % 256 == 0 for c in FFN_COL_SPLITS)

    avg = _block_diag(jnp.full((1, N_HEADS, HEAD_DIM, HEAD_DIM), 1.0 / HEAD_DIM, f32))[0]
    avg = avg.astype(bf16)

    ffn1 = (_rows(norm_ffn1), w_ffn1_gate.astype(bf16), w_ffn1_up.astype(bf16),
            w_ffn1_down.astype(bf16))
    ffn2 = (_rows(norm_ffn2), w_ffn2_gate.astype(bf16), w_ffn2_up.astype(bf16),
            w_ffn2_down.astype(bf16))
    ple_w = (_rows(norm_ple), w_ple_gate.astype(bf16), w_ple_proj.astype(bf16))
    gf = norm_final.reshape(1, D_MODEL)
    mix_head = (_rows(norm_mix), w_mix_in.astype(bf16), w_mix_out.astype(bf16),
                conv_a_w, _rows(conv_a_b), _rows(norm_a_g), _rows(norm_a_b),
                _block_diag(pool_w).astype(bf16), _rows(pool_scale),
                _rows(sgu_norm_g), _rows(sgu_norm_b))
    sw_prompt = jnp.transpose(sgu_w, (0, 2, 1, 3)).reshape(depth, CHUNK, N_HEADS * CHUNK)
    sb_prompt = jnp.repeat(jnp.swapaxes(sgu_b, 1, 2), HEAD_DIM, axis=2)
    sw_sample = jnp.repeat(jnp.transpose(sgu_w[:, :, :dec_seq, :dec_seq], (0, 3, 2, 1)),
                           HEAD_DIM, axis=3)
    sb_sample = sb_prompt[:, :dec_seq]
    mix_prompt_w = mix_head + (sw_prompt, sb_prompt, short_conv_w)
    mix_sample_w = mix_head + (sw_sample, sb_sample, short_conv_w)

    pp = p_prompt.reshape(depth, batch * seq, D_PLE)
    ps = p_sample.reshape(depth, dec_batch * dec_seq, D_PLE)
    xp = x_prompt.reshape(batch * seq, D_MODEL)
    xs = x_sample.reshape(dec_batch * dec_seq, D_MODEL)
    prompt_states, sample_states = [], []
    for i in range(depth):
        last = i == depth - 1

        xp = _ffn_call(xp, i, ffn1)
        xp, sa, sp, ss = _mixer_prompt_call(xp, batch, seq, i, mix_prompt_w, avg)
        xp = _ffn_call(xp, i, ffn2, ple=(pp, ple_w, gf), final_norm=last)
        prompt_states.append((sa, sp, ss))

        xs = _ffn_call(xs, i, ffn1)
        xs, sa, sp, ss, sv = _mixer_sample_call(
            xs, state_conv_a, state_pool, state_short_conv, dec_batch, dec_seq, i,
            mix_sample_w, avg)
        xs = _ffn_call(xs, i, ffn2, ple=(ps, ple_w, gf), final_norm=last)
        sample_states.append((sa, sp, ss, sv))

    def stack(states, j):
        return jnp.stack([s[j] for s in states])

    return (xp.reshape(batch, seq, D_MODEL), xs.reshape(dec_batch, dec_seq, D_MODEL),
            stack(prompt_states, 0), stack(sample_states, 0),
            stack(prompt_states, 1), stack(sample_states, 1),
            stack(prompt_states, 2), stack(sample_states, 2),
            stack(sample_states, 3))
```

```python
import functools

import jax
import jax.numpy as jnp
from jax import lax
from jax.experimental import pallas as pl
from jax.experimental.pallas import tpu as pltpu

D_MODEL = 1024
D_PLE = 256
D_FF = 2816
GROUP_W = 256
N_HEADS = 4
HEAD_DIM = 64
MIX_IN_W = 8 * GROUP_W
CONV_A_WIDTH = 31
POOL_WINDOWS = (2, 4, 8, 16)
POOL_STATE = 15
CHUNK = 128
SHORT_CONV_WIDTH = 3
EPS = 1e-6
PAST_LEN = 16384

V7X_SUBLANES = 8
V7X_VMEM_LIMIT_BYTES = 56 * 1024 * 1024

FFN_TILE = 512
FFN_COL_SPLITS = (0, 1536, D_FF)
MIX_TILE = 256
MIX_SEQS = 2
MIX_HALO = 32
MIX_ROWS = 64
SAMPLE_SEQS = 32

COL_A_VAL, COL_A_GATE, COL_ZB, COL_U, COL_V, COL_B_GATE, COL_C_GATE, COL_D_IN = (
    i * GROUP_W for i in range(8))

bf16 = jnp.bfloat16
f32 = jnp.float32


def _dot(a, b):
    return jnp.dot(a, b, preferred_element_type=f32)


def _rms(x, g):
    ms = jnp.mean(x * x, axis=-1, keepdims=True)
    return x * lax.rsqrt(ms + EPS) * g


def _group_mean(x, avg):
    hi = x.astype(bf16)
    lo = (x - hi.astype(f32)).astype(bf16)
    return _dot(hi, avg) + _dot(lo, avg)


def _head_layer_norm(x, g, b, avg):
    mu = _group_mean(x, avg)
    xc = x - mu
    var = _group_mean(xc * xc, avg)
    return xc * lax.rsqrt(var + EPS) * g + b


def _silu(x):
    return x * jax.nn.sigmoid(x)


def _pool_select(sums, lane_group):
    s2, s4, s8, s16 = sums
    return jnp.where(lane_group == 0, s2,
                     jnp.where(lane_group == 1, s4, jnp.where(lane_group == 2, s8, s16)))


def _ffn_body(x, g_ref, wg_ref, wu_ref, wd_ref, act_ref):
    h = _rms(x, g_ref[...]).astype(bf16)
    for lo, hi in zip(FFN_COL_SPLITS[:-1], FFN_COL_SPLITS[1:]):
        cs = slice(lo, hi)
        gate = _dot(h, wg_ref[:, cs])
        up = _dot(h, wu_ref[:, cs])
        act_ref[:, cs] = (_silu(gate) * up).astype(bf16)
    return x + 0.5 * _dot(act_ref[...], wd_ref[...])


def _ffn_kernel(x_ref, g_ref, wg_ref, wu_ref, wd_ref, o_ref, act_ref):
    o_ref[...] = _ffn_body(x_ref[...], g_ref, wg_ref, wu_ref, wd_ref, act_ref)


def _ffn_ple_kernel(x_ref, g_ref, wg_ref, wu_ref, wd_ref, p_ref, gp_ref, wpg_ref, wpp_ref,
                    gf_ref, o_ref, act_ref, *, final_norm):
    x = _ffn_body(x_ref[...], g_ref, wg_ref, wu_ref, wd_ref, act_ref)
    gate = jax.nn.sigmoid(_dot(_rms(x, gp_ref[...]).astype(bf16), wpg_ref[...]))
    x = x + gate * _dot(p_ref[...].astype(bf16), wpp_ref[...])
    if final_norm:
        x = _rms(x, gf_ref[...])
    o_ref[...] = x


def _resident(w, layer=None):
    if layer is None:
        return pl.BlockSpec(w.shape, lambda *_: (0,) * w.ndim, pipeline_mode=pl.Buffered(1))
    return pl.BlockSpec((None,) + w.shape[1:], lambda *_: (layer,) + (0,) * (w.ndim - 1),
                        pipeline_mode=pl.Buffered(1))


def _ffn_call(x, layer, ffn_w, ple=None, final_norm=False):
    m = x.shape[0]
    tile = min(FFN_TILE, m)
    row_spec = pl.BlockSpec((tile, D_MODEL), lambda i: (i, 0))
    in_specs = [row_spec] + [_resident(w, layer) for w in ffn_w]
    args = [x, *ffn_w]
    if ple is None:
        body = _ffn_kernel
        name = "ffn"
    else:
        p, ple_w, gf = ple
        body = functools.partial(_ffn_ple_kernel, final_norm=final_norm)
        name = "ffn_ple"
        in_specs += ([pl.BlockSpec((None, tile, D_PLE), lambda i: (layer, i, 0))]
                     + [_resident(w, layer) for w in ple_w] + [_resident(gf)])
        args += [p, *ple_w, gf]
    return pl.pallas_call(
        body,
        grid=(m // tile,),
        in_specs=in_specs,
        out_specs=row_spec,
        out_shape=jax.ShapeDtypeStruct((m, D_MODEL), f32),
        scratch_shapes=[pltpu.VMEM((tile, D_FF), bf16)],
        compiler_params=pltpu.CompilerParams(
            dimension_semantics=("arbitrary",), vmem_limit_bytes=V7X_VMEM_LIMIT_BYTES),
        name=name,
    )(*args)


def _mixer_prompt_kernel(x_ref, gm_ref, wmi_ref, wmo_ref, caw_ref, cab_ref, nag_ref, nab_ref,
                         pw_ref, ps_ref, sng_ref, snb_ref, sw_ref, sb_ref, scw_ref, avg_ref,
                         o_ref, sa_ref, sp_ref, ss_ref,
                         z_all, sh_all, c_all, d_all, y_all):
    tm, halo = MIX_TILE, MIX_HALO
    sub, half = V7X_SUBLANES, GROUP_W // 2
    step = pl.program_id(1)
    weights = (gm_ref, wmi_ref, wmo_ref, caw_ref, cab_ref, nag_ref, nab_ref, pw_ref, ps_ref,
               sng_ref, snb_ref, sw_ref, sb_ref, scw_ref, avg_ref)

    @pl.when(step == 0)
    def _():
        z_all[:, 0:halo, :] = jnp.zeros((MIX_SEQS, halo, MIX_IN_W), f32)
        sh_all[:, :, 0:halo, :] = jnp.zeros((MIX_SEQS, sub - 1, halo, 2 * GROUP_W), f32)

    for s in range(MIX_SEQS):
        h = _rms(x_ref[s], gm_ref[...]).astype(bf16)
        z_all[s, halo:halo + tm, :] = _dot(h, wmi_ref[...])
    for s in range(MIX_SEQS):
        _mixer_prompt_tile(step, x_ref.at[s], o_ref.at[s], weights, z_all.at[s], sh_all.at[s],
                           c_all.at[s], d_all.at[s], y_all.at[s])

    @pl.when(step == pl.num_programs(1) - 1)
    def _():
        for s in range(MIX_SEQS):
            z_scr = z_all.at[s]
            sa_ref[s] = z_scr[pl.ds(halo + tm - (CONV_A_WIDTH - 1), CONV_A_WIDTH - 1),
                              COL_A_VAL:COL_A_VAL + GROUP_W]
            sp_ref[s] = z_scr[pl.ds(halo + tm - POOL_STATE, POOL_STATE),
                              COL_ZB:COL_ZB + GROUP_W]
            ss_ref[s] = z_scr[pl.ds(halo + tm - (SHORT_CONV_WIDTH - 1), SHORT_CONV_WIDTH - 1),
                              COL_D_IN:COL_D_IN + GROUP_W]

    for s in range(MIX_SEQS):
        z_scr, sh_scr = z_all.at[s], sh_all.at[s]
        for col in (COL_A_VAL, COL_ZB, COL_D_IN):
            z_scr[0:halo, col:col + GROUP_W] = z_scr[tm:tm + halo, col:col + GROUP_W]
        sh_scr[0:3, 0:halo, :] = sh_scr[0:3, tm:tm + halo, :]
        for lanes in (slice(0, GROUP_W), slice(GROUP_W + half, 2 * GROUP_W)):
            sh_scr[3:sub - 1, 0:halo, lanes] = sh_scr[3:sub - 1, tm:tm + halo, lanes]


def _mixer_prompt_tile(step, x_ref, o_ref, weights, z_scr, sh_scr, c_scr, d_scr, y_scr):
    (gm_ref, wmi_ref, wmo_ref, caw_ref, cab_ref, nag_ref, nab_ref, pw_ref, ps_ref,
     sng_ref, snb_ref, sw_ref, sb_ref, scw_ref, avg_ref) = weights
    tm, halo, rb = MIX_TILE, MIX_HALO, MIX_ROWS
    sub, half = V7X_SUBLANES, GROUP_W // 2

    for r in range(0, tm, rb):
        rows = pl.ds(halo + r, rb)
        a_val = z_scr[rows, COL_A_VAL:COL_A_VAL + GROUP_W]
        a_gate = z_scr[rows, COL_A_GATE:COL_A_GATE + GROUP_W]
        z_scr[rows, COL_A_VAL:COL_A_VAL + GROUP_W] = a_val * jax.nn.sigmoid(a_gate)
        c_gate = z_scr[rows, COL_C_GATE:COL_C_GATE + GROUP_W]
        d_in = z_scr[rows, COL_D_IN:COL_D_IN + GROUP_W]
        z_scr[rows, COL_D_IN:COL_D_IN + GROUP_W] = c_gate * d_in
        for b in range(1, sub):
            late = pl.ds(halo + r - b, rb)
            sh_scr[b - 1, rows, 0:GROUP_W] = z_scr[late, COL_A_VAL:COL_A_VAL + GROUP_W]
            if b < 4:
                sh_scr[b - 1, rows, GROUP_W:2 * GROUP_W] = z_scr[late, COL_ZB:COL_ZB + GROUP_W]
            else:
                sh_scr[b - 1, rows, GROUP_W + half:2 * GROUP_W] = (
                    z_scr[late, COL_ZB + half:COL_ZB + GROUP_W])

    lane_group = lax.broadcasted_iota(jnp.int32, (rb, GROUP_W), 1) // HEAD_DIM
    win = jnp.left_shift(2, lane_group)
    row_iota = lax.broadcasted_iota(jnp.int32, (rb, GROUP_W), 0)
    first_head = lax.broadcasted_iota(jnp.int32, (rb, half), 1) < HEAD_DIM

    for r in range(0, tm, rb):
        def delayed(col, lanes, b, back):
            at = pl.ds(halo + r - back, rb)
            if b == 0:
                return z_scr[at, col + lanes.start:col + lanes.stop]
            base = 0 if col == COL_A_VAL else GROUP_W
            return sh_scr[b - 1, at, base + lanes.start:base + lanes.stop]

        acc = jnp.broadcast_to(cab_ref[...], (rb, GROUP_W))
        for j in range(CONV_A_WIDTH):
            k = CONV_A_WIDTH - 1 - j
            tap = delayed(COL_A_VAL, slice(0, GROUP_W), j % sub, j - j % sub)
            acc = acc + caw_ref[k:k + 1, :] * tap
        c_scr[r:r + rb, :] = acc

        lo, hi = slice(0, half), slice(half, GROUP_W)
        s2 = delayed(COL_ZB, lo, 0, 0) + delayed(COL_ZB, lo, 1, 0)
        s4 = s2 + delayed(COL_ZB, lo, 2, 0) + delayed(COL_ZB, lo, 3, 0)
        s8 = delayed(COL_ZB, hi, 0, 0)
        for b in range(1, sub):
            s8 = s8 + delayed(COL_ZB, hi, b, 0)
        s16 = s8
        for b in range(sub):
            s16 = s16 + delayed(COL_ZB, hi, b, sub)
        sums = jnp.concatenate(
            [jnp.where(first_head, s2, s4), jnp.where(first_head, s8, s16)], axis=1)
        zb = z_scr[pl.ds(halo + r, rb), COL_ZB:COL_ZB + GROUP_W]
        pos = step * tm + r + row_iota
        cnt = jnp.minimum(pos + 1, win).astype(f32)
        d_scr[r:r + rb, :] = (sums / cnt - zb).astype(bf16)

        rows = pl.ds(halo + r, rb)
        q0 = z_scr[pl.ds(halo + r - 2, rb), COL_D_IN:COL_D_IN + GROUP_W]
        q1 = z_scr[pl.ds(halo + r - 1, rb), COL_D_IN:COL_D_IN + GROUP_W]
        q2 = z_scr[rows, COL_D_IN:COL_D_IN + GROUP_W]
        conv = scw_ref[0:1, :] * q0 + scw_ref[1:2, :] * q1 + scw_ref[2:3, :] * q2
        b_gate = z_scr[rows, COL_B_GATE:COL_B_GATE + GROUP_W]
        y_scr[r:r + rb, 3 * GROUP_W:4 * GROUP_W] = (b_gate * conv).astype(bf16)

    avg = avg_ref[...]
    ya = _head_layer_norm(c_scr[...], nag_ref[...], nab_ref[...], avg)
    y_scr[:, 0:GROUP_W] = _silu(ya).astype(bf16)
    y_scr[:, GROUP_W:2 * GROUP_W] = (_dot(d_scr[...], pw_ref[...]) * ps_ref[...]).astype(bf16)

    t_idx = lax.broadcasted_iota(jnp.int32, (CHUNK, N_HEADS * CHUNK), 0)
    s_idx = lax.broadcasted_iota(jnp.int32, (CHUNK, N_HEADS * CHUNK), 1) % CHUNK
    w_cat = jnp.where(s_idx <= t_idx, sw_ref[...], 0.0).astype(bf16)
    head_of_lane = lax.broadcasted_iota(jnp.int32, (CHUNK, GROUP_W), 1) // HEAD_DIM
    for c in range(tm // CHUNK):
        rows = pl.ds(halo + c * CHUNK, CHUNK)
        u = jax.nn.gelu(z_scr[rows, COL_U:COL_U + GROUP_W])
        v = jax.nn.gelu(z_scr[rows, COL_V:COL_V + GROUP_W])
        v = _head_layer_norm(v, sng_ref[...], snb_ref[...], avg)
        v_stack = jnp.concatenate(
            [jnp.where(head_of_lane == hd, v, 0.0).astype(bf16) for hd in range(N_HEADS)], axis=0)
        s = _dot(w_cat, v_stack) + sb_ref[...]
        y_scr[c * CHUNK:(c + 1) * CHUNK, 2 * GROUP_W:3 * GROUP_W] = (u * s).astype(bf16)

    o_ref[...] = x_ref[...] + _dot(y_scr[...], wmo_ref[...])


def _mixer_prompt_call(x, batch, seq, layer, wts, avg):
    tm, ns = MIX_TILE, MIX_SEQS
    x = x.reshape(batch, seq, D_MODEL)
    row_spec = pl.BlockSpec((ns, tm, D_MODEL), lambda b, l: (b, l, 0))

    def state_spec(rows):
        return pl.BlockSpec((ns, rows, GROUP_W), lambda b, l: (b, 0, 0))

    in_specs = [row_spec] + [_resident(w, layer) for w in wts] + [_resident(avg)]
    out = pl.pallas_call(
        _mixer_prompt_kernel,
        grid=(batch // ns, seq // tm),
        in_specs=in_specs,
        out_specs=[row_spec, state_spec(CONV_A_WIDTH - 1), state_spec(POOL_STATE),
                   state_spec(SHORT_CONV_WIDTH - 1)],
        out_shape=[jax.ShapeDtypeStruct((batch, seq, D_MODEL), f32),
                   jax.ShapeDtypeStruct((batch, CONV_A_WIDTH - 1, GROUP_W), f32),
                   jax.ShapeDtypeStruct((batch, POOL_STATE, GROUP_W), f32),
                   jax.ShapeDtypeStruct((batch, SHORT_CONV_WIDTH - 1, GROUP_W), f32)],
        scratch_shapes=[pltpu.VMEM((ns, MIX_HALO + tm, MIX_IN_W), f32),
                        pltpu.VMEM((ns, V7X_SUBLANES - 1, MIX_HALO + tm, 2 * GROUP_W), f32),
                        pltpu.VMEM((ns, tm, GROUP_W), f32),
                        pltpu.VMEM((ns, tm, GROUP_W), bf16),
                        pltpu.VMEM((ns, tm, D_MODEL), bf16)],
        compiler_params=pltpu.CompilerParams(
            dimension_semantics=("arbitrary", "arbitrary"),
            vmem_limit_bytes=V7X_VMEM_LIMIT_BYTES),
        name="mixer_prompt",
    )(x, *wts, avg)
    return (out[0].reshape(batch * seq, D_MODEL),) + tuple(out[1:])


def _mixer_sample_kernel(x_ref, gm_ref, wmi_ref, wmo_ref, caw_ref, cab_ref, nag_ref, nab_ref,
                         pw_ref, ps_ref, sng_ref, snb_ref, coef_ref, sb_ref, scw_ref, avg_ref,
                         sta_ref, stp_ref, sts_ref,
                         o_ref, sa_ref, sp_ref, ss_ref, v_ref,
                         xa, xp, xs, vs, *, dec_seq):
    nb = SAMPLE_SEQS
    rows = nb * dec_seq
    hist_a, hist_p, hist_s = CONV_A_WIDTH - 1, POOL_STATE, SHORT_CONV_WIDTH - 1

    x = x_ref[...]
    z = _dot(_rms(x, gm_ref[...]).astype(bf16), wmi_ref[...])

    def col(c):
        return z[:, c:c + GROUP_W]

    def seqs(a):
        return a.reshape(nb, dec_seq, GROUP_W)

    avg = avg_ref[...]

    xa[:, 0:hist_a, :] = sta_ref[...]
    xa[:, hist_a:hist_a + dec_seq, :] = seqs(col(COL_A_VAL) * jax.nn.sigmoid(col(COL_A_GATE)))
    acc = jnp.broadcast_to(cab_ref[...].reshape(1, 1, GROUP_W), (nb, dec_seq, GROUP_W))
    for k in range(CONV_A_WIDTH):
        acc = acc + caw_ref[k:k + 1, :].reshape(1, 1, GROUP_W) * xa[:, k:k + dec_seq, :]
    sa_ref[...] = xa[:, dec_seq:dec_seq + hist_a, :]
    ya = _head_layer_norm(acc.reshape(rows, GROUP_W), nag_ref[...], nab_ref[...], avg)
    ya = _silu(ya).astype(bf16)

    zb = col(COL_ZB)
    xp[:, 0:hist_p, :] = stp_ref[...]
    xp[:, hist_p:hist_p + dec_seq, :] = seqs(zb)
    run = xp[:, hist_p:hist_p + dec_seq, :]
    sums = []
    for j in range(1, max(POOL_WINDOWS)):
        run = run + xp[:, hist_p - j:hist_p - j + dec_seq, :]
        if j + 1 in POOL_WINDOWS:
            sums.append(run)
    sp_ref[...] = xp[:, dec_seq:dec_seq + hist_p, :]
    lane_group = lax.broadcasted_iota(jnp.int32, (nb, dec_seq, GROUP_W), 2) // HEAD_DIM
    pos = PAST_LEN + lax.broadcasted_iota(jnp.int32, (nb, dec_seq, GROUP_W), 1)
    cnt = jnp.minimum(pos + 1, jnp.left_shift(2, lane_group)).astype(f32)
    mean = (_pool_select(sums, lane_group) / cnt).reshape(rows, GROUP_W)
    yb = (_dot((mean - zb).astype(bf16), pw_ref[...]) * ps_ref[...]).astype(bf16)

    u = jax.nn.gelu(col(COL_U))
    v = _head_layer_norm(jax.nn.gelu(col(COL_V)), sng_ref[...], snb_ref[...], avg)
    v_ref[...] = seqs(v)
    vs[...] = seqs(v)
    t_idx = lax.broadcasted_iota(jnp.int32, (dec_seq, GROUP_W), 0)
    s = jnp.broadcast_to(sb_ref[...].reshape(1, dec_seq, GROUP_W), (nb, dec_seq, GROUP_W))
    for j in range(dec_seq):
        w_j = jnp.where(t_idx >= j, coef_ref[j], 0.0)
        s = s + w_j.reshape(1, dec_seq, GROUP_W) * vs[:, j:j + 1, :]
    yc = (u * s.reshape(rows, GROUP_W)).astype(bf16)

    xs[:, 0:hist_s, :] = sts_ref[...]
    xs[:, hist_s:hist_s + dec_seq, :] = seqs(col(COL_C_GATE) * col(COL_D_IN))
    conv = jnp.zeros((nb, dec_seq, GROUP_W), f32)
    for k in range(SHORT_CONV_WIDTH):
        conv = conv + scw_ref[k:k + 1, :].reshape(1, 1, GROUP_W) * xs[:, k:k + dec_seq, :]
    ss_ref[...] = xs[:, dec_seq:dec_seq + hist_s, :]
    yd = (col(COL_B_GATE) * conv.reshape(rows, GROUP_W)).astype(bf16)

    y = jnp.concatenate([ya, yb, yc, yd], axis=1)
    o_ref[...] = x + _dot(y, wmo_ref[...])


def _mixer_sample_call(x, st_a, st_p, st_s, dec_batch, dec_seq, layer, wts, avg):
    nb = SAMPLE_SEQS
    rows = nb * dec_seq
    row_spec = pl.BlockSpec((rows, D_MODEL), lambda i: (i, 0))

    def state_spec(r):
        return pl.BlockSpec((nb, r, GROUP_W), lambda i: (i, 0, 0))

    def state_in_spec(r):
        return pl.BlockSpec((None, nb, r, GROUP_W), lambda i: (layer, i, 0, 0))

    hist_a, hist_p, hist_s = CONV_A_WIDTH - 1, POOL_STATE, SHORT_CONV_WIDTH - 1
    in_specs = ([row_spec] + [_resident(w, layer) for w in wts] + [_resident(avg)]
                + [state_in_spec(hist_a), state_in_spec(hist_p), state_in_spec(hist_s)])
    return pl.pallas_call(
        functools.partial(_mixer_sample_kernel, dec_seq=dec_seq),
        grid=(dec_batch // nb,),
        in_specs=in_specs,
        out_specs=[row_spec, state_spec(hist_a), state_spec(hist_p), state_spec(hist_s),
                   state_spec(dec_seq)],
        out_shape=[jax.ShapeDtypeStruct((dec_batch * dec_seq, D_MODEL), f32),
                   jax.ShapeDtypeStruct((dec_batch, hist_a, GROUP_W), f32),
                   jax.ShapeDtypeStruct((dec_batch, hist_p, GROUP_W), f32),
                   jax.ShapeDtypeStruct((dec_batch, hist_s, GROUP_W), f32),
                   jax.ShapeDtypeStruct((dec_batch, dec_seq, GROUP_W), f32)],
        scratch_shapes=[pltpu.VMEM((nb, hist_a + dec_seq, GROUP_W), f32),
                        pltpu.VMEM((nb, hist_p + dec_seq, GROUP_W), f32),
                        pltpu.VMEM((nb, hist_s + dec_seq, GROUP_W), f32),
                        pltpu.VMEM((nb, dec_seq, GROUP_W), f32)],
        compiler_params=pltpu.CompilerParams(
            dimension_semantics=("arbitrary",), vmem_limit_bytes=V7X_VMEM_LIMIT_BYTES),
        name="mixer_sample",
    )(x, *wts, avg, st_a, st_p, st_s)


def _rows(v):
    return v.reshape(v.shape[0], 1, v.shape[-1])


def _block_diag(w):
    n, g, c, d = w.shape
    eye = jnp.eye(g, dtype=w.dtype)
    return (eye[None, :, None, :, None] * w[:, :, :, None, :]).reshape(n, g * c, g * d)


def kernel(x_prompt, x_sample, p_prompt, p_sample, state_conv_a, state_pool, state_short_conv,
           norm_ffn1, w_ffn1_gate, w_ffn1_up, w_ffn1_down, norm_mix, w_mix_in,
           conv_a_w, conv_a_b, norm_a_g, norm_a_b, pool_w, pool_scale,
           sgu_norm_g, sgu_norm_b, sgu_w, sgu_b, short_conv_w, w_mix_out,
           norm_ffn2, w_ffn2_gate, w_ffn2_up, w_ffn2_down,
           norm_ple, w_ple_gate, w_ple_proj, norm_final):
    depth = w_mix_in.shape[0]
    batch, seq, _ = x_prompt.shape
    dec_batch, dec_seq, _ = x_sample.shape
    assert seq % MIX_TILE == 0 and MIX_TILE % CHUNK == 0 and dec_seq <= CHUNK
    assert batch % MIX_SEQS == 0
    assert dec_batch % SAMPLE_SEQS == 0 and dec_seq == V7X_SUBLANES
    assert (batch * seq) % FFN_TILE == 0 and all(c % 256 == 0 for c in FFN_COL_SPLITS)

    avg = _block_diag(jnp.full((1, N_HEADS, HEAD_DIM, HEAD_DIM), 1.0 / HEAD_DIM, f32))[0]
    avg = avg.astype(bf16)

    ffn1 = (_rows(norm_ffn1), w_ffn1_gate.astype(bf16), w_ffn1_up.astype(bf16),
            w_ffn1_down.astype(bf16))
    ffn2 = (_rows(norm_ffn2), w_ffn2_gate.astype(bf16), w_ffn2_up.astype(bf16),
            w_ffn2_down.astype(bf16))
    ple_w = (_rows(norm_ple), w_ple_gate.astype(bf16), w_ple_proj.astype(bf16))
    gf = norm_final.reshape(1, D_MODEL)
    mix_head = (_rows(norm_mix), w_mix_in.astype(bf16), w_mix_out.astype(bf16),
                conv_a_w, _rows(conv_a_b), _rows(norm_a_g), _rows(norm_a_b),
                _block_diag(pool_w).astype(bf16), _rows(pool_scale),
                _rows(sgu_norm_g), _rows(sgu_norm_b))
    sw_prompt = jnp.transpose(sgu_w, (0, 2, 1, 3)).reshape(depth, CHUNK, N_HEADS * CHUNK)
    sb_prompt = jnp.repeat(jnp.swapaxes(sgu_b, 1, 2), HEAD_DIM, axis=2)
    sw_sample = jnp.repeat(jnp.transpose(sgu_w[:, :, :dec_seq, :dec_seq], (0, 3, 2, 1)),
                           HEAD_DIM, axis=3)
    sb_sample = sb_prompt[:, :dec_seq]
    mix_prompt_w = mix_head + (sw_prompt, sb_prompt, short_conv_w)
    mix_sample_w = mix_head + (sw_sample, sb_sample, short_conv_w)

    pp = p_prompt.reshape(depth, batch * seq, D_PLE)
    ps = p_sample.reshape(depth, dec_batch * dec_seq, D_PLE)
    xp = x_prompt.reshape(batch * seq, D_MODEL)
    xs = x_sample.reshape(dec_batch * dec_seq, D_MODEL)
    prompt_states, sample_states = [], []
    for i in range(depth):
        last = i == depth - 1

        xp = _ffn_call(xp, i, ffn1)
        xp, sa, sp, ss = _mixer_prompt_call(xp, batch, seq, i, mix_prompt_w, avg)
        xp = _ffn_call(xp, i, ffn2, ple=(pp, ple_w, gf), final_norm=last)
        prompt_states.append((sa, sp, ss))

        xs = _ffn_call(xs, i, ffn1)
        xs, sa, sp, ss, sv = _mixer_sample_call(
            xs, state_conv_a, state_pool, state_short_conv, dec_batch, dec_seq, i,
            mix_sample_w, avg)
        xs = _ffn_call(xs, i, ffn2, ple=(ps, ple_w, gf), final_norm=last)
        sample_states.append((sa, sp, ss, sv))

    def stack(states, j):
        return jnp.stack([s[j] for s in states])

    return (xp.reshape(batch, seq, D_MODEL), xs.reshape(dec_batch, dec_seq, D_MODEL),
            stack(prompt_states, 0), stack(sample_states, 0),
            stack(prompt_states, 1), stack(sample_states, 1),
            stack(prompt_states, 2), stack(sample_states, 2),
            stack(sample_states, 3))
```

```python
import functools

import jax
import jax.numpy as jnp
from jax import lax
from jax.experimental import pallas as pl
from jax.experimental.pallas import tpu as pltpu

D_MODEL = 1024
D_PLE = 256
D_FF = 2816
GROUP_W = 256
N_HEADS = 4
HEAD_DIM = 64
MIX_IN_W = 8 * GROUP_W
CONV_A_WIDTH = 31
POOL_WINDOWS = (2, 4, 8, 16)
POOL_STATE = 15
CHUNK = 128
SHORT_CONV_WIDTH = 3
EPS = 1e-6
PAST_LEN = 16384

V7X_SUBLANES = 8
V7X_VMEM_BYTES = 64 * 1024 * 1024
V7X_VMEM_LIMIT_BYTES = V7X_VMEM_BYTES - 4 * 1024 * 1024

FFN_TILE = 1024
FFN_SUB_TILE = 512
FFN_COL_SPLITS = (0, 768, 1536, 2304, D_FF)
MIX_TILE = 256
MIX_SEQS = 2
MIX_HALO = 32
MIX_ROWS = 64
SAMPLE_SEQS = 32

COL_A_VAL, COL_A_GATE, COL_ZB, COL_U, COL_V, COL_B_GATE, COL_C_GATE, COL_D_IN = (
    i * GROUP_W for i in range(8))

bf16 = jnp.bfloat16
f32 = jnp.float32


def _dot(a, b):
    return jnp.dot(a, b, preferred_element_type=f32)


def _rms(x, g):
    ms = jnp.mean(x * x, axis=-1, keepdims=True)
    return x * lax.rsqrt(ms + EPS) * g


def _group_mean(x, avg):
    hi = x.astype(bf16)
    lo = (x - hi.astype(f32)).astype(bf16)
    return _dot(hi, avg) + _dot(lo, avg)


def _head_layer_norm(x, g, b, avg):
    mu = _group_mean(x, avg)
    xc = x - mu
    var = _group_mean(xc * xc, avg)
    return xc * lax.rsqrt(var + EPS) * g + b


def _silu(x):
    return x * jax.nn.sigmoid(x)


def _pool_select(sums, lane_group):
    s2, s4, s8, s16 = sums
    return jnp.where(lane_group == 0, s2,
                     jnp.where(lane_group == 1, s4, jnp.where(lane_group == 2, s8, s16)))


def _ffn_rows(x_ref, rows, g_ref, wg_ref, wu_ref, wd_ref, act_ref):
    h = _rms(x_ref[rows, :], g_ref[...]).astype(bf16)
    for lo, hi in zip(FFN_COL_SPLITS[:-1], FFN_COL_SPLITS[1:]):
        cs = slice(lo, hi)
        gate = _dot(h, wg_ref[:, cs])
        up = _dot(h, wu_ref[:, cs])
        act_ref[rows, cs] = (_silu(gate) * up).astype(bf16)
    return x_ref[rows, :] + 0.5 * _dot(act_ref[rows, :], wd_ref[...])


def _sub_tiles():
    return [pl.ds(r, FFN_SUB_TILE) for r in range(0, FFN_TILE, FFN_SUB_TILE)]


def _ffn_kernel(xs_ref, xp_ref, g_ref, wg_ref, wu_ref, wd_ref, os_ref, op_ref, act_ref, *,
                sample_steps):
    is_sample = pl.program_id(0) < sample_steps
    for rows in _sub_tiles():
        op_ref[rows, :] = jnp.where(is_sample, xs_ref[rows, :], xp_ref[rows, :])
        op_ref[rows, :] = _ffn_rows(op_ref, rows, g_ref, wg_ref, wu_ref, wd_ref, act_ref)

    @pl.when(is_sample)
    def _():
        os_ref[...] = op_ref[...]


def _ffn_ple_kernel(xs_ref, xp_ref, g_ref, wg_ref, wu_ref, wd_ref, ps_ref, pp_ref, gp_ref,
                    wpg_ref, wpp_ref, gf_ref, os_ref, op_ref, act_ref, *, sample_steps,
                    final_norm):
    is_sample = pl.program_id(0) < sample_steps
    for rows in _sub_tiles():
        op_ref[rows, :] = jnp.where(is_sample, xs_ref[rows, :], xp_ref[rows, :])
        op_ref[rows, :] = _ffn_rows(op_ref, rows, g_ref, wg_ref, wu_ref, wd_ref, act_ref)
        p = jnp.where(is_sample, ps_ref[rows, :], pp_ref[rows, :]).astype(bf16)
        gate = jax.nn.sigmoid(
            _dot(_rms(op_ref[rows, :], gp_ref[...]).astype(bf16), wpg_ref[...]))
        x = op_ref[rows, :] + gate * _dot(p, wpp_ref[...])
        if final_norm:
            x = _rms(x, gf_ref[...])
        op_ref[rows, :] = x

    @pl.when(is_sample)
    def _():
        os_ref[...] = op_ref[...]


def _resident(w, layer=None):
    if layer is None:
        return pl.BlockSpec(w.shape, lambda *_: (0,) * w.ndim, pipeline_mode=pl.Buffered(1))
    return pl.BlockSpec((None,) + w.shape[1:], lambda *_: (layer,) + (0,) * (w.ndim - 1),
                        pipeline_mode=pl.Buffered(1))


def _ffn_call(xs, xp, layer, ffn_w, ple=None, final_norm=False):
    tile = FFN_TILE
    ns, np_ = xs.shape[0] // tile, xp.shape[0] // tile

    def sample_tile(i):
        return jnp.minimum(i, ns - 1)

    def prompt_tile(i):
        return jnp.maximum(i - ns, 0)

    xs_spec = pl.BlockSpec((tile, D_MODEL), lambda i: (sample_tile(i), 0),
                           pipeline_mode=pl.Buffered(1))
    xp_spec = pl.BlockSpec((tile, D_MODEL), lambda i: (prompt_tile(i), 0))
    in_specs = [xs_spec, xp_spec] + [_resident(w, layer) for w in ffn_w]
    args = [xs, xp, *ffn_w]
    if ple is None:
        body = functools.partial(_ffn_kernel, sample_steps=ns)
        name = "ffn"
    else:
        ps, pp, ple_w, gf = ple
        body = functools.partial(_ffn_ple_kernel, sample_steps=ns, final_norm=final_norm)
        name = "ffn_ple"
        in_specs += ([pl.BlockSpec((None, tile, D_PLE), lambda i: (layer, sample_tile(i), 0),
                                   pipeline_mode=pl.Buffered(1)),
                      pl.BlockSpec((None, tile, D_PLE), lambda i: (layer, prompt_tile(i), 0))]
                     + [_resident(w, layer) for w in ple_w] + [_resident(gf)])
        args += [ps, pp, *ple_w, gf]
    return pl.pallas_call(
        body,
        grid=(ns + np_,),
        in_specs=in_specs,
        out_specs=[xs_spec, xp_spec],
        out_shape=[jax.ShapeDtypeStruct(xs.shape, f32), jax.ShapeDtypeStruct(xp.shape, f32)],
        scratch_shapes=[pltpu.VMEM((tile, D_FF), bf16)],
        compiler_params=pltpu.CompilerParams(
            dimension_semantics=("arbitrary",), vmem_limit_bytes=V7X_VMEM_LIMIT_BYTES),
        name=name,
    )(*args)


def _mixer_prompt_kernel(x_ref, gm_ref, wmi_ref, wmo_ref, caw_ref, cab_ref, nag_ref, nab_ref,
                         pw_ref, ps_ref, sng_ref, snb_ref, sw_ref, sb_ref, scw_ref, avg_ref,
                         o_ref, sa_ref, sp_ref, ss_ref,
                         z_all, sh_all, c_all, d_all, y_all):
    tm, halo = MIX_TILE, MIX_HALO
    sub, half = V7X_SUBLANES, GROUP_W // 2
    step = pl.program_id(1)
    weights = (gm_ref, wmi_ref, wmo_ref, caw_ref, cab_ref, nag_ref, nab_ref, pw_ref, ps_ref,
               sng_ref, snb_ref, sw_ref, sb_ref, scw_ref, avg_ref)

    @pl.when(step == 0)
    def _():
        z_all[:, 0:halo, :] = jnp.zeros((MIX_SEQS, halo, MIX_IN_W), f32)
        sh_all[:, :, 0:halo, :] = jnp.zeros((MIX_SEQS, sub - 1, halo, 2 * GROUP_W), f32)

    for s in range(MIX_SEQS):
        h = _rms(x_ref[s], gm_ref[...]).astype(bf16)
        z_all[s, halo:halo + tm, :] = _dot(h, wmi_ref[...])
    for s in range(MIX_SEQS):
        _mixer_prompt_tile(step, x_ref.at[s], o_ref.at[s], weights, z_all.at[s], sh_all.at[s],
                           c_all.at[s], d_all.at[s], y_all.at[s])

    @pl.when(step == pl.num_programs(1) - 1)
    def _():
        for s in range(MIX_SEQS):
            z_scr = z_all.at[s]
            sa_ref[s] = z_scr[pl.ds(halo + tm - (CONV_A_WIDTH - 1), CONV_A_WIDTH - 1),
                              COL_A_VAL:COL_A_VAL + GROUP_W]
            sp_ref[s] = z_scr[pl.ds(halo + tm - POOL_STATE, POOL_STATE),
                              COL_ZB:COL_ZB + GROUP_W]
            ss_ref[s] = z_scr[pl.ds(halo + tm - (SHORT_CONV_WIDTH - 1), SHORT_CONV_WIDTH - 1),
                              COL_D_IN:COL_D_IN + GROUP_W]

    for s in range(MIX_SEQS):
        z_scr, sh_scr = z_all.at[s], sh_all.at[s]
        for col in (COL_A_VAL, COL_ZB, COL_D_IN):
            z_scr[0:halo, col:col + GROUP_W] = z_scr[tm:tm + halo, col:col + GROUP_W]
        sh_scr[0:3, 0:halo, :] = sh_scr[0:3, tm:tm + halo, :]
        for lanes in (slice(0, GROUP_W), slice(GROUP_W + half, 2 * GROUP_W)):
            sh_scr[3:sub - 1, 0:halo, lanes] = sh_scr[3:sub - 1, tm:tm + halo, lanes]


def _mixer_prompt_tile(step, x_ref, o_ref, weights, z_scr, sh_scr, c_scr, d_scr, y_scr):
    (gm_ref, wmi_ref, wmo_ref, caw_ref, cab_ref, nag_ref, nab_ref, pw_ref, ps_ref,
     sng_ref, snb_ref, sw_ref, sb_ref, scw_ref, avg_ref) = weights
    tm, halo, rb = MIX_TILE, MIX_HALO, MIX_ROWS
    sub, half = V7X_SUBLANES, GROUP_W // 2

    for r in range(0, tm, rb):
        rows = pl.ds(halo + r, rb)
        a_val = z_scr[rows, COL_A_VAL:COL_A_VAL + GROUP_W]
        a_gate = z_scr[rows, COL_A_GATE:COL_A_GATE + GROUP_W]
        z_scr[rows, COL_A_VAL:COL_A_VAL + GROUP_W] = a_val * jax.nn.sigmoid(a_gate)
        c_gate = z_scr[rows, COL_C_GATE:COL_C_GATE + GROUP_W]
        d_in = z_scr[rows, COL_D_IN:COL_D_IN + GROUP_W]
        z_scr[rows, COL_D_IN:COL_D_IN + GROUP_W] = c_gate * d_in
        for b in range(1, sub):
            late = pl.ds(halo + r - b, rb)
            sh_scr[b - 1, rows, 0:GROUP_W] = z_scr[late, COL_A_VAL:COL_A_VAL + GROUP_W]
            if b < 4:
                sh_scr[b - 1, rows, GROUP_W:2 * GROUP_W] = z_scr[late, COL_ZB:COL_ZB + GROUP_W]
            else:
                sh_scr[b - 1, rows, GROUP_W + half:2 * GROUP_W] = (
                    z_scr[late, COL_ZB + half:COL_ZB + GROUP_W])

    lane_group = lax.broadcasted_iota(jnp.int32, (rb, GROUP_W), 1) // HEAD_DIM
    win = jnp.left_shift(2, lane_group)
    row_iota = lax.broadcasted_iota(jnp.int32, (rb, GROUP_W), 0)
    first_head = lax.broadcasted_iota(jnp.int32, (rb, half), 1) < HEAD_DIM

    for r in range(0, tm, rb):
        def delayed(col, lanes, b, back):
            at = pl.ds(halo + r - back, rb)
            if b == 0:
                return z_scr[at, col + lanes.start:col + lanes.stop]
            base = 0 if col == COL_A_VAL else GROUP_W
            return sh_scr[b - 1, at, base + lanes.start:base + lanes.stop]

        acc = jnp.broadcast_to(cab_ref[...], (rb, GROUP_W))
        for j in range(CONV_A_WIDTH):
            k = CONV_A_WIDTH - 1 - j
            tap = delayed(COL_A_VAL, slice(0, GROUP_W), j % sub, j - j % sub)
            acc = acc + caw_ref[k:k + 1, :] * tap
        c_scr[r:r + rb, :] = acc

        lo, hi = slice(0, half), slice(half, GROUP_W)
        s2 = delayed(COL_ZB, lo, 0, 0) + delayed(COL_ZB, lo, 1, 0)
        s4 = s2 + delayed(COL_ZB, lo, 2, 0) + delayed(COL_ZB, lo, 3, 0)
        s8 = delayed(COL_ZB, hi, 0, 0)
        for b in range(1, sub):
            s8 = s8 + delayed(COL_ZB, hi, b, 0)
        s16 = s8
        for b in range(sub):
            s16 = s16 + delayed(COL_ZB, hi, b, sub)
        sums = jnp.concatenate(
            [jnp.where(first_head, s2, s4), jnp.where(first_head, s8, s16)], axis=1)
        zb = z_scr[pl.ds(halo + r, rb), COL_ZB:COL_ZB + GROUP_W]
        pos = step * tm + r + row_iota
        cnt = jnp.minimum(pos + 1, win).astype(f32)
        d_scr[r:r + rb, :] = (sums / cnt - zb).astype(bf16)

        rows = pl.ds(halo + r, rb)
        q0 = z_scr[pl.ds(halo + r - 2, rb), COL_D_IN:COL_D_IN + GROUP_W]
        q1 = z_scr[pl.ds(halo + r - 1, rb), COL_D_IN:COL_D_IN + GROUP_W]
        q2 = z_scr[rows, COL_D_IN:COL_D_IN + GROUP_W]
        conv = scw_ref[0:1, :] * q0 + scw_ref[1:2, :] * q1 + scw_ref[2:3, :] * q2
        b_gate = z_scr[rows, COL_B_GATE:COL_B_GATE + GROUP_W]
        y_scr[r:r + rb, 3 * GROUP_W:4 * GROUP_W] = (b_gate * conv).astype(bf16)

    avg = avg_ref[...]
    ya = _head_layer_norm(c_scr[...], nag_ref[...], nab_ref[...], avg)
    y_scr[:, 0:GROUP_W] = _silu(ya).astype(bf16)
    y_scr[:, GROUP_W:2 * GROUP_W] = (_dot(d_scr[...], pw_ref[...]) * ps_ref[...]).astype(bf16)

    t_idx = lax.broadcasted_iota(jnp.int32, (CHUNK, N_HEADS * CHUNK), 0)
    s_idx = lax.broadcasted_iota(jnp.int32, (CHUNK, N_HEADS * CHUNK), 1) % CHUNK
    w_cat = jnp.where(s_idx <= t_idx, sw_ref[...], 0.0).astype(bf16)
    head_of_lane = lax.broadcasted_iota(jnp.int32, (CHUNK, GROUP_W), 1) // HEAD_DIM
    for c in range(tm // CHUNK):
        rows = pl.ds(halo + c * CHUNK, CHUNK)
        u = jax.nn.gelu(z_scr[rows, COL_U:COL_U + GROUP_W])
        v = jax.nn.gelu(z_scr[rows, COL_V:COL_V + GROUP_W])
        v = _head_layer_norm(v, sng_ref[...], snb_ref[...], avg)
        v_stack = jnp.concatenate(
            [jnp.where(head_of_lane == hd, v, 0.0).astype(bf16) for hd in range(N_HEADS)], axis=0)
        s = _dot(w_cat, v_stack) + sb_ref[...]
        y_scr[c * CHUNK:(c + 1) * CHUNK, 2 * GROUP_W:3 * GROUP_W] = (u * s).astype(bf16)

    o_ref[...] = x_ref[...] + _dot(y_scr[...], wmo_ref[...])


def _mixer_prompt_call(x, batch, seq, layer, wts, avg):
    tm, ns = MIX_TILE, MIX_SEQS
    x = x.reshape(batch, seq, D_MODEL)
    row_spec = pl.BlockSpec((ns, tm, D_MODEL), lambda b, l: (b, l, 0))

    def state_spec(rows):
        return pl.BlockSpec((ns, rows, GROUP_W), lambda b, l: (b, 0, 0))

    in_specs = [row_spec] + [_resident(w, layer) for w in wts] + [_resident(avg)]
    out = pl.pallas_call(
        _mixer_prompt_kernel,
        grid=(batch // ns, seq // tm),
        in_specs=in_specs,
        out_specs=[row_spec, state_spec(CONV_A_WIDTH - 1), state_spec(POOL_STATE),
                   state_spec(SHORT_CONV_WIDTH - 1)],
        out_shape=[jax.ShapeDtypeStruct((batch, seq, D_MODEL), f32),
                   jax.ShapeDtypeStruct((batch, CONV_A_WIDTH - 1, GROUP_W), f32),
                   jax.ShapeDtypeStruct((batch, POOL_STATE, GROUP_W), f32),
                   jax.ShapeDtypeStruct((batch, SHORT_CONV_WIDTH - 1, GROUP_W), f32)],
        scratch_shapes=[pltpu.VMEM((ns, MIX_HALO + tm, MIX_IN_W), f32),
                        pltpu.VMEM((ns, V7X_SUBLANES - 1, MIX_HALO + tm, 2 * GROUP_W), f32),
                        pltpu.VMEM((ns, tm, GROUP_W), f32),
                        pltpu.VMEM((ns, tm, GROUP_W), bf16),
                        pltpu.VMEM((ns, tm, D_MODEL), bf16)],
        compiler_params=pltpu.CompilerParams(
            dimension_semantics=("arbitrary", "arbitrary"),
            vmem_limit_bytes=V7X_VMEM_LIMIT_BYTES),
        name="mixer_prompt",
    )(x, *wts, avg)
    return (out[0].reshape(batch * seq, D_MODEL),) + tuple(out[1:])


def _mixer_sample_kernel(x_ref, gm_ref, wmi_ref, wmo_ref, caw_ref, cab_ref, nag_ref, nab_ref,
                         pw_ref, ps_ref, sng_ref, snb_ref, coef_ref, sb_ref, scw_ref, avg_ref,
                         sta_ref, stp_ref, sts_ref,
                         o_ref, sa_ref, sp_ref, ss_ref, v_ref,
                         xa, xp, xs, vs, *, dec_seq):
    nb = SAMPLE_SEQS
    rows = nb * dec_seq
    hist_a, hist_p, hist_s = CONV_A_WIDTH - 1, POOL_STATE, SHORT_CONV_WIDTH - 1

    x = x_ref[...]
    z = _dot(_rms(x, gm_ref[...]).astype(bf16), wmi_ref[...])

    def col(c):
        return z[:, c:c + GROUP_W]

    def seqs(a):
        return a.reshape(nb, dec_seq, GROUP_W)

    avg = avg_ref[...]

    xa[:, 0:hist_a, :] = sta_ref[...]
    xa[:, hist_a:hist_a + dec_seq, :] = seqs(col(COL_A_VAL) * jax.nn.sigmoid(col(COL_A_GATE)))
    acc = jnp.broadcast_to(cab_ref[...].reshape(1, 1, GROUP_W), (nb, dec_seq, GROUP_W))
    for k in range(CONV_A_WIDTH):
        acc = acc + caw_ref[k:k + 1, :].reshape(1, 1, GROUP_W) * xa[:, k:k + dec_seq, :]
    sa_ref[...] = xa[:, dec_seq:dec_seq + hist_a, :]
    ya = _head_layer_norm(acc.reshape(rows, GROUP_W), nag_ref[...], nab_ref[...], avg)
    ya = _silu(ya).astype(bf16)

    zb = col(COL_ZB)
    xp[:, 0:hist_p, :] = stp_ref[...]
    xp[:, hist_p:hist_p + dec_seq, :] = seqs(zb)
    run = xp[:, hist_p:hist_p + dec_seq, :]
    sums = []
    for j in range(1, max(POOL_WINDOWS)):
        run = run + xp[:, hist_p - j:hist_p - j + dec_seq, :]
        if j + 1 in POOL_WINDOWS:
            sums.append(run)
    sp_ref[...] = xp[:, dec_seq:dec_seq + hist_p, :]
    lane_group = lax.broadcasted_iota(jnp.int32, (nb, dec_seq, GROUP_W), 2) // HEAD_DIM
    pos = PAST_LEN + lax.broadcasted_iota(jnp.int32, (nb, dec_seq, GROUP_W), 1)
    cnt = jnp.minimum(pos + 1, jnp.left_shift(2, lane_group)).astype(f32)
    mean = (_pool_select(sums, lane_group) / cnt).reshape(rows, GROUP_W)
    yb = (_dot((mean - zb).astype(bf16), pw_ref[...]) * ps_ref[...]).astype(bf16)

    u = jax.nn.gelu(col(COL_U))
    v = _head_layer_norm(jax.nn.gelu(col(COL_V)), sng_ref[...], snb_ref[...], avg)
    v_ref[...] = seqs(v)
    vs[...] = seqs(v)
    t_idx = lax.broadcasted_iota(jnp.int32, (dec_seq, GROUP_W), 0)
    s = jnp.broadcast_to(sb_ref[...].reshape(1, dec_seq, GROUP_W), (nb, dec_seq, GROUP_W))
    for j in range(dec_seq):
        w_j = jnp.where(t_idx >= j, coef_ref[j], 0.0)
        s = s + w_j.reshape(1, dec_seq, GROUP_W) * vs[:, j:j + 1, :]
    yc = (u * s.reshape(rows, GROUP_W)).astype(bf16)

    xs[:, 0:hist_s, :] = sts_ref[...]
    xs[:, hist_s:hist_s + dec_seq, :] = seqs(col(COL_C_GATE) * col(COL_D_IN))
    conv = jnp.zeros((nb, dec_seq, GROUP_W), f32)
    for k in range(SHORT_CONV_WIDTH):
        conv = conv + scw_ref[k:k + 1, :].reshape(1, 1, GROUP_W) * xs[:, k:k + dec_seq, :]
    ss_ref[...] = xs[:, dec_seq:dec_seq + hist_s, :]
    yd = (col(COL_B_GATE) * conv.reshape(rows, GROUP_W)).astype(bf16)

    y = jnp.concatenate([ya, yb, yc, yd], axis=1)
    o_ref[...] = x + _dot(y, wmo_ref[...])


def _mixer_sample_call(x, st_a, st_p, st_s, dec_batch, dec_seq, layer, wts, avg):
    nb = SAMPLE_SEQS
    rows = nb * dec_seq
    row_spec = pl.BlockSpec((rows, D_MODEL), lambda i: (i, 0))

    def state_spec(r):
        return pl.BlockSpec((nb, r, GROUP_W), lambda i: (i, 0, 0))

    def state_in_spec(r):
        return pl.BlockSpec((None, nb, r, GROUP_W), lambda i: (layer, i, 0, 0))

    hist_a, hist_p, hist_s = CONV_A_WIDTH - 1, POOL_STATE, SHORT_CONV_WIDTH - 1
    in_specs = ([row_spec] + [_resident(w, layer) for w in wts] + [_resident(avg)]
                + [state_in_spec(hist_a), state_in_spec(hist_p), state_in_spec(hist_s)])
    return pl.pallas_call(
        functools.partial(_mixer_sample_kernel, dec_seq=dec_seq),
        grid=(dec_batch // nb,),
        in_specs=in_specs,
        out_specs=[row_spec, state_spec(hist_a), state_spec(hist_p), state_spec(hist_s),
                   state_spec(dec_seq)],
        out_shape=[jax.ShapeDtypeStruct((dec_batch * dec_seq, D_MODEL), f32),
                   jax.ShapeDtypeStruct((dec_batch, hist_a, GROUP_W), f32),
                   jax.ShapeDtypeStruct((dec_batch, hist_p, GROUP_W), f32),
                   jax.ShapeDtypeStruct((dec_batch, hist_s, GROUP_W), f32),
                   jax.ShapeDtypeStruct((dec_batch, dec_seq, GROUP_W), f32)],
        scratch_shapes=[pltpu.VMEM((nb, hist_a + dec_seq, GROUP_W), f32),
                        pltpu.VMEM((nb, hist_p + dec_seq, GROUP_W), f32),
                        pltpu.VMEM((nb, hist_s + dec_seq, GROUP_W), f32),
                        pltpu.VMEM((nb, dec_seq, GROUP_W), f32)],
        compiler_params=pltpu.CompilerParams(
            dimension_semantics=("arbitrary",), vmem_limit_bytes=V7X_VMEM_LIMIT_BYTES),
        name="mixer_sample",
    )(x, *wts, avg, st_a, st_p, st_s)


def _rows(v):
    return v.reshape(v.shape[0], 1, v.shape[-1])


def _block_diag(w):
    n, g, c, d = w.shape
    eye = jnp.eye(g, dtype=w.dtype)
    return (eye[None, :, None, :, None] * w[:, :, :, None, :]).reshape(n, g * c, g * d)


def kernel(x_prompt, x_sample, p_prompt, p_sample, state_conv_a, state_pool, state_short_conv,
           norm_ffn1, w_ffn1_gate, w_ffn1_up, w_ffn1_down, norm_mix, w_mix_in,
           conv_a_w, conv_a_b, norm_a_g, norm_a_b, pool_w, pool_scale,
           sgu_norm_g, sgu_norm_b, sgu_w, sgu_b, short_conv_w, w_mix_out,
           norm_ffn2, w_ffn2_gate, w_ffn2_up, w_ffn2_down,
           norm_ple, w_ple_gate, w_ple_proj, norm_final):
    depth = w_mix_in.shape[0]
    batch, seq, _ = x_prompt.shape
    dec_batch, dec_seq, _ = x_sample.shape
    assert seq % MIX_TILE == 0 and MIX_TILE % CHUNK == 0 and dec_seq <= CHUNK
    assert batch % MIX_SEQS == 0
    assert dec_batch % SAMPLE_SEQS == 0 and dec_seq == V7X_SUBLANES
    assert (batch * seq) % FFN_TILE == 0 and (dec_batch * dec_seq) % FFN_TILE == 0
    assert all(c % 256 == 0 for c in FFN_COL_SPLITS)

    avg = _block_diag(jnp.full((1, N_HEADS, HEAD_DIM, HEAD_DIM), 1.0 / HEAD_DIM, f32))[0]
    avg = avg.astype(bf16)

    ffn1 = (_rows(norm_ffn1), w_ffn1_gate.astype(bf16), w_ffn1_up.astype(bf16),
            w_ffn1_down.astype(bf16))
    ffn2 = (_rows(norm_ffn2), w_ffn2_gate.astype(bf16), w_ffn2_up.astype(bf16),
            w_ffn2_down.astype(bf16))
    ple_w = (_rows(norm_ple), w_ple_gate.astype(bf16), w_ple_proj.astype(bf16))
    gf = norm_final.reshape(1, D_MODEL)
    mix_head = (_rows(norm_mix), w_mix_in.astype(bf16), w_mix_out.astype(bf16),
                conv_a_w, _rows(conv_a_b), _rows(norm_a_g), _rows(norm_a_b),
                _block_diag(pool_w).astype(bf16), _rows(pool_scale),
                _rows(sgu_norm_g), _rows(sgu_norm_b))
    sw_prompt = jnp.transpose(sgu_w, (0, 2, 1, 3)).reshape(depth, CHUNK, N_HEADS * CHUNK)
    sb_prompt = jnp.repeat(jnp.swapaxes(sgu_b, 1, 2), HEAD_DIM, axis=2)
    sw_sample = jnp.repeat(jnp.transpose(sgu_w[:, :, :dec_seq, :dec_seq], (0, 3, 2, 1)),
                           HEAD_DIM, axis=3)
    sb_sample = sb_prompt[:, :dec_seq]
    mix_prompt_w = mix_head + (sw_prompt, sb_prompt, short_conv_w)
    mix_sample_w = mix_head + (sw_sample, sb_sample, short_conv_w)

    pp = p_prompt.reshape(depth, batch * seq, D_PLE)
    ps = p_sample.reshape(depth, dec_batch * dec_seq, D_PLE)
    xp = x_prompt.reshape(batch * seq, D_MODEL)
    xs = x_sample.reshape(dec_batch * dec_seq, D_MODEL)
    prompt_states, sample_states = [], []
    for i in range(depth):
        last = i == depth - 1

        xs, xp = _ffn_call(xs, xp, i, ffn1)
        xp, sa, sp, ss = _mixer_prompt_call(xp, batch, seq, i, mix_prompt_w, avg)
        prompt_states.append((sa, sp, ss))
        xs, sa, sp, ss, sv = _mixer_sample_call(
            xs, state_conv_a, state_pool, state_short_conv, dec_batch, dec_seq, i,
            mix_sample_w, avg)
        sample_states.append((sa, sp, ss, sv))
        xs, xp = _ffn_call(xs, xp, i, ffn2, ple=(ps, pp, ple_w, gf), final_norm=last)

    def stack(states, j):
        return jnp.stack([s[j] for s in states])

    return (xp.reshape(batch, seq, D_MODEL), xs.reshape(dec_batch, dec_seq, D_MODEL),
            stack(prompt_states, 0), stack(sample_states, 0),
            stack(prompt_states, 1), stack(sample_states, 1),
            stack(prompt_states, 2), stack(sample_states, 2),
            stack(sample_states, 3))
```

```python
import functools

import jax
import jax.numpy as jnp
from jax import lax
from jax.experimental import pallas as pl
from jax.experimental.pallas import tpu as pltpu

D_MODEL = 1024
D_PLE = 256
D_FF = 2816
GROUP_W = 256
N_HEADS = 4
HEAD_DIM = 64
MIX_IN_W = 8 * GROUP_W
CONV_A_WIDTH = 31
POOL_WINDOWS = (2, 4, 8, 16)
POOL_STATE = 15
CHUNK = 128
SHORT_CONV_WIDTH = 3
EPS = 1e-6
PAST_LEN = 16384

V7X_SUBLANES = 8
V7X_VMEM_BYTES = 64 * 1024 * 1024
V7X_VMEM_LIMIT_BYTES = V7X_VMEM_BYTES - 4 * 1024 * 1024

FFN_TILE = 1024
FFN_SUB_TILE = 512
FFN_COL_SPLITS = (0, 768, 1536, 2304, D_FF)
MIX_TILE = 256
MIX_SEQS = 2
MIX_HALO = 32
MIX_ROWS = 64
SAMPLE_SEQS = 64

COL_A_VAL, COL_A_GATE, COL_ZB, COL_U, COL_V, COL_B_GATE, COL_C_GATE, COL_D_IN = (
    i * GROUP_W for i in range(8))

bf16 = jnp.bfloat16
f32 = jnp.float32


def _dot(a, b):
    return jnp.dot(a, b, preferred_element_type=f32)


def _rms(x, g):
    ms = jnp.mean(x * x, axis=-1, keepdims=True)
    return x * lax.rsqrt(ms + EPS) * g


def _group_mean(x, avg):
    hi = x.astype(bf16)
    lo = (x - hi.astype(f32)).astype(bf16)
    return _dot(hi, avg) + _dot(lo, avg)


def _head_layer_norm(x, g, b, avg):
    mu = _group_mean(x, avg)
    xc = x - mu
    var = _group_mean(xc * xc, avg)
    return xc * lax.rsqrt(var + EPS) * g + b


def _silu(x):
    return x * jax.nn.sigmoid(x)


def _pool_select(sums, lane_group):
    s2, s4, s8, s16 = sums
    return jnp.where(lane_group == 0, s2,
                     jnp.where(lane_group == 1, s4, jnp.where(lane_group == 2, s8, s16)))


def _ffn_rows(x_ref, rows, g_ref, wg_ref, wu_ref, wd_ref, act_ref):
    h = _rms(x_ref[rows, :], g_ref[...]).astype(bf16)
    for lo, hi in zip(FFN_COL_SPLITS[:-1], FFN_COL_SPLITS[1:]):
        cs = slice(lo, hi)
        gate = _dot(h, wg_ref[:, cs])
        up = _dot(h, wu_ref[:, cs])
        act_ref[rows, cs] = (_silu(gate) * up).astype(bf16)
    return x_ref[rows, :] + 0.5 * _dot(act_ref[rows, :], wd_ref[...])


def _sub_tiles():
    return [pl.ds(r, FFN_SUB_TILE) for r in range(0, FFN_TILE, FFN_SUB_TILE)]


def _ffn_kernel(xs_ref, xp_ref, g_ref, wg_ref, wu_ref, wd_ref, os_ref, op_ref, act_ref, *,
                sample_steps, layer):
    g_ref = _LayerRow(g_ref, layer)
    is_sample = pl.program_id(0) < sample_steps
    for rows in _sub_tiles():
        op_ref[rows, :] = jnp.where(is_sample, xs_ref[rows, :], xp_ref[rows, :])
        op_ref[rows, :] = _ffn_rows(op_ref, rows, g_ref, wg_ref, wu_ref, wd_ref, act_ref)

    @pl.when(is_sample)
    def _():
        os_ref[...] = op_ref[...]


def _ffn_ple_kernel(xs_ref, xp_ref, g_ref, wg_ref, wu_ref, wd_ref, ps_ref, pp_ref, gp_ref,
                    wpg_ref, wpp_ref, gf_ref, os_ref, op_ref, act_ref, *, sample_steps,
                    layer, final_norm):
    g_ref, gp_ref = _LayerRow(g_ref, layer), _LayerRow(gp_ref, layer)
    is_sample = pl.program_id(0) < sample_steps
    for rows in _sub_tiles():
        op_ref[rows, :] = jnp.where(is_sample, xs_ref[rows, :], xp_ref[rows, :])
        op_ref[rows, :] = _ffn_rows(op_ref, rows, g_ref, wg_ref, wu_ref, wd_ref, act_ref)
        p = jnp.where(is_sample, ps_ref[rows, :], pp_ref[rows, :]).astype(bf16)
        gate = jax.nn.sigmoid(
            _dot(_rms(op_ref[rows, :], gp_ref[...]).astype(bf16), wpg_ref[...]))
        x = op_ref[rows, :] + gate * _dot(p, wpp_ref[...])
        if final_norm:
            x = _rms(x, gf_ref[...])
        op_ref[rows, :] = x

    @pl.when(is_sample)
    def _():
        os_ref[...] = op_ref[...]


class _LayerRow:
    def __init__(self, ref, layer):
        self.ref, self.layer = ref, layer

    def __getitem__(self, idx):
        assert idx is Ellipsis
        return self.ref[self.layer:self.layer + 1, :]


def _resident(w, layer=None):
    if layer is None or w.ndim == 2:
        return pl.BlockSpec(w.shape, lambda *_: (0,) * w.ndim, pipeline_mode=pl.Buffered(1))
    return pl.BlockSpec((None,) + w.shape[1:], lambda *_: (layer,) + (0,) * (w.ndim - 1),
                        pipeline_mode=pl.Buffered(1))


def _ffn_call(xs, xp, layer, ffn_w, ple=None, final_norm=False):
    tile = FFN_TILE
    ns, np_ = xs.shape[0] // tile, xp.shape[0] // tile

    def sample_tile(i):
        return jnp.minimum(i, ns - 1)

    def prompt_tile(i):
        return jnp.maximum(i - ns, 0)

    xs_spec = pl.BlockSpec((tile, D_MODEL), lambda i: (sample_tile(i), 0),
                           pipeline_mode=pl.Buffered(1))
    xp_spec = pl.BlockSpec((tile, D_MODEL), lambda i: (prompt_tile(i), 0))
    in_specs = [xs_spec, xp_spec] + [_resident(w, layer) for w in ffn_w]
    args = [xs, xp, *ffn_w]
    if ple is None:
        body = functools.partial(_ffn_kernel, sample_steps=ns, layer=layer)
        name = "ffn"
    else:
        ps, pp, ple_w, gf = ple
        body = functools.partial(_ffn_ple_kernel, sample_steps=ns, layer=layer,
                                 final_norm=final_norm)
        name = "ffn_ple"
        in_specs += ([pl.BlockSpec((None, tile, D_PLE), lambda i: (layer, sample_tile(i), 0),
                                   pipeline_mode=pl.Buffered(1)),
                      pl.BlockSpec((None, tile, D_PLE), lambda i: (layer, prompt_tile(i), 0))]
                     + [_resident(w, layer) for w in ple_w] + [_resident(gf)])
        args += [ps, pp, *ple_w, gf]
    return pl.pallas_call(
        body,
        grid=(ns + np_,),
        in_specs=in_specs,
        out_specs=[xs_spec, xp_spec],
        out_shape=[jax.ShapeDtypeStruct(xs.shape, f32), jax.ShapeDtypeStruct(xp.shape, f32)],
        scratch_shapes=[pltpu.VMEM((tile, D_FF), bf16)],
        compiler_params=pltpu.CompilerParams(
            dimension_semantics=("arbitrary",), vmem_limit_bytes=V7X_VMEM_LIMIT_BYTES),
        name=name,
    )(*args)


def _mixer_prompt_kernel(x_ref, gm_ref, wmi_ref, wmo_ref, caw_ref, cab_ref, nag_ref, nab_ref,
                         pw_ref, ps_ref, sng_ref, snb_ref, sw_ref, sb_ref, scw_ref, avg_ref,
                         o_ref, sa_ref, sp_ref, ss_ref,
                         z_all, sh_all, c_all, d_all, y_all, *, layer):
    gm_ref, cab_ref, nag_ref, nab_ref, ps_ref, sng_ref, snb_ref = (
        _LayerRow(r, layer) for r in (gm_ref, cab_ref, nag_ref, nab_ref, ps_ref, sng_ref, snb_ref))
    tm, halo = MIX_TILE, MIX_HALO
    sub, half = V7X_SUBLANES, GROUP_W // 2
    step = pl.program_id(1)
    weights = (gm_ref, wmi_ref, wmo_ref, caw_ref, cab_ref, nag_ref, nab_ref, pw_ref, ps_ref,
               sng_ref, snb_ref, sw_ref, sb_ref, scw_ref, avg_ref)

    @pl.when(step == 0)
    def _():
        z_all[:, 0:halo, :] = jnp.zeros((MIX_SEQS, halo, MIX_IN_W), f32)
        sh_all[:, :, 0:halo, :] = jnp.zeros((MIX_SEQS, sub - 1, halo, 2 * GROUP_W), f32)

    for s in range(MIX_SEQS):
        h = _rms(x_ref[s], gm_ref[...]).astype(bf16)
        z_all[s, halo:halo + tm, :] = _dot(h, wmi_ref[...])
    for s in range(MIX_SEQS):
        _mixer_prompt_tile(step, x_ref.at[s], o_ref.at[s], weights, z_all.at[s], sh_all.at[s],
                           c_all.at[s], d_all.at[s], y_all.at[s])

    @pl.when(step == pl.num_programs(1) - 1)
    def _():
        for s in range(MIX_SEQS):
            z_scr = z_all.at[s]
            sa_ref[s] = z_scr[pl.ds(halo + tm - (CONV_A_WIDTH - 1), CONV_A_WIDTH - 1),
                              COL_A_VAL:COL_A_VAL + GROUP_W]
            sp_ref[s] = z_scr[pl.ds(halo + tm - POOL_STATE, POOL_STATE),
                              COL_ZB:COL_ZB + GROUP_W]
            ss_ref[s] = z_scr[pl.ds(halo + tm - (SHORT_CONV_WIDTH - 1), SHORT_CONV_WIDTH - 1),
                              COL_D_IN:COL_D_IN + GROUP_W]

    for s in range(MIX_SEQS):
        z_scr, sh_scr = z_all.at[s], sh_all.at[s]
        for col in (COL_A_VAL, COL_ZB, COL_D_IN):
            z_scr[0:halo, col:col + GROUP_W] = z_scr[tm:tm + halo, col:col + GROUP_W]
        sh_scr[0:3, 0:halo, :] = sh_scr[0:3, tm:tm + halo, :]
        for lanes in (slice(0, GROUP_W), slice(GROUP_W + half, 2 * GROUP_W)):
            sh_scr[3:sub - 1, 0:halo, lanes] = sh_scr[3:sub - 1, tm:tm + halo, lanes]


def _mixer_prompt_tile(step, x_ref, o_ref, weights, z_scr, sh_scr, c_scr, d_scr, y_scr):
    (gm_ref, wmi_ref, wmo_ref, caw_ref, cab_ref, nag_ref, nab_ref, pw_ref, ps_ref,
     sng_ref, snb_ref, sw_ref, sb_ref, scw_ref, avg_ref) = weights
    tm, halo, rb = MIX_TILE, MIX_HALO, MIX_ROWS
    sub, half = V7X_SUBLANES, GROUP_W // 2

    for r in range(0, tm, rb):
        rows = pl.ds(halo + r, rb)
        a_val = z_scr[rows, COL_A_VAL:COL_A_VAL + GROUP_W]
        a_gate = z_scr[rows, COL_A_GATE:COL_A_GATE + GROUP_W]
        z_scr[rows, COL_A_VAL:COL_A_VAL + GROUP_W] = a_val * jax.nn.sigmoid(a_gate)
        c_gate = z_scr[rows, COL_C_GATE:COL_C_GATE + GROUP_W]
        d_in = z_scr[rows, COL_D_IN:COL_D_IN + GROUP_W]
        z_scr[rows, COL_D_IN:COL_D_IN + GROUP_W] = c_gate * d_in
        for b in range(1, sub):
            late = pl.ds(halo + r - b, rb)
            sh_scr[b - 1, rows, 0:GROUP_W] = z_scr[late, COL_A_VAL:COL_A_VAL + GROUP_W]
            if b < 4:
                sh_scr[b - 1, rows, GROUP_W:2 * GROUP_W] = z_scr[late, COL_ZB:COL_ZB + GROUP_W]
            else:
                sh_scr[b - 1, rows, GROUP_W + half:2 * GROUP_W] = (
                    z_scr[late, COL_ZB + half:COL_ZB + GROUP_W])

    lane_group = lax.broadcasted_iota(jnp.int32, (rb, GROUP_W), 1) // HEAD_DIM
    win = jnp.left_shift(2, lane_group)
    row_iota = lax.broadcasted_iota(jnp.int32, (rb, GROUP_W), 0)
    first_head = lax.broadcasted_iota(jnp.int32, (rb, half), 1) < HEAD_DIM

    for r in range(0, tm, rb):
        def delayed(col, lanes, b, back):
            at = pl.ds(halo + r - back, rb)
            if b == 0:
                return z_scr[at, col + lanes.start:col + lanes.stop]
            base = 0 if col == COL_A_VAL else GROUP_W
            return sh_scr[b - 1, at, base + lanes.start:base + lanes.stop]

        acc = jnp.broadcast_to(cab_ref[...], (rb, GROUP_W))
        for j in range(CONV_A_WIDTH):
            k = CONV_A_WIDTH - 1 - j
            tap = delayed(COL_A_VAL, slice(0, GROUP_W), j % sub, j - j % sub)
            acc = acc + caw_ref[k:k + 1, :] * tap
        c_scr[r:r + rb, :] = acc

        lo, hi = slice(0, half), slice(half, GROUP_W)
        s2 = delayed(COL_ZB, lo, 0, 0) + delayed(COL_ZB, lo, 1, 0)
        s4 = s2 + delayed(COL_ZB, lo, 2, 0) + delayed(COL_ZB, lo, 3, 0)
        s8 = delayed(COL_ZB, hi, 0, 0)
        for b in range(1, sub):
            s8 = s8 + delayed(COL_ZB, hi, b, 0)
        s16 = s8
        for b in range(sub):
            s16 = s16 + delayed(COL_ZB, hi, b, sub)
        sums = jnp.concatenate(
            [jnp.where(first_head, s2, s4), jnp.where(first_head, s8, s16)], axis=1)
        zb = z_scr[pl.ds(halo + r, rb), COL_ZB:COL_ZB + GROUP_W]
        pos = step * tm + r + row_iota
        cnt = jnp.minimum(pos + 1, win).astype(f32)
        d_scr[r:r + rb, :] = (sums / cnt - zb).astype(bf16)

        rows = pl.ds(halo + r, rb)
        q0 = z_scr[pl.ds(halo + r - 2, rb), COL_D_IN:COL_D_IN + GROUP_W]
        q1 = z_scr[pl.ds(halo + r - 1, rb), COL_D_IN:COL_D_IN + GROUP_W]
        q2 = z_scr[rows, COL_D_IN:COL_D_IN + GROUP_W]
        conv = scw_ref[0:1, :] * q0 + scw_ref[1:2, :] * q1 + scw_ref[2:3, :] * q2
        b_gate = z_scr[rows, COL_B_GATE:COL_B_GATE + GROUP_W]
        y_scr[r:r + rb, 3 * GROUP_W:4 * GROUP_W] = (b_gate * conv).astype(bf16)

    avg = avg_ref[...]
    ya = _head_layer_norm(c_scr[...], nag_ref[...], nab_ref[...], avg)
    y_scr[:, 0:GROUP_W] = _silu(ya).astype(bf16)
    y_scr[:, GROUP_W:2 * GROUP_W] = (_dot(d_scr[...], pw_ref[...]) * ps_ref[...]).astype(bf16)

    t_idx = lax.broadcasted_iota(jnp.int32, (CHUNK, N_HEADS * CHUNK), 0)
    s_idx = lax.broadcasted_iota(jnp.int32, (CHUNK, N_HEADS * CHUNK), 1) % CHUNK
    w_cat = jnp.where(s_idx <= t_idx, sw_ref[...], 0.0).astype(bf16)
    head_of_lane = lax.broadcasted_iota(jnp.int32, (CHUNK, GROUP_W), 1) // HEAD_DIM
    for c in range(tm // CHUNK):
        rows = pl.ds(halo + c * CHUNK, CHUNK)
        u = jax.nn.gelu(z_scr[rows, COL_U:COL_U + GROUP_W])
        v = jax.nn.gelu(z_scr[rows, COL_V:COL_V + GROUP_W])
        v = _head_layer_norm(v, sng_ref[...], snb_ref[...], avg)
        v_stack = jnp.concatenate(
            [jnp.where(head_of_lane == hd, v, 0.0).astype(bf16) for hd in range(N_HEADS)], axis=0)
        s = _dot(w_cat, v_stack) + sb_ref[...]
        y_scr[c * CHUNK:(c + 1) * CHUNK, 2 * GROUP_W:3 * GROUP_W] = (u * s).astype(bf16)

    o_ref[...] = x_ref[...] + _dot(y_scr[...], wmo_ref[...])


def _mixer_prompt_call(x, batch, seq, layer, wts, avg):
    tm, ns = MIX_TILE, MIX_SEQS
    x = x.reshape(batch, seq, D_MODEL)
    row_spec = pl.BlockSpec((ns, tm, D_MODEL), lambda b, l: (b, l, 0))

    def state_spec(rows):
        return pl.BlockSpec((ns, rows, GROUP_W), lambda b, l: (b, 0, 0))

    in_specs = [row_spec] + [_resident(w, layer) for w in wts] + [_resident(avg)]
    out = pl.pallas_call(
        functools.partial(_mixer_prompt_kernel, layer=layer),
        grid=(batch // ns, seq // tm),
        in_specs=in_specs,
        out_specs=[row_spec, state_spec(CONV_A_WIDTH - 1), state_spec(POOL_STATE),
                   state_spec(SHORT_CONV_WIDTH - 1)],
        out_shape=[jax.ShapeDtypeStruct((batch, seq, D_MODEL), f32),
                   jax.ShapeDtypeStruct((batch, CONV_A_WIDTH - 1, GROUP_W), f32),
                   jax.ShapeDtypeStruct((batch, POOL_STATE, GROUP_W), f32),
                   jax.ShapeDtypeStruct((batch, SHORT_CONV_WIDTH - 1, GROUP_W), f32)],
        scratch_shapes=[pltpu.VMEM((ns, MIX_HALO + tm, MIX_IN_W), f32),
                        pltpu.VMEM((ns, V7X_SUBLANES - 1, MIX_HALO + tm, 2 * GROUP_W), f32),
                        pltpu.VMEM((ns, tm, GROUP_W), f32),
                        pltpu.VMEM((ns, tm, GROUP_W), bf16),
                        pltpu.VMEM((ns, tm, D_MODEL), bf16)],
        compiler_params=pltpu.CompilerParams(
            dimension_semantics=("arbitrary", "arbitrary"),
            vmem_limit_bytes=V7X_VMEM_LIMIT_BYTES),
        name="mixer_prompt",
    )(x, *wts, avg)
    return (out[0].reshape(batch * seq, D_MODEL),) + tuple(out[1:])


def _mixer_sample_kernel(x_ref, gm_ref, wmi_ref, wmo_ref, caw_ref, cab_ref, nag_ref, nab_ref,
                         pw_ref, ps_ref, sng_ref, snb_ref, coef_ref, sb_ref, scw_ref, avg_ref,
                         sta_ref, stp_ref, sts_ref,
                         o_ref, sa_ref, sp_ref, ss_ref, v_ref,
                         z_scr, xa, xp, xs, c_scr, y_scr, *, dec_seq, layer):
    gm_ref, cab_ref, nag_ref, nab_ref, ps_ref, sng_ref, snb_ref = (
        _LayerRow(r, layer) for r in (gm_ref, cab_ref, nag_ref, nab_ref, ps_ref, sng_ref, snb_ref))
    nb, steps = SAMPLE_SEQS, dec_seq
    rows = nb * steps
    hist_a, hist_p, hist_s = CONV_A_WIDTH - 1, POOL_STATE, SHORT_CONV_WIDTH - 1
    avg = avg_ref[...]

    def slabs(col):
        return z_scr[:, col:col + GROUP_W].reshape(steps, nb, GROUP_W)

    x = x_ref[...].reshape(rows, D_MODEL)
    z_scr[...] = _dot(_rms(x, gm_ref[...]).astype(bf16), wmi_ref[...])

    xa[0:hist_a] = sta_ref[...]
    xa[hist_a:hist_a + steps] = slabs(COL_A_VAL) * jax.nn.sigmoid(slabs(COL_A_GATE))
    sa_ref[...] = xa[steps:steps + hist_a]
    for t in range(steps):
        acc = jnp.broadcast_to(cab_ref[...], (nb, GROUP_W))
        for k in range(CONV_A_WIDTH):
            acc = acc + caw_ref[k:k + 1, :] * xa[t + k]
        c_scr[t] = acc
    ya = _head_layer_norm(c_scr[...].reshape(rows, GROUP_W), nag_ref[...], nab_ref[...], avg)
    y_scr[:, 0:GROUP_W] = _silu(ya).astype(bf16)

    xp[0:hist_p] = stp_ref[...]
    xp[hist_p:hist_p + steps] = slabs(COL_ZB)
    sp_ref[...] = xp[steps:steps + hist_p]
    lane_group = lax.broadcasted_iota(jnp.int32, (nb, GROUP_W), 1) // HEAD_DIM
    win = jnp.left_shift(2, lane_group)
    for t in range(steps):
        zb = xp[hist_p + t]
        run = zb
        sums = []
        for j in range(1, max(POOL_WINDOWS)):
            run = run + xp[hist_p + t - j]
            if j + 1 in POOL_WINDOWS:
                sums.append(run)
        cnt = jnp.minimum(PAST_LEN + t + 1, win).astype(f32)
        c_scr[t] = _pool_select(sums, lane_group) / cnt - zb
    d = c_scr[...].reshape(rows, GROUP_W).astype(bf16)
    y_scr[:, GROUP_W:2 * GROUP_W] = (_dot(d, pw_ref[...]) * ps_ref[...]).astype(bf16)

    v = _head_layer_norm(jax.nn.gelu(z_scr[:, COL_V:COL_V + GROUP_W]),
                         sng_ref[...], snb_ref[...], avg)
    v_ref[...] = v.reshape(steps, nb, GROUP_W)
    for t in range(steps):
        s = jnp.broadcast_to(sb_ref[t:t + 1, :], (nb, GROUP_W))
        for j in range(t + 1):
            s = s + coef_ref[j, t:t + 1, :] * v_ref[j]
        u = jax.nn.gelu(z_scr[t * nb:(t + 1) * nb, COL_U:COL_U + GROUP_W])
        y_scr[t * nb:(t + 1) * nb, 2 * GROUP_W:3 * GROUP_W] = (u * s).astype(bf16)

    xs[0:hist_s] = sts_ref[...]
    xs[hist_s:hist_s + steps] = slabs(COL_C_GATE) * slabs(COL_D_IN)
    ss_ref[...] = xs[steps:steps + hist_s]
    for t in range(steps):
        conv = scw_ref[0:1, :] * xs[t]
        for k in range(1, SHORT_CONV_WIDTH):
            conv = conv + scw_ref[k:k + 1, :] * xs[t + k]
        b_gate = z_scr[t * nb:(t + 1) * nb, COL_B_GATE:COL_B_GATE + GROUP_W]
        y_scr[t * nb:(t + 1) * nb, 3 * GROUP_W:4 * GROUP_W] = (b_gate * conv).astype(bf16)

    out = x_ref[...].reshape(rows, D_MODEL) + _dot(y_scr[...], wmo_ref[...])
    o_ref[...] = out.reshape(steps, nb, D_MODEL)


def _mixer_sample_call(x, st_a, st_p, st_s, dec_batch, dec_seq, layer, wts, avg):
    nb = SAMPLE_SEQS
    rows = nb * dec_seq
    row_spec = pl.BlockSpec((dec_seq, nb, D_MODEL), lambda i: (0, i, 0))

    def state_spec(r):
        return pl.BlockSpec((r, nb, GROUP_W), lambda i: (0, i, 0))

    def state_in_spec(r):
        return pl.BlockSpec((None, r, nb, GROUP_W), lambda i: (layer, 0, i, 0))

    def state_shape(r):
        return jax.ShapeDtypeStruct((r, dec_batch, GROUP_W), f32)

    hist_a, hist_p, hist_s = CONV_A_WIDTH - 1, POOL_STATE, SHORT_CONV_WIDTH - 1
    in_specs = ([row_spec] + [_resident(w, layer) for w in wts] + [_resident(avg)]
                + [state_in_spec(hist_a), state_in_spec(hist_p), state_in_spec(hist_s)])
    return pl.pallas_call(
        functools.partial(_mixer_sample_kernel, dec_seq=dec_seq, layer=layer),
        grid=(dec_batch // nb,),
        in_specs=in_specs,
        out_specs=[row_spec, state_spec(hist_a), state_spec(hist_p), state_spec(hist_s),
                   state_spec(dec_seq)],
        out_shape=[jax.ShapeDtypeStruct((dec_seq, dec_batch, D_MODEL), f32),
                   state_shape(hist_a), state_shape(hist_p), state_shape(hist_s),
                   state_shape(dec_seq)],
        scratch_shapes=[pltpu.VMEM((rows, MIX_IN_W), f32),
                        pltpu.VMEM((hist_a + dec_seq, nb, GROUP_W), f32),
                        pltpu.VMEM((hist_p + dec_seq, nb, GROUP_W), f32),
                        pltpu.VMEM((hist_s + dec_seq, nb, GROUP_W), f32),
                        pltpu.VMEM((dec_seq, nb, GROUP_W), f32),
                        pltpu.VMEM((rows, D_MODEL), bf16)],
        compiler_params=pltpu.CompilerParams(
            dimension_semantics=("arbitrary",), vmem_limit_bytes=V7X_VMEM_LIMIT_BYTES),
        name="mixer_sample",
    )(x, *wts, avg, st_a, st_p, st_s)


def _block_diag(w):
    n, g, c, d = w.shape
    eye = jnp.eye(g, dtype=w.dtype)
    return (eye[None, :, None, :, None] * w[:, :, :, None, :]).reshape(n, g * c, g * d)


def kernel(x_prompt, x_sample, p_prompt, p_sample, state_conv_a, state_pool, state_short_conv,
           norm_ffn1, w_ffn1_gate, w_ffn1_up, w_ffn1_down, norm_mix, w_mix_in,
           conv_a_w, conv_a_b, norm_a_g, norm_a_b, pool_w, pool_scale,
           sgu_norm_g, sgu_norm_b, sgu_w, sgu_b, short_conv_w, w_mix_out,
           norm_ffn2, w_ffn2_gate, w_ffn2_up, w_ffn2_down,
           norm_ple, w_ple_gate, w_ple_proj, norm_final):
    depth = w_mix_in.shape[0]
    batch, seq, _ = x_prompt.shape
    dec_batch, dec_seq, _ = x_sample.shape
    assert seq % MIX_TILE == 0 and MIX_TILE % CHUNK == 0 and dec_seq <= CHUNK
    assert batch % MIX_SEQS == 0
    assert dec_batch % SAMPLE_SEQS == 0 and dec_seq == V7X_SUBLANES
    assert (batch * seq) % FFN_TILE == 0 and (dec_batch * dec_seq) % FFN_TILE == 0
    assert all(c % 256 == 0 for c in FFN_COL_SPLITS)

    avg = _block_diag(jnp.full((1, N_HEADS, HEAD_DIM, HEAD_DIM), 1.0 / HEAD_DIM, f32))[0]
    avg = avg.astype(bf16)

    ffn1 = (norm_ffn1, w_ffn1_gate.astype(bf16), w_ffn1_up.astype(bf16),
            w_ffn1_down.astype(bf16))
    ffn2 = (norm_ffn2, w_ffn2_gate.astype(bf16), w_ffn2_up.astype(bf16),
            w_ffn2_down.astype(bf16))
    ple_w = (norm_ple, w_ple_gate.astype(bf16), w_ple_proj.astype(bf16))
    gf = norm_final.reshape(1, D_MODEL)
    mix_head = (norm_mix, w_mix_in.astype(bf16), w_mix_out.astype(bf16),
                conv_a_w, conv_a_b, norm_a_g, norm_a_b,
                _block_diag(pool_w).astype(bf16), pool_scale,
                sgu_norm_g, sgu_norm_b)
    sw_prompt = jnp.transpose(sgu_w, (0, 2, 1, 3)).reshape(depth, CHUNK, N_HEADS * CHUNK)
    sb_prompt = jnp.repeat(jnp.swapaxes(sgu_b, 1, 2), HEAD_DIM, axis=2)
    sw_sample = jnp.repeat(jnp.transpose(sgu_w[:, :, :dec_seq, :dec_seq], (0, 3, 2, 1)),
                           HEAD_DIM, axis=3)
    sb_sample = sb_prompt[:, :dec_seq]
    mix_prompt_w = mix_head + (sw_prompt, sb_prompt, short_conv_w)
    mix_sample_w = mix_head + (sw_sample, sb_sample, short_conv_w)

    sample_rows = dec_seq * dec_batch
    pp = p_prompt.reshape(depth, batch * seq, D_PLE)
    ps = jnp.swapaxes(p_sample, 1, 2).reshape(depth, sample_rows, D_PLE)
    xp = x_prompt.reshape(batch * seq, D_MODEL)
    xs = jnp.swapaxes(x_sample, 0, 1).reshape(sample_rows, D_MODEL)
    st_a, st_p, st_s = (jnp.swapaxes(st, 1, 2)
                        for st in (state_conv_a, state_pool, state_short_conv))
    prompt_states, sample_states = [], []
    for i in range(depth):
        last = i == depth - 1

        xs, xp = _ffn_call(xs, xp, i, ffn1)
        xp, sa, sp, ss = _mixer_prompt_call(xp, batch, seq, i, mix_prompt_w, avg)
        prompt_states.append((sa, sp, ss))
        xs, sa, sp, ss, sv = _mixer_sample_call(
            xs.reshape(dec_seq, dec_batch, D_MODEL), st_a, st_p, st_s, dec_batch, dec_seq, i,
            mix_sample_w, avg)
        xs = xs.reshape(sample_rows, D_MODEL)
        sample_states.append((sa, sp, ss, sv))
        xs, xp = _ffn_call(xs, xp, i, ffn2, ple=(ps, pp, ple_w, gf), final_norm=last)

    def stack(states, j):
        return jnp.stack([s[j] for s in states])

    def stack_sample(j):
        return jnp.swapaxes(stack(sample_states, j), 1, 2)

    y_sample = jnp.swapaxes(xs.reshape(dec_seq, dec_batch, D_MODEL), 0, 1)
    return (xp.reshape(batch, seq, D_MODEL), y_sample,
            stack(prompt_states, 0), stack_sample(0),
            stack(prompt_states, 1), stack_sample(1),
            stack(prompt_states, 2), stack_sample(2),
            stack_sample(3))
```

```python
import functools

import jax
import jax.numpy as jnp
from jax import lax
from jax.experimental import pallas as pl
from jax.experimental.pallas import tpu as pltpu

D_MODEL = 1024
D_PLE = 256
D_FF = 2816
GROUP_W = 256
N_HEADS = 4
HEAD_DIM = 64
MIX_IN_W = 8 * GROUP_W
CONV_A_WIDTH = 31
POOL_WINDOWS = (2, 4, 8, 16)
POOL_STATE = 15
CHUNK = 128
SHORT_CONV_WIDTH = 3
EPS = 1e-6
PAST_LEN = 16384

V7X_SUBLANES = 8
V7X_VMEM_BYTES = 64 * 1024 * 1024
V7X_VMEM_LIMIT_BYTES = V7X_VMEM_BYTES - 4 * 1024 * 1024

FFN_TILE = 1024
FFN_SUB_TILE = 512
FFN_COL_SPLITS = (0, 1536, D_FF)
MIX_TILE = 512
MIX_SEQS = 2
MIX_HALO = 32
MIX_ROWS = 64
SAMPLE_SEQS = 64

COL_A_VAL, COL_A_GATE, COL_ZB, COL_U, COL_V, COL_B_GATE, COL_C_GATE, COL_D_IN = (
    i * GROUP_W for i in range(8))

bf16 = jnp.bfloat16
f32 = jnp.float32


def _dot(a, b):
    return jnp.dot(a, b, preferred_element_type=f32)


def _rms(x, g):
    ms = jnp.mean(x * x, axis=-1, keepdims=True)
    return x * lax.rsqrt(ms + EPS) * g


def _group_mean(x, avg):
    hi = x.astype(bf16)
    lo = (x - hi.astype(f32)).astype(bf16)
    return _dot(hi, avg) + _dot(lo, avg)


def _head_layer_norm(x, g, b, avg):
    mu = _group_mean(x, avg)
    xc = x - mu
    var = _group_mean(xc * xc, avg)
    return xc * lax.rsqrt(var + EPS) * g + b


def _silu(x):
    return x * jax.nn.sigmoid(x)


def _pool_select(sums, lane_group):
    s2, s4, s8, s16 = sums
    return jnp.where(lane_group == 0, s2,
                     jnp.where(lane_group == 1, s4, jnp.where(lane_group == 2, s8, s16)))


def _ffn_rows(x_ref, rows, g_ref, wg_ref, wu_ref, wd_ref, act_ref):
    h = _rms(x_ref[rows, :], g_ref[...]).astype(bf16)
    for lo, hi in zip(FFN_COL_SPLITS[:-1], FFN_COL_SPLITS[1:]):
        cs = slice(lo, hi)
        gate = _dot(h, wg_ref[:, cs])
        up = _dot(h, wu_ref[:, cs])
        act_ref[rows, cs] = (_silu(gate) * up).astype(bf16)
    return x_ref[rows, :] + 0.5 * _dot(act_ref[rows, :], wd_ref[...])


def _sub_tiles():
    return [pl.ds(r, FFN_SUB_TILE) for r in range(0, FFN_TILE, FFN_SUB_TILE)]


def _ffn_kernel(xs_ref, xp_ref, g_ref, wg_ref, wu_ref, wd_ref, os_ref, op_ref, act_ref, *,
                sample_steps, layer):
    g_ref = _LayerRow(g_ref, layer)

    def step(x_ref, o_ref):
        for rows in _sub_tiles():
            o_ref[rows, :] = _ffn_rows(x_ref, rows, g_ref, wg_ref, wu_ref, wd_ref, act_ref)

    is_sample = pl.program_id(0) < sample_steps
    pl.when(is_sample)(functools.partial(step, xs_ref, os_ref))
    pl.when(jnp.logical_not(is_sample))(functools.partial(step, xp_ref, op_ref))


def _ffn_ple_kernel(xs_ref, xp_ref, g_ref, wg_ref, wu_ref, wd_ref, ps_ref, pp_ref, gp_ref,
                    wpg_ref, wpp_ref, gf_ref, os_ref, op_ref, act_ref, *, sample_steps,
                    layer, final_norm):
    g_ref, gp_ref = _LayerRow(g_ref, layer), _LayerRow(gp_ref, layer)

    def step(x_ref, p_ref, o_ref):
        for rows in _sub_tiles():
            o_ref[rows, :] = _ffn_rows(x_ref, rows, g_ref, wg_ref, wu_ref, wd_ref, act_ref)
            gate = jax.nn.sigmoid(
                _dot(_rms(o_ref[rows, :], gp_ref[...]).astype(bf16), wpg_ref[...]))
            x = o_ref[rows, :] + gate * _dot(p_ref[rows, :].astype(bf16), wpp_ref[...])
            if final_norm:
                x = _rms(x, gf_ref[...])
            o_ref[rows, :] = x

    is_sample = pl.program_id(0) < sample_steps
    pl.when(is_sample)(functools.partial(step, xs_ref, ps_ref, os_ref))
    pl.when(jnp.logical_not(is_sample))(functools.partial(step, xp_ref, pp_ref, op_ref))


class _LayerRow:
    def __init__(self, ref, layer):
        self.ref, self.layer = ref, layer

    def __getitem__(self, idx):
        assert idx is Ellipsis
        return self.ref[self.layer:self.layer + 1, :]


def _resident(w, layer=None):
    if layer is None or w.ndim == 2:
        return pl.BlockSpec(w.shape, lambda *_: (0,) * w.ndim, pipeline_mode=pl.Buffered(1))
    return pl.BlockSpec((None,) + w.shape[1:], lambda *_: (layer,) + (0,) * (w.ndim - 1),
                        pipeline_mode=pl.Buffered(1))


def _ffn_call(xs, xp, layer, ffn_w, ple=None, final_norm=False):
    tile = FFN_TILE
    ns, np_ = xs.shape[0] // tile, xp.shape[0] // tile

    def sample_tile(i):
        return jnp.minimum(i, ns - 1)

    def prompt_tile(i):
        return jnp.maximum(i - ns, 0)

    xs_spec = pl.BlockSpec((tile, D_MODEL), lambda i: (sample_tile(i), 0),
                           pipeline_mode=pl.Buffered(1))
    xp_spec = pl.BlockSpec((tile, D_MODEL), lambda i: (prompt_tile(i), 0))
    in_specs = [xs_spec, xp_spec] + [_resident(w, layer) for w in ffn_w]
    args = [xs, xp, *ffn_w]
    if ple is None:
        body = functools.partial(_ffn_kernel, sample_steps=ns, layer=layer)
        name = "ffn"
    else:
        ps, pp, ple_w, gf = ple
        body = functools.partial(_ffn_ple_kernel, sample_steps=ns, layer=layer,
                                 final_norm=final_norm)
        name = "ffn_ple"
        in_specs += ([pl.BlockSpec((None, tile, D_PLE), lambda i: (layer, sample_tile(i), 0),
                                   pipeline_mode=pl.Buffered(1)),
                      pl.BlockSpec((None, tile, D_PLE), lambda i: (layer, prompt_tile(i), 0))]
                     + [_resident(w, layer) for w in ple_w] + [_resident(gf)])
        args += [ps, pp, *ple_w, gf]
    return pl.pallas_call(
        body,
        grid=(ns + np_,),
        in_specs=in_specs,
        out_specs=[xs_spec, xp_spec],
        out_shape=[jax.ShapeDtypeStruct(xs.shape, f32), jax.ShapeDtypeStruct(xp.shape, f32)],
        scratch_shapes=[pltpu.VMEM((tile, D_FF), bf16)],
        compiler_params=pltpu.CompilerParams(
            dimension_semantics=("arbitrary",), vmem_limit_bytes=V7X_VMEM_LIMIT_BYTES),
        name=name,
    )(*args)


def _mixer_prompt_kernel(x_ref, gm_ref, wmi_ref, wmo_ref, caw_ref, cab_ref, nag_ref, nab_ref,
                         pw_ref, ps_ref, sng_ref, snb_ref, sw_ref, sb_ref, scw_ref, avg_ref,
                         o_ref, sa_ref, sp_ref, ss_ref,
                         z_all, sh_all, c_all, d_all, y_all, *, layer):
    gm_ref, cab_ref, nag_ref, nab_ref, ps_ref, sng_ref, snb_ref = (
        _LayerRow(r, layer) for r in (gm_ref, cab_ref, nag_ref, nab_ref, ps_ref, sng_ref, snb_ref))
    tm, halo = MIX_TILE, MIX_HALO
    sub, half = V7X_SUBLANES, GROUP_W // 2
    step = pl.program_id(1)
    weights = (gm_ref, wmi_ref, wmo_ref, caw_ref, cab_ref, nag_ref, nab_ref, pw_ref, ps_ref,
               sng_ref, snb_ref, sw_ref, sb_ref, scw_ref, avg_ref)

    @pl.when(step == 0)
    def _():
        z_all[:, 0:halo, :] = jnp.zeros((MIX_SEQS, halo, MIX_IN_W), f32)
        sh_all[:, :, 0:halo, :] = jnp.zeros((MIX_SEQS, sub - 1, halo, 2 * GROUP_W), f32)

    for s in range(MIX_SEQS):
        h = _rms(x_ref[s], gm_ref[...]).astype(bf16)
        z_all[s, halo:halo + tm, :] = _dot(h, wmi_ref[...])
    for s in range(MIX_SEQS):
        _mixer_prompt_tile(step, x_ref.at[s], o_ref.at[s], weights, z_all.at[s], sh_all.at[s],
                           c_all.at[s], d_all.at[s], y_all.at[s])

    @pl.when(step == pl.num_programs(1) - 1)
    def _():
        for s in range(MIX_SEQS):
            z_scr = z_all.at[s]
            sa_ref[s] = z_scr[pl.ds(halo + tm - (CONV_A_WIDTH - 1), CONV_A_WIDTH - 1),
                              COL_A_VAL:COL_A_VAL + GROUP_W]
            sp_ref[s] = z_scr[pl.ds(halo + tm - POOL_STATE, POOL_STATE),
                              COL_ZB:COL_ZB + GROUP_W]
            ss_ref[s] = z_scr[pl.ds(halo + tm - (SHORT_CONV_WIDTH - 1), SHORT_CONV_WIDTH - 1),
                              COL_D_IN:COL_D_IN + GROUP_W]

    for s in range(MIX_SEQS):
        z_scr, sh_scr = z_all.at[s], sh_all.at[s]
        for col in (COL_A_VAL, COL_ZB, COL_D_IN):
            z_scr[0:halo, col:col + GROUP_W] = z_scr[tm:tm + halo, col:col + GROUP_W]
        sh_scr[0:3, 0:halo, :] = sh_scr[0:3, tm:tm + halo, :]
        for lanes in (slice(0, GROUP_W), slice(GROUP_W + half, 2 * GROUP_W)):
            sh_scr[3:sub - 1, 0:halo, lanes] = sh_scr[3:sub - 1, tm:tm + halo, lanes]


def _mixer_prompt_tile(step, x_ref, o_ref, weights, z_scr, sh_scr, c_scr, d_scr, y_scr):
    (gm_ref, wmi_ref, wmo_ref, caw_ref, cab_ref, nag_ref, nab_ref, pw_ref, ps_ref,
     sng_ref, snb_ref, sw_ref, sb_ref, scw_ref, avg_ref) = weights
    tm, halo, rb = MIX_TILE, MIX_HALO, MIX_ROWS
    sub, half = V7X_SUBLANES, GROUP_W // 2

    for r in range(0, tm, rb):
        rows = pl.ds(halo + r, rb)
        a_val = z_scr[rows, COL_A_VAL:COL_A_VAL + GROUP_W]
        a_gate = z_scr[rows, COL_A_GATE:COL_A_GATE + GROUP_W]
        z_scr[rows, COL_A_VAL:COL_A_VAL + GROUP_W] = a_val * jax.nn.sigmoid(a_gate)
        c_gate = z_scr[rows, COL_C_GATE:COL_C_GATE + GROUP_W]
        d_in = z_scr[rows, COL_D_IN:COL_D_IN + GROUP_W]
        z_scr[rows, COL_D_IN:COL_D_IN + GROUP_W] = c_gate * d_in
        for b in range(1, sub):
            late = pl.ds(halo + r - b, rb)
            sh_scr[b - 1, rows, 0:GROUP_W] = z_scr[late, COL_A_VAL:COL_A_VAL + GROUP_W]
            if b < 4:
                sh_scr[b - 1, rows, GROUP_W:2 * GROUP_W] = z_scr[late, COL_ZB:COL_ZB + GROUP_W]
            else:
                sh_scr[b - 1, rows, GROUP_W + half:2 * GROUP_W] = (
                    z_scr[late, COL_ZB + half:COL_ZB + GROUP_W])

    lane_group = lax.broadcasted_iota(jnp.int32, (rb, GROUP_W), 1) // HEAD_DIM
    win = jnp.left_shift(2, lane_group)
    row_iota = lax.broadcasted_iota(jnp.int32, (rb, GROUP_W), 0)
    first_head = lax.broadcasted_iota(jnp.int32, (rb, half), 1) < HEAD_DIM

    for r in range(0, tm, rb):
        def delayed(col, lanes, b, back):
            at = pl.ds(halo + r - back, rb)
            if b == 0:
                return z_scr[at, col + lanes.start:col + lanes.stop]
            base = 0 if col == COL_A_VAL else GROUP_W
            return sh_scr[b - 1, at, base + lanes.start:base + lanes.stop]

        acc = jnp.broadcast_to(cab_ref[...], (rb, GROUP_W))
        for j in range(CONV_A_WIDTH):
            k = CONV_A_WIDTH - 1 - j
            tap = delayed(COL_A_VAL, slice(0, GROUP_W), j % sub, j - j % sub)
            acc = acc + caw_ref[k:k + 1, :] * tap
        c_scr[r:r + rb, :] = acc

        lo, hi = slice(0, half), slice(half, GROUP_W)
        s2 = delayed(COL_ZB, lo, 0, 0) + delayed(COL_ZB, lo, 1, 0)
        s4 = s2 + delayed(COL_ZB, lo, 2, 0) + delayed(COL_ZB, lo, 3, 0)
        s8 = delayed(COL_ZB, hi, 0, 0)
        for b in range(1, sub):
            s8 = s8 + delayed(COL_ZB, hi, b, 0)
        s16 = s8
        for b in range(sub):
            s16 = s16 + delayed(COL_ZB, hi, b, sub)
        sums = jnp.concatenate(
            [jnp.where(first_head, s2, s4), jnp.where(first_head, s8, s16)], axis=1)
        zb = z_scr[pl.ds(halo + r, rb), COL_ZB:COL_ZB + GROUP_W]
        pos = step * tm + r + row_iota
        cnt = jnp.minimum(pos + 1, win).astype(f32)
        d_scr[r:r + rb, :] = (sums / cnt - zb).astype(bf16)

        rows = pl.ds(halo + r, rb)
        q0 = z_scr[pl.ds(halo + r - 2, rb), COL_D_IN:COL_D_IN + GROUP_W]
        q1 = z_scr[pl.ds(halo + r - 1, rb), COL_D_IN:COL_D_IN + GROUP_W]
        q2 = z_scr[rows, COL_D_IN:COL_D_IN + GROUP_W]
        conv = scw_ref[0:1, :] * q0 + scw_ref[1:2, :] * q1 + scw_ref[2:3, :] * q2
        b_gate = z_scr[rows, COL_B_GATE:COL_B_GATE + GROUP_W]
        y_scr[r:r + rb, 3 * GROUP_W:4 * GROUP_W] = (b_gate * conv).astype(bf16)

    avg = avg_ref[...]
    ya = _head_layer_norm(c_scr[...], nag_ref[...], nab_ref[...], avg)
    y_scr[:, 0:GROUP_W] = _silu(ya).astype(bf16)
    y_scr[:, GROUP_W:2 * GROUP_W] = (_dot(d_scr[...], pw_ref[...]) * ps_ref[...]).astype(bf16)

    t_idx = lax.broadcasted_iota(jnp.int32, (CHUNK, N_HEADS * CHUNK), 0)
    s_idx = lax.broadcasted_iota(jnp.int32, (CHUNK, N_HEADS * CHUNK), 1) % CHUNK
    w_cat = jnp.where(s_idx <= t_idx, sw_ref[...], 0.0).astype(bf16)
    head_of_lane = lax.broadcasted_iota(jnp.int32, (CHUNK, GROUP_W), 1) // HEAD_DIM
    for c in range(tm // CHUNK):
        rows = pl.ds(halo + c * CHUNK, CHUNK)
        u = jax.nn.gelu(z_scr[rows, COL_U:COL_U + GROUP_W])
        v = jax.nn.gelu(z_scr[rows, COL_V:COL_V + GROUP_W])
        v = _head_layer_norm(v, sng_ref[...], snb_ref[...], avg)
        v_stack = jnp.concatenate(
            [jnp.where(head_of_lane == hd, v, 0.0).astype(bf16) for hd in range(N_HEADS)], axis=0)
        s = _dot(w_cat, v_stack) + sb_ref[...]
        y_scr[c * CHUNK:(c + 1) * CHUNK, 2 * GROUP_W:3 * GROUP_W] = (u * s).astype(bf16)

    o_ref[...] = x_ref[...] + _dot(y_scr[...], wmo_ref[...])


def _mixer_prompt_call(x, batch, seq, layer, wts, avg):
    tm, ns = MIX_TILE, MIX_SEQS
    x = x.reshape(batch, seq, D_MODEL)
    row_spec = pl.BlockSpec((ns, tm, D_MODEL), lambda b, l: (b, l, 0))

    def state_spec(rows):
        return pl.BlockSpec((ns, rows, GROUP_W), lambda b, l: (b, 0, 0))

    in_specs = [row_spec] + [_resident(w, layer) for w in wts] + [_resident(avg)]
    out = pl.pallas_call(
        functools.partial(_mixer_prompt_kernel, layer=layer),
        grid=(batch // ns, seq // tm),
        in_specs=in_specs,
        out_specs=[row_spec, state_spec(CONV_A_WIDTH - 1), state_spec(POOL_STATE),
                   state_spec(SHORT_CONV_WIDTH - 1)],
        out_shape=[jax.ShapeDtypeStruct((batch, seq, D_MODEL), f32),
                   jax.ShapeDtypeStruct((batch, CONV_A_WIDTH - 1, GROUP_W), f32),
                   jax.ShapeDtypeStruct((batch, POOL_STATE, GROUP_W), f32),
                   jax.ShapeDtypeStruct((batch, SHORT_CONV_WIDTH - 1, GROUP_W), f32)],
        scratch_shapes=[pltpu.VMEM((ns, MIX_HALO + tm, MIX_IN_W), f32),
                        pltpu.VMEM((ns, V7X_SUBLANES - 1, MIX_HALO + tm, 2 * GROUP_W), f32),
                        pltpu.VMEM((ns, tm, GROUP_W), f32),
                        pltpu.VMEM((ns, tm, GROUP_W), bf16),
                        pltpu.VMEM((ns, tm, D_MODEL), bf16)],
        compiler_params=pltpu.CompilerParams(
            dimension_semantics=("arbitrary", "arbitrary"),
            vmem_limit_bytes=V7X_VMEM_LIMIT_BYTES),
        name="mixer_prompt",
    )(x, *wts, avg)
    return (out[0].reshape(batch * seq, D_MODEL),) + tuple(out[1:])


def _mixer_sample_kernel(x_ref, gm_ref, wmi_ref, wmo_ref, caw_ref, cab_ref, nag_ref, nab_ref,
                         pw_ref, ps_ref, sng_ref, snb_ref, coef_ref, sb_ref, scw_ref, avg_ref,
                         sta_ref, stp_ref, sts_ref,
                         o_ref, sa_ref, sp_ref, ss_ref, v_ref,
                         z_scr, xa, xp, xs, c_scr, y_scr, *, dec_seq, layer):
    gm_ref, cab_ref, nag_ref, nab_ref, ps_ref, sng_ref, snb_ref = (
        _LayerRow(r, layer) for r in (gm_ref, cab_ref, nag_ref, nab_ref, ps_ref, sng_ref, snb_ref))
    nb, steps = SAMPLE_SEQS, dec_seq
    rows = nb * steps
    hist_a, hist_p, hist_s = CONV_A_WIDTH - 1, POOL_STATE, SHORT_CONV_WIDTH - 1
    avg = avg_ref[...]

    def slabs(col):
        return z_scr[:, col:col + GROUP_W].reshape(steps, nb, GROUP_W)

    x = x_ref[...].reshape(rows, D_MODEL)
    z_scr[...] = _dot(_rms(x, gm_ref[...]).astype(bf16), wmi_ref[...])

    xa[0:hist_a] = sta_ref[...]
    xa[hist_a:hist_a + steps] = slabs(COL_A_VAL) * jax.nn.sigmoid(slabs(COL_A_GATE))
    sa_ref[...] = xa[steps:steps + hist_a]
    for t in range(steps):
        acc = jnp.broadcast_to(cab_ref[...], (nb, GROUP_W))
        for k in range(CONV_A_WIDTH):
            acc = acc + caw_ref[k:k + 1, :] * xa[t + k]
        c_scr[t] = acc
    ya = _head_layer_norm(c_scr[...].reshape(rows, GROUP_W), nag_ref[...], nab_ref[...], avg)
    y_scr[:, 0:GROUP_W] = _silu(ya).astype(bf16)

    xp[0:hist_p] = stp_ref[...]
    xp[hist_p:hist_p + steps] = slabs(COL_ZB)
    sp_ref[...] = xp[steps:steps + hist_p]
    lane_group = lax.broadcasted_iota(jnp.int32, (nb, GROUP_W), 1) // HEAD_DIM
    win = jnp.left_shift(2, lane_group)
    for t in range(steps):
        zb = xp[hist_p + t]
        run = zb
        sums = []
        for j in range(1, max(POOL_WINDOWS)):
            run = run + xp[hist_p + t - j]
            if j + 1 in POOL_WINDOWS:
                sums.append(run)
        cnt = jnp.minimum(PAST_LEN + t + 1, win).astype(f32)
        c_scr[t] = _pool_select(sums, lane_group) / cnt - zb
    d = c_scr[...].reshape(rows, GROUP_W).astype(bf16)
    y_scr[:, GROUP_W:2 * GROUP_W] = (_dot(d, pw_ref[...]) * ps_ref[...]).astype(bf16)

    v = _head_layer_norm(jax.nn.gelu(z_scr[:, COL_V:COL_V + GROUP_W]),
                         sng_ref[...], snb_ref[...], avg)
    v_ref[...] = v.reshape(steps, nb, GROUP_W)
    for t in range(steps):
        s = jnp.broadcast_to(sb_ref[t:t + 1, :], (nb, GROUP_W))
        for j in range(t + 1):
            s = s + coef_ref[j, t:t + 1, :] * v_ref[j]
        u = jax.nn.gelu(z_scr[t * nb:(t + 1) * nb, COL_U:COL_U + GROUP_W])
        y_scr[t * nb:(t + 1) * nb, 2 * GROUP_W:3 * GROUP_W] = (u * s).astype(bf16)

    xs[0:hist_s] = sts_ref[...]
    xs[hist_s:hist_s + steps] = slabs(COL_C_GATE) * slabs(COL_D_IN)
    ss_ref[...] = xs[steps:steps + hist_s]
    for t in range(steps):
        conv = scw_ref[0:1, :] * xs[t]
        for k in range(1, SHORT_CONV_WIDTH):
            conv = conv + scw_ref[k:k + 1, :] * xs[t + k]
        b_gate = z_scr[t * nb:(t + 1) * nb, COL_B_GATE:COL_B_GATE + GROUP_W]
        y_scr[t * nb:(t + 1) * nb, 3 * GROUP_W:4 * GROUP_W] = (b_gate * conv).astype(bf16)

    out = x_ref[...].reshape(rows, D_MODEL) + _dot(y_scr[...], wmo_ref[...])
    o_ref[...] = out.reshape(steps, nb, D_MODEL)


def _mixer_sample_call(x, st_a, st_p, st_s, dec_batch, dec_seq, layer, wts, avg):
    nb = SAMPLE_SEQS
    rows = nb * dec_seq
    row_spec = pl.BlockSpec((dec_seq, nb, D_MODEL), lambda i: (0, i, 0))

    def state_spec(r):
        return pl.BlockSpec((r, nb, GROUP_W), lambda i: (0, i, 0))

    def state_in_spec(r):
        return pl.BlockSpec((None, r, nb, GROUP_W), lambda i: (layer, 0, i, 0))

    def state_shape(r):
        return jax.ShapeDtypeStruct((r, dec_batch, GROUP_W), f32)

    hist_a, hist_p, hist_s = CONV_A_WIDTH - 1, POOL_STATE, SHORT_CONV_WIDTH - 1
    in_specs = ([row_spec] + [_resident(w, layer) for w in wts] + [_resident(avg)]
                + [state_in_spec(hist_a), state_in_spec(hist_p), state_in_spec(hist_s)])
    return pl.pallas_call(
        functools.partial(_mixer_sample_kernel, dec_seq=dec_seq, layer=layer),
        grid=(dec_batch // nb,),
        in_specs=in_specs,
        out_specs=[row_spec, state_spec(hist_a), state_spec(hist_p), state_spec(hist_s),
                   state_spec(dec_seq)],
        out_shape=[jax.ShapeDtypeStruct((dec_seq, dec_batch, D_MODEL), f32),
                   state_shape(hist_a), state_shape(hist_p), state_shape(hist_s),
                   state_shape(dec_seq)],
        scratch_shapes=[pltpu.VMEM((rows, MIX_IN_W), f32),
                        pltpu.VMEM((hist_a + dec_seq, nb, GROUP_W), f32),
                        pltpu.VMEM((hist_p + dec_seq, nb, GROUP_W), f32),
                        pltpu.VMEM((hist_s + dec_seq, nb, GROUP_W), f32),
                        pltpu.VMEM((dec_seq, nb, GROUP_W), f32),
                        pltpu.VMEM((rows, D_MODEL), bf16)],
        compiler_params=pltpu.CompilerParams(
            dimension_semantics=("arbitrary",), vmem_limit_bytes=V7X_VMEM_LIMIT_BYTES),
        name="mixer_sample",
    )(x, *wts, avg, st_a, st_p, st_s)


def _block_diag(w):
    n, g, c, d = w.shape
    eye = jnp.eye(g, dtype=w.dtype)
    return (eye[None, :, None, :, None] * w[:, :, :, None, :]).reshape(n, g * c, g * d)


def kernel(x_prompt, x_sample, p_prompt, p_sample, state_conv_a, state_pool, state_short_conv,
           norm_ffn1, w_ffn1_gate, w_ffn1_up, w_ffn1_down, norm_mix, w_mix_in,
           conv_a_w, conv_a_b, norm_a_g, norm_a_b, pool_w, pool_scale,
           sgu_norm_g, sgu_norm_b, sgu_w, sgu_b, short_conv_w, w_mix_out,
           norm_ffn2, w_ffn2_gate, w_ffn2_up, w_ffn2_down,
           norm_ple, w_ple_gate, w_ple_proj, norm_final):
    depth = w_mix_in.shape[0]
    batch, seq, _ = x_prompt.shape
    dec_batch, dec_seq, _ = x_sample.shape
    assert seq % MIX_TILE == 0 and MIX_TILE % CHUNK == 0 and dec_seq <= CHUNK
    assert batch % MIX_SEQS == 0
    assert dec_batch % SAMPLE_SEQS == 0 and dec_seq == V7X_SUBLANES
    assert (batch * seq) % FFN_TILE == 0 and (dec_batch * dec_seq) % FFN_TILE == 0
    assert all(c % 256 == 0 for c in FFN_COL_SPLITS)

    avg = _block_diag(jnp.full((1, N_HEADS, HEAD_DIM, HEAD_DIM), 1.0 / HEAD_DIM, f32))[0]
    avg = avg.astype(bf16)

    ffn1 = (norm_ffn1, w_ffn1_gate.astype(bf16), w_ffn1_up.astype(bf16),
            w_ffn1_down.astype(bf16))
    ffn2 = (norm_ffn2, w_ffn2_gate.astype(bf16), w_ffn2_up.astype(bf16),
            w_ffn2_down.astype(bf16))
    ple_w = (norm_ple, w_ple_gate.astype(bf16), w_ple_proj.astype(bf16))
    gf = norm_final.reshape(1, D_MODEL)
    mix_head = (norm_mix, w_mix_in.astype(bf16), w_mix_out.astype(bf16),
                conv_a_w, conv_a_b, norm_a_g, norm_a_b,
                _block_diag(pool_w).astype(bf16), pool_scale,
                sgu_norm_g, sgu_norm_b)
    sw_prompt = jnp.transpose(sgu_w, (0, 2, 1, 3)).reshape(depth, CHUNK, N_HEADS * CHUNK)
    sb_prompt = jnp.repeat(jnp.swapaxes(sgu_b, 1, 2), HEAD_DIM, axis=2)
    sw_sample = jnp.repeat(jnp.transpose(sgu_w[:, :, :dec_seq, :dec_seq], (0, 3, 2, 1)),
                           HEAD_DIM, axis=3)
    sb_sample = sb_prompt[:, :dec_seq]
    mix_prompt_w = mix_head + (sw_prompt, sb_prompt, short_conv_w)
    mix_sample_w = mix_head + (sw_sample, sb_sample, short_conv_w)

    sample_rows = dec_seq * dec_batch
    pp = p_prompt.reshape(depth, batch * seq, D_PLE)
    ps = jnp.swapaxes(p_sample, 1, 2).reshape(depth, sample_rows, D_PLE)
    xp = x_prompt.reshape(batch * seq, D_MODEL)
    xs = jnp.swapaxes(x_sample, 0, 1).reshape(sample_rows, D_MODEL)
    st_a, st_p, st_s = (jnp.swapaxes(st, 1, 2)
                        for st in (state_conv_a, state_pool, state_short_conv))
    prompt_states, sample_states = [], []
    for i in range(depth):
        last = i == depth - 1

        xs, xp = _ffn_call(xs, xp, i, ffn1)
        xp, sa, sp, ss = _mixer_prompt_call(xp, batch, seq, i, mix_prompt_w, avg)
        prompt_states.append((sa, sp, ss))
        xs, sa, sp, ss, sv = _mixer_sample_call(
            xs.reshape(dec_seq, dec_batch, D_MODEL), st_a, st_p, st_s, dec_batch, dec_seq, i,
            mix_sample_w, avg)
        xs = xs.reshape(sample_rows, D_MODEL)
        sample_states.append((sa, sp, ss, sv))
        xs, xp = _ffn_call(xs, xp, i, ffn2, ple=(ps, pp, ple_w, gf), final_norm=last)

    def stack(states, j):
        return jnp.stack([s[j] for s in states])

    def stack_sample(j):
        return jnp.swapaxes(stack(sample_states, j), 1, 2)

    y_sample = jnp.swapaxes(xs.reshape(dec_seq, dec_batch, D_MODEL), 0, 1)
    return (xp.reshape(batch, seq, D_MODEL), y_sample,
            stack(prompt_states, 0), stack_sample(0),
            stack(prompt_states, 1), stack_sample(1),
            stack(prompt_states, 2), stack_sample(2),
            stack_sample(3))
```

```python
import functools

import jax
import jax.numpy as jnp
from jax import lax
from jax.experimental import pallas as pl
from jax.experimental.pallas import tpu as pltpu

D_MODEL = 1024
D_PLE = 256
D_FF = 2816
GROUP_W = 256
N_HEADS = 4
HEAD_DIM = 64
MIX_IN_W = 8 * GROUP_W
CONV_A_WIDTH = 31
POOL_WINDOWS = (2, 4, 8, 16)
POOL_STATE = 15
CHUNK = 128
SHORT_CONV_WIDTH = 3
EPS = 1e-6
PAST_LEN = 16384

V7X_SUBLANES = 8
V7X_BF16_ROWS = 16
V7X_VMEM_BYTES = 64 * 1024 * 1024
V7X_VMEM_LIMIT_BYTES = V7X_VMEM_BYTES - 4 * 1024 * 1024

FFN_TILE = 1024
FFN_SUB_TILE = 512
FFN_COL_SPLITS = (0, 1536, D_FF)
MIX_TILE = 512
MIX_SEQS = 2
MIX_HALO = 32
MIX_ROWS = 64
SAMPLE_SEQS = 64

COL_A_VAL, COL_A_GATE, COL_ZB, COL_U, COL_V, COL_B_GATE, COL_C_GATE, COL_D_IN = (
    i * GROUP_W for i in range(8))

bf16 = jnp.bfloat16
f32 = jnp.float32


def _dot(a, b):
    return jnp.dot(a, b, preferred_element_type=f32)


def _rms(x, g):
    ms = jnp.mean(x * x, axis=-1, keepdims=True)
    return x * lax.rsqrt(ms + EPS) * g


def _group_mean(x, avg):
    hi = x.astype(bf16)
    lo = (x - hi.astype(f32)).astype(bf16)
    return _dot(hi, avg) + _dot(lo, avg)


def _head_layer_norm(x, g, b, avg):
    mu = _group_mean(x, avg)
    xc = x - mu
    var = _group_mean(xc * xc, avg)
    return xc * lax.rsqrt(var + EPS) * g + b


def _silu(x):
    return x * jax.nn.sigmoid(x)


def _pool_select(sums, lane_group):
    s2, s4, s8, s16 = sums
    return jnp.where(lane_group == 0, s2,
                     jnp.where(lane_group == 1, s4, jnp.where(lane_group == 2, s8, s16)))


def _ffn_rows(x_ref, rows, g_ref, wg_ref, wu_ref, wd_ref, act_ref):
    h = _rms(x_ref[rows, :], g_ref[...]).astype(bf16)
    for lo, hi in zip(FFN_COL_SPLITS[:-1], FFN_COL_SPLITS[1:]):
        cs = slice(lo, hi)
        gate = _dot(h, wg_ref[:, cs])
        up = _dot(h, wu_ref[:, cs])
        act_ref[rows, cs] = (_silu(gate) * up).astype(bf16)
    return x_ref[rows, :] + 0.5 * _dot(act_ref[rows, :], wd_ref[...])


def _sub_tiles():
    return [pl.ds(r, FFN_SUB_TILE) for r in range(0, FFN_TILE, FFN_SUB_TILE)]


class _SideCast:
    def __init__(self, arrays, layer, chunks):
        self.layer, self.chunks = layer, chunks
        self.shapes = [a.shape[1:] for a in arrays]
        assert all(r % (chunks * V7X_BF16_ROWS) == 0 for r, _ in self.shapes)
        self.rows = [r // chunks for r, _ in self.shapes]

    def in_copy(self, a, j, srcs, bufs, sem):
        start = pl.multiple_of(j * self.rows[a], V7X_BF16_ROWS)
        return pltpu.make_async_copy(
            srcs[a].at[self.layer, pl.ds(start, self.rows[a]), :], bufs[a], sem.at[a])

    def out_copy(self, a, j, bufs, dsts, sem):
        start = pl.multiple_of(j * self.rows[a], V7X_BF16_ROWS)
        return pltpu.make_async_copy(
            bufs[a], dsts[a].at[pl.ds(start, self.rows[a]), :], sem.at[a])


def _ffn_kernel(*refs, sample_steps, layer, ple, final_norm, side):
    refs = list(refs)
    take = lambda n: [refs.pop(0) for _ in range(n)]
    xs_ref, xp_ref, g_ref, wg_ref, wu_ref, wd_ref = take(6)
    if ple:
        ps_ref, pp_ref, gp_ref, wpg_ref, wpp_ref, gf_ref = take(6)
        gp_ref = _LayerRow(gp_ref, layer)
    n_side = len(side.shapes) if side else 0
    srcs = take(n_side)
    os_ref, op_ref = take(2)
    dsts = take(n_side)
    (act_ref,) = take(1)
    in_bufs, out_bufs = take(n_side), take(n_side)
    in_sem, out_sem = take(2) if side else (None, None)
    g_ref = _LayerRow(g_ref, layer)

    step = pl.program_id(0)
    is_sample = step < sample_steps
    chunk = step - sample_steps
    has_chunk = jnp.logical_and(chunk >= 0, chunk < side.chunks) if side else None

    if side:
        @pl.when(has_chunk)
        def _():
            for a in range(n_side):
                side.in_copy(a, chunk, srcs, in_bufs, in_sem).start()

    for rows in _sub_tiles():
        op_ref[rows, :] = jnp.where(is_sample, xs_ref[rows, :], xp_ref[rows, :])
        op_ref[rows, :] = _ffn_rows(op_ref, rows, g_ref, wg_ref, wu_ref, wd_ref, act_ref)
        if ple:
            p = jnp.where(is_sample, ps_ref[rows, :], pp_ref[rows, :]).astype(bf16)
            gate = jax.nn.sigmoid(
                _dot(_rms(op_ref[rows, :], gp_ref[...]).astype(bf16), wpg_ref[...]))
            x = op_ref[rows, :] + gate * _dot(p, wpp_ref[...])
            if final_norm:
                x = _rms(x, gf_ref[...])
            op_ref[rows, :] = x

    @pl.when(is_sample)
    def _():
        os_ref[...] = op_ref[...]

    if side:
        @pl.when(jnp.logical_and(chunk >= 1, chunk < side.chunks))
        def _():
            for a in range(n_side):
                side.out_copy(a, chunk - 1, out_bufs, dsts, out_sem).wait()

        @pl.when(has_chunk)
        def _():
            for a in range(n_side):
                side.in_copy(a, chunk, srcs, in_bufs, in_sem).wait()
                out_bufs[a][...] = in_bufs[a][...].astype(bf16)
                side.out_copy(a, chunk, out_bufs, dsts, out_sem).start()

        @pl.when(chunk == side.chunks - 1)
        def _():
            for a in range(n_side):
                side.out_copy(a, chunk, out_bufs, dsts, out_sem).wait()


class _LayerRow:
    def __init__(self, ref, layer):
        self.ref, self.layer = ref, layer

    def __getitem__(self, idx):
        assert idx is Ellipsis
        return self.ref[self.layer:self.layer + 1, :]


def _resident(w, layer=None):
    if layer is None or w.ndim == 2:
        return pl.BlockSpec(w.shape, lambda *_: (0,) * w.ndim, pipeline_mode=pl.Buffered(1))
    return pl.BlockSpec((None,) + w.shape[1:], lambda *_: (layer,) + (0,) * (w.ndim - 1),
                        pipeline_mode=pl.Buffered(1))


def _ffn_call(xs, xp, layer, ffn_w, ple=None, final_norm=False, cast=(), cast_layer=0):
    tile = FFN_TILE
    ns, np_ = xs.shape[0] // tile, xp.shape[0] // tile

    def sample_tile(i):
        return jnp.minimum(i, ns - 1)

    def prompt_tile(i):
        return jnp.maximum(i - ns, 0)

    xs_spec = pl.BlockSpec((tile, D_MODEL), lambda i: (sample_tile(i), 0),
                           pipeline_mode=pl.Buffered(1))
    xp_spec = pl.BlockSpec((tile, D_MODEL), lambda i: (prompt_tile(i), 0))
    in_specs = [xs_spec, xp_spec] + [_resident(w) for w in ffn_w]
    args = [xs, xp, *ffn_w]
    name = "ffn"
    if ple is not None:
        ps, pp, ple_w, gf = ple
        name = "ffn_ple"
        in_specs += ([pl.BlockSpec((None, tile, D_PLE), lambda i: (layer, sample_tile(i), 0),
                                   pipeline_mode=pl.Buffered(1)),
                      pl.BlockSpec((None, tile, D_PLE), lambda i: (layer, prompt_tile(i), 0))]
                     + [_resident(w) for w in ple_w] + [_resident(gf)])
        args += [ps, pp, *ple_w, gf]
    side = _SideCast(cast, cast_layer, np_) if cast else None
    any_spec = pl.BlockSpec(memory_space=pl.ANY)
    out_specs = [xs_spec, xp_spec] + [any_spec] * len(cast)
    out_shape = ([jax.ShapeDtypeStruct(xs.shape, f32), jax.ShapeDtypeStruct(xp.shape, f32)]
                 + [jax.ShapeDtypeStruct(a.shape[1:], bf16) for a in cast])
    scratch = [pltpu.VMEM((tile, D_FF), bf16)]
    if side:
        scratch += [pltpu.VMEM((r, s[1]), f32) for r, s in zip(side.rows, side.shapes)]
        scratch += [pltpu.VMEM((r, s[1]), bf16) for r, s in zip(side.rows, side.shapes)]
        scratch += [pltpu.SemaphoreType.DMA((len(cast),)), pltpu.SemaphoreType.DMA((len(cast),))]
    out = pl.pallas_call(
        functools.partial(_ffn_kernel, sample_steps=ns, layer=layer, ple=ple is not None,
                          final_norm=final_norm, side=side),
        grid=(ns + np_,),
        in_specs=in_specs + [any_spec] * len(cast),
        out_specs=out_specs,
        out_shape=out_shape,
        scratch_shapes=scratch,
        compiler_params=pltpu.CompilerParams(
            dimension_semantics=("arbitrary",), vmem_limit_bytes=V7X_VMEM_LIMIT_BYTES),
        name=name,
    )(*args, *cast)
    return out[0], out[1], list(out[2:])


def _mixer_prompt_kernel(x_ref, gm_ref, wmi_ref, wmo_ref, caw_ref, cab_ref, nag_ref, nab_ref,
                         pw_ref, ps_ref, sng_ref, snb_ref, sw_ref, sb_ref, scw_ref, avg_ref,
                         o_ref, sa_ref, sp_ref, ss_ref,
                         z_all, sh_all, c_all, d_all, y_all, *, layer):
    gm_ref, cab_ref, nag_ref, nab_ref, ps_ref, sng_ref, snb_ref = (
        _LayerRow(r, layer) for r in (gm_ref, cab_ref, nag_ref, nab_ref, ps_ref, sng_ref, snb_ref))
    tm, halo = MIX_TILE, MIX_HALO
    sub, half = V7X_SUBLANES, GROUP_W // 2
    step = pl.program_id(1)
    weights = (gm_ref, wmi_ref, wmo_ref, caw_ref, cab_ref, nag_ref, nab_ref, pw_ref, ps_ref,
               sng_ref, snb_ref, sw_ref, sb_ref, scw_ref, avg_ref)

    @pl.when(step == 0)
    def _():
        z_all[:, 0:halo, :] = jnp.zeros((MIX_SEQS, halo, MIX_IN_W), f32)
        sh_all[:, :, 0:halo, :] = jnp.zeros((MIX_SEQS, sub - 1, halo, 2 * GROUP_W), f32)

    for s in range(MIX_SEQS):
        h = _rms(x_ref[s], gm_ref[...]).astype(bf16)
        z_all[s, halo:halo + tm, :] = _dot(h, wmi_ref[...])
    for s in range(MIX_SEQS):
        _mixer_prompt_tile(step, x_ref.at[s], o_ref.at[s], weights, z_all.at[s], sh_all.at[s],
                           c_all.at[s], d_all.at[s], y_all.at[s])

    @pl.when(step == pl.num_programs(1) - 1)
    def _():
        for s in range(MIX_SEQS):
            z_scr = z_all.at[s]
            sa_ref[s] = z_scr[pl.ds(halo + tm - (CONV_A_WIDTH - 1), CONV_A_WIDTH - 1),
                              COL_A_VAL:COL_A_VAL + GROUP_W]
            sp_ref[s] = z_scr[pl.ds(halo + tm - POOL_STATE, POOL_STATE),
                              COL_ZB:COL_ZB + GROUP_W]
            ss_ref[s] = z_scr[pl.ds(halo + tm - (SHORT_CONV_WIDTH - 1), SHORT_CONV_WIDTH - 1),
                              COL_D_IN:COL_D_IN + GROUP_W]

    for s in range(MIX_SEQS):
        z_scr, sh_scr = z_all.at[s], sh_all.at[s]
        for col in (COL_A_VAL, COL_ZB, COL_D_IN):
            z_scr[0:halo, col:col + GROUP_W] = z_scr[tm:tm + halo, col:col + GROUP_W]
        sh_scr[0:3, 0:halo, :] = sh_scr[0:3, tm:tm + halo, :]
        for lanes in (slice(0, GROUP_W), slice(GROUP_W + half, 2 * GROUP_W)):
            sh_scr[3:sub - 1, 0:halo, lanes] = sh_scr[3:sub - 1, tm:tm + halo, lanes]


def _mixer_prompt_tile(step, x_ref, o_ref, weights, z_scr, sh_scr, c_scr, d_scr, y_scr):
    (gm_ref, wmi_ref, wmo_ref, caw_ref, cab_ref, nag_ref, nab_ref, pw_ref, ps_ref,
     sng_ref, snb_ref, sw_ref, sb_ref, scw_ref, avg_ref) = weights
    tm, halo, rb = MIX_TILE, MIX_HALO, MIX_ROWS
    sub, half = V7X_SUBLANES, GROUP_W // 2

    for r in range(0, tm, rb):
        rows = pl.ds(halo + r, rb)
        a_val = z_scr[rows, COL_A_VAL:COL_A_VAL + GROUP_W]
        a_gate = z_scr[rows, COL_A_GATE:COL_A_GATE + GROUP_W]
        z_scr[rows, COL_A_VAL:COL_A_VAL + GROUP_W] = a_val * jax.nn.sigmoid(a_gate)
        c_gate = z_scr[rows, COL_C_GATE:COL_C_GATE + GROUP_W]
        d_in = z_scr[rows, COL_D_IN:COL_D_IN + GROUP_W]
        z_scr[rows, COL_D_IN:COL_D_IN + GROUP_W] = c_gate * d_in
        for b in range(1, sub):
            late = pl.ds(halo + r - b, rb)
            sh_scr[b - 1, rows, 0:GROUP_W] = z_scr[late, COL_A_VAL:COL_A_VAL + GROUP_W]
            if b < 4:
                sh_scr[b - 1, rows, GROUP_W:2 * GROUP_W] = z_scr[late, COL_ZB:COL_ZB + GROUP_W]
            else:
                sh_scr[b - 1, rows, GROUP_W + half:2 * GROUP_W] = (
                    z_scr[late, COL_ZB + half:COL_ZB + GROUP_W])

    lane_group = lax.broadcasted_iota(jnp.int32, (rb, GROUP_W), 1) // HEAD_DIM
    win = jnp.left_shift(2, lane_group)
    row_iota = lax.broadcasted_iota(jnp.int32, (rb, GROUP_W), 0)
    first_head = lax.broadcasted_iota(jnp.int32, (rb, half), 1) < HEAD_DIM

    for r in range(0, tm, rb):
        def delayed(col, lanes, b, back):
            at = pl.ds(halo + r - back, rb)
            if b == 0:
                return z_scr[at, col + lanes.start:col + lanes.stop]
            base = 0 if col == COL_A_VAL else GROUP_W
            return sh_scr[b - 1, at, base + lanes.start:base + lanes.stop]

        acc = jnp.broadcast_to(cab_ref[...], (rb, GROUP_W))
        for j in range(CONV_A_WIDTH):
            k = CONV_A_WIDTH - 1 - j
            tap = delayed(COL_A_VAL, slice(0, GROUP_W), j % sub, j - j % sub)
            acc = acc + caw_ref[k:k + 1, :] * tap
        c_scr[r:r + rb, :] = acc

        lo, hi = slice(0, half), slice(half, GROUP_W)
        s2 = delayed(COL_ZB, lo, 0, 0) + delayed(COL_ZB, lo, 1, 0)
        s4 = s2 + delayed(COL_ZB, lo, 2, 0) + delayed(COL_ZB, lo, 3, 0)
        s8 = delayed(COL_ZB, hi, 0, 0)
        for b in range(1, sub):
            s8 = s8 + delayed(COL_ZB, hi, b, 0)
        s16 = s8
        for b in range(sub):
            s16 = s16 + delayed(COL_ZB, hi, b, sub)
        sums = jnp.concatenate(
            [jnp.where(first_head, s2, s4), jnp.where(first_head, s8, s16)], axis=1)
        zb = z_scr[pl.ds(halo + r, rb), COL_ZB:COL_ZB + GROUP_W]
        pos = step * tm + r + row_iota
        cnt = jnp.minimum(pos + 1, win).astype(f32)
        d_scr[r:r + rb, :] = (sums / cnt - zb).astype(bf16)

        rows = pl.ds(halo + r, rb)
        q0 = z_scr[pl.ds(halo + r - 2, rb), COL_D_IN:COL_D_IN + GROUP_W]
        q1 = z_scr[pl.ds(halo + r - 1, rb), COL_D_IN:COL_D_IN + GROUP_W]
        q2 = z_scr[rows, COL_D_IN:COL_D_IN + GROUP_W]
        conv = scw_ref[0:1, :] * q0 + scw_ref[1:2, :] * q1 + scw_ref[2:3, :] * q2
        b_gate = z_scr[rows, COL_B_GATE:COL_B_GATE + GROUP_W]
        y_scr[r:r + rb, 3 * GROUP_W:4 * GROUP_W] = (b_gate * conv).astype(bf16)

    avg = avg_ref[...]
    ya = _head_layer_norm(c_scr[...], nag_ref[...], nab_ref[...], avg)
    y_scr[:, 0:GROUP_W] = _silu(ya).astype(bf16)
    y_scr[:, GROUP_W:2 * GROUP_W] = (_dot(d_scr[...], pw_ref[...]) * ps_ref[...]).astype(bf16)

    t_idx = lax.broadcasted_iota(jnp.int32, (CHUNK, N_HEADS * CHUNK), 0)
    s_idx = lax.broadcasted_iota(jnp.int32, (CHUNK, N_HEADS * CHUNK), 1) % CHUNK
    w_cat = jnp.where(s_idx <= t_idx, sw_ref[...], 0.0).astype(bf16)
    head_of_lane = lax.broadcasted_iota(jnp.int32, (CHUNK, GROUP_W), 1) // HEAD_DIM
    for c in range(tm // CHUNK):
        rows = pl.ds(halo + c * CHUNK, CHUNK)
        u = jax.nn.gelu(z_scr[rows, COL_U:COL_U + GROUP_W])
        v = jax.nn.gelu(z_scr[rows, COL_V:COL_V + GROUP_W])
        v = _head_layer_norm(v, sng_ref[...], snb_ref[...], avg)
        v_stack = jnp.concatenate(
            [jnp.where(head_of_lane == hd, v, 0.0).astype(bf16) for hd in range(N_HEADS)], axis=0)
        s = _dot(w_cat, v_stack) + sb_ref[...]
        y_scr[c * CHUNK:(c + 1) * CHUNK, 2 * GROUP_W:3 * GROUP_W] = (u * s).astype(bf16)

    o_ref[...] = x_ref[...] + _dot(y_scr[...], wmo_ref[...])


def _mixer_prompt_call(x, batch, seq, layer, wts, avg):
    tm, ns = MIX_TILE, MIX_SEQS
    x = x.reshape(batch, seq, D_MODEL)
    row_spec = pl.BlockSpec((ns, tm, D_MODEL), lambda b, l: (b, l, 0))

    def state_spec(rows):
        return pl.BlockSpec((ns, rows, GROUP_W), lambda b, l: (b, 0, 0))

    in_specs = [row_spec] + [_resident(w, layer) for w in wts] + [_resident(avg)]
    out = pl.pallas_call(
        functools.partial(_mixer_prompt_kernel, layer=layer),
        grid=(batch // ns, seq // tm),
        in_specs=in_specs,
        out_specs=[row_spec, state_spec(CONV_A_WIDTH - 1), state_spec(POOL_STATE),
                   state_spec(SHORT_CONV_WIDTH - 1)],
        out_shape=[jax.ShapeDtypeStruct((batch, seq, D_MODEL), f32),
                   jax.ShapeDtypeStruct((batch, CONV_A_WIDTH - 1, GROUP_W), f32),
                   jax.ShapeDtypeStruct((batch, POOL_STATE, GROUP_W), f32),
                   jax.ShapeDtypeStruct((batch, SHORT_CONV_WIDTH - 1, GROUP_W), f32)],
        scratch_shapes=[pltpu.VMEM((ns, MIX_HALO + tm, MIX_IN_W), f32),
                        pltpu.VMEM((ns, V7X_SUBLANES - 1, MIX_HALO + tm, 2 * GROUP_W), f32),
                        pltpu.VMEM((ns, tm, GROUP_W), f32),
                        pltpu.VMEM((ns, tm, GROUP_W), bf16),
                        pltpu.VMEM((ns, tm, D_MODEL), bf16)],
        compiler_params=pltpu.CompilerParams(
            dimension_semantics=("arbitrary", "arbitrary"),
            vmem_limit_bytes=V7X_VMEM_LIMIT_BYTES),
        name="mixer_prompt",
    )(x, *wts, avg)
    return (out[0].reshape(batch * seq, D_MODEL),) + tuple(out[1:])


def _mixer_sample_kernel(x_ref, gm_ref, wmi_ref, wmo_ref, caw_ref, cab_ref, nag_ref, nab_ref,
                         pw_ref, ps_ref, sng_ref, snb_ref, coef_ref, sb_ref, scw_ref, avg_ref,
                         sta_ref, stp_ref, sts_ref,
                         o_ref, sa_ref, sp_ref, ss_ref, v_ref,
                         z_scr, xa, xp, xs, c_scr, y_scr, *, dec_seq, layer):
    gm_ref, cab_ref, nag_ref, nab_ref, ps_ref, sng_ref, snb_ref = (
        _LayerRow(r, layer) for r in (gm_ref, cab_ref, nag_ref, nab_ref, ps_ref, sng_ref, snb_ref))
    nb, steps = SAMPLE_SEQS, dec_seq
    rows = nb * steps
    hist_a, hist_p, hist_s = CONV_A_WIDTH - 1, POOL_STATE, SHORT_CONV_WIDTH - 1
    avg = avg_ref[...]

    def slabs(col):
        return z_scr[:, col:col + GROUP_W].reshape(steps, nb, GROUP_W)

    x = x_ref[...].reshape(rows, D_MODEL)
    z_scr[...] = _dot(_rms(x, gm_ref[...]).astype(bf16), wmi_ref[...])

    xa[0:hist_a] = sta_ref[...]
    xa[hist_a:hist_a + steps] = slabs(COL_A_VAL) * jax.nn.sigmoid(slabs(COL_A_GATE))
    sa_ref[...] = xa[steps:steps + hist_a]
    for t in range(steps):
        acc = jnp.broadcast_to(cab_ref[...], (nb, GROUP_W))
        for k in range(CONV_A_WIDTH):
            acc = acc + caw_ref[k:k + 1, :] * xa[t + k]
        c_scr[t] = acc
    ya = _head_layer_norm(c_scr[...].reshape(rows, GROUP_W), nag_ref[...], nab_ref[...], avg)
    y_scr[:, 0:GROUP_W] = _silu(ya).astype(bf16)

    xp[0:hist_p] = stp_ref[...]
    xp[hist_p:hist_p + steps] = slabs(COL_ZB)
    sp_ref[...] = xp[steps:steps + hist_p]
    lane_group = lax.broadcasted_iota(jnp.int32, (nb, GROUP_W), 1) // HEAD_DIM
    win = jnp.left_shift(2, lane_group)
    for t in range(steps):
        zb = xp[hist_p + t]
        run = zb
        sums = []
        for j in range(1, max(POOL_WINDOWS)):
            run = run + xp[hist_p + t - j]
            if j + 1 in POOL_WINDOWS:
                sums.append(run)
        cnt = jnp.minimum(PAST_LEN + t + 1, win).astype(f32)
        c_scr[t] = _pool_select(sums, lane_group) / cnt - zb
    d = c_scr[...].reshape(rows, GROUP_W).astype(bf16)
    y_scr[:, GROUP_W:2 * GROUP_W] = (_dot(d, pw_ref[...]) * ps_ref[...]).astype(bf16)

    v = _head_layer_norm(jax.nn.gelu(z_scr[:, COL_V:COL_V + GROUP_W]),
                         sng_ref[...], snb_ref[...], avg)
    v_ref[...] = v.reshape(steps, nb, GROUP_W)
    for t in range(steps):
        s = jnp.broadcast_to(sb_ref[t:t + 1, :], (nb, GROUP_W))
        for j in range(t + 1):
            s = s + coef_ref[j, t:t + 1, :] * v_ref[j]
        u = jax.nn.gelu(z_scr[t * nb:(t + 1) * nb, COL_U:COL_U + GROUP_W])
        y_scr[t * nb:(t + 1) * nb, 2 * GROUP_W:3 * GROUP_W] = (u * s).astype(bf16)

    xs[0:hist_s] = sts_ref[...]
    xs[hist_s:hist_s + steps] = slabs(COL_C_GATE) * slabs(COL_D_IN)
    ss_ref[...] = xs[steps:steps + hist_s]
    for t in range(steps):
        conv = scw_ref[0:1, :] * xs[t]
        for k in range(1, SHORT_CONV_WIDTH):
            conv = conv + scw_ref[k:k + 1, :] * xs[t + k]
        b_gate = z_scr[t * nb:(t + 1) * nb, COL_B_GATE:COL_B_GATE + GROUP_W]
        y_scr[t * nb:(t + 1) * nb, 3 * GROUP_W:4 * GROUP_W] = (b_gate * conv).astype(bf16)

    out = x_ref[...].reshape(rows, D_MODEL) + _dot(y_scr[...], wmo_ref[...])
    o_ref[...] = out.reshape(steps, nb, D_MODEL)


def _mixer_sample_call(x, st_a, st_p, st_s, dec_batch, dec_seq, layer, wts, avg):
    nb = SAMPLE_SEQS
    rows = nb * dec_seq
    row_spec = pl.BlockSpec((dec_seq, nb, D_MODEL), lambda i: (0, i, 0))

    def state_spec(r):
        return pl.BlockSpec((r, nb, GROUP_W), lambda i: (0, i, 0))

    def state_in_spec(r):
        return pl.BlockSpec((None, r, nb, GROUP_W), lambda i: (layer, 0, i, 0))

    def state_shape(r):
        return jax.ShapeDtypeStruct((r, dec_batch, GROUP_W), f32)

    hist_a, hist_p, hist_s = CONV_A_WIDTH - 1, POOL_STATE, SHORT_CONV_WIDTH - 1
    in_specs = ([row_spec] + [_resident(w, layer) for w in wts] + [_resident(avg)]
                + [state_in_spec(hist_a), state_in_spec(hist_p), state_in_spec(hist_s)])
    return pl.pallas_call(
        functools.partial(_mixer_sample_kernel, dec_seq=dec_seq, layer=layer),
        grid=(dec_batch // nb,),
        in_specs=in_specs,
        out_specs=[row_spec, state_spec(hist_a), state_spec(hist_p), state_spec(hist_s),
                   state_spec(dec_seq)],
        out_shape=[jax.ShapeDtypeStruct((dec_seq, dec_batch, D_MODEL), f32),
                   state_shape(hist_a), state_shape(hist_p), state_shape(hist_s),
                   state_shape(dec_seq)],
        scratch_shapes=[pltpu.VMEM((rows, MIX_IN_W), f32),
                        pltpu.VMEM((hist_a + dec_seq, nb, GROUP_W), f32),
                        pltpu.VMEM((hist_p + dec_seq, nb, GROUP_W), f32),
                        pltpu.VMEM((hist_s + dec_seq, nb, GROUP_W), f32),
                        pltpu.VMEM((dec_seq, nb, GROUP_W), f32),
                        pltpu.VMEM((rows, D_MODEL), bf16)],
        compiler_params=pltpu.CompilerParams(
            dimension_semantics=("arbitrary",), vmem_limit_bytes=V7X_VMEM_LIMIT_BYTES),
        name="mixer_sample",
    )(x, *wts, avg, st_a, st_p, st_s)


def _block_diag(w):
    n, g, c, d = w.shape
    eye = jnp.eye(g, dtype=w.dtype)
    return (eye[None, :, None, :, None] * w[:, :, :, None, :]).reshape(n, g * c, g * d)


def kernel(x_prompt, x_sample, p_prompt, p_sample, state_conv_a, state_pool, state_short_conv,
           norm_ffn1, w_ffn1_gate, w_ffn1_up, w_ffn1_down, norm_mix, w_mix_in,
           conv_a_w, conv_a_b, norm_a_g, norm_a_b, pool_w, pool_scale,
           sgu_norm_g, sgu_norm_b, sgu_w, sgu_b, short_conv_w, w_mix_out,
           norm_ffn2, w_ffn2_gate, w_ffn2_up, w_ffn2_down,
           norm_ple, w_ple_gate, w_ple_proj, norm_final):
    depth = w_mix_in.shape[0]
    batch, seq, _ = x_prompt.shape
    dec_batch, dec_seq, _ = x_sample.shape
    assert seq % MIX_TILE == 0 and MIX_TILE % CHUNK == 0 and dec_seq <= CHUNK
    assert batch % MIX_SEQS == 0
    assert dec_batch % SAMPLE_SEQS == 0 and dec_seq == V7X_SUBLANES
    assert (batch * seq) % FFN_TILE == 0 and (dec_batch * dec_seq) % FFN_TILE == 0
    assert all(c % 256 == 0 for c in FFN_COL_SPLITS)

    avg = _block_diag(jnp.full((1, N_HEADS, HEAD_DIM, HEAD_DIM), 1.0 / HEAD_DIM, f32))[0]
    avg = avg.astype(bf16)

    ffn1_mats = [w[0].astype(bf16) for w in (w_ffn1_gate, w_ffn1_up, w_ffn1_down)]
    cast_in_ffn1 = (w_ffn2_gate, w_ffn2_up, w_ffn2_down, w_ple_gate, w_ple_proj,
                    w_mix_in, w_mix_out)
    cast_in_ffn2 = (w_ffn1_gate, w_ffn1_up, w_ffn1_down)
    gf = norm_final.reshape(1, D_MODEL)
    mix_tail = (conv_a_w, conv_a_b, norm_a_g, norm_a_b,
                _block_diag(pool_w).astype(bf16), pool_scale,
                sgu_norm_g, sgu_norm_b)
    sw_prompt = jnp.transpose(sgu_w, (0, 2, 1, 3)).reshape(depth, CHUNK, N_HEADS * CHUNK)
    sb_prompt = jnp.repeat(jnp.swapaxes(sgu_b, 1, 2), HEAD_DIM, axis=2)
    sw_sample = jnp.repeat(jnp.transpose(sgu_w[:, :, :dec_seq, :dec_seq], (0, 3, 2, 1)),
                           HEAD_DIM, axis=3)
    sb_sample = sb_prompt[:, :dec_seq]

    sample_rows = dec_seq * dec_batch
    pp = p_prompt.reshape(depth, batch * seq, D_PLE)
    ps = jnp.swapaxes(p_sample, 1, 2).reshape(depth, sample_rows, D_PLE)
    xp = x_prompt.reshape(batch * seq, D_MODEL)
    xs = jnp.swapaxes(x_sample, 0, 1).reshape(sample_rows, D_MODEL)
    st_a, st_p, st_s = (jnp.swapaxes(st, 1, 2)
                        for st in (state_conv_a, state_pool, state_short_conv))
    prompt_states, sample_states = [], []
    for i in range(depth):
        last = i == depth - 1

        xs, xp, mats = _ffn_call(xs, xp, i, (norm_ffn1, *ffn1_mats),
                                 cast=cast_in_ffn1, cast_layer=i)
        ffn2_mats, ple_mats, mix_mats = mats[0:3], mats[3:5], mats[5:7]
        mix_head = (norm_mix, *mix_mats) + mix_tail
        xp, sa, sp, ss = _mixer_prompt_call(
            xp, batch, seq, i, mix_head + (sw_prompt, sb_prompt, short_conv_w), avg)
        prompt_states.append((sa, sp, ss))
        xs, sa, sp, ss, sv = _mixer_sample_call(
            xs.reshape(dec_seq, dec_batch, D_MODEL), st_a, st_p, st_s, dec_batch, dec_seq, i,
            mix_head + (sw_sample, sb_sample, short_conv_w), avg)
        xs = xs.reshape(sample_rows, D_MODEL)
        sample_states.append((sa, sp, ss, sv))
        xs, xp, ffn1_mats = _ffn_call(
            xs, xp, i, (norm_ffn2, *ffn2_mats), ple=(ps, pp, (norm_ple, *ple_mats), gf),
            final_norm=last, cast=() if last else cast_in_ffn2, cast_layer=i + 1)

    def stack(states, j):
        return jnp.stack([s[j] for s in states])

    def stack_sample(j):
        return jnp.swapaxes(stack(sample_states, j), 1, 2)

    y_sample = jnp.swapaxes(xs.reshape(dec_seq, dec_batch, D_MODEL), 0, 1)
    return (xp.reshape(batch, seq, D_MODEL), y_sample,
            stack(prompt_states, 0), stack_sample(0),
            stack(prompt_states, 1), stack_sample(1),
            stack(prompt_states, 2), stack_sample(2),
            stack_sample(3))
```

```python
import functools

import jax
import jax.numpy as jnp
from jax import lax
from jax.experimental import pallas as pl
from jax.experimental.pallas import tpu as pltpu

D_MODEL = 1024
D_PLE = 256
D_FF = 2816
GROUP_W = 256
N_HEADS = 4
HEAD_DIM = 64
MIX_IN_W = 8 * GROUP_W
CONV_A_WIDTH = 31
POOL_WINDOWS = (2, 4, 8, 16)
POOL_STATE = 15
CHUNK = 128
SHORT_CONV_WIDTH = 3
EPS = 1e-6
PAST_LEN = 16384

V7X_SUBLANES = 8
V7X_BF16_ROWS = 16
V7X_VMEM_BYTES = 64 * 1024 * 1024
V7X_VMEM_LIMIT_BYTES = V7X_VMEM_BYTES - 4 * 1024 * 1024

FFN_TILE = 1024
FFN_SUB_TILE = 512
FFN_COL_SPLITS = (0, 1536, D_FF)
MIX_TILE = 512
MIX_SEQS = 2
MIX_HALO = 32
MIX_ROWS = 128
SAMPLE_SEQS = 64

COL_A_VAL, COL_A_GATE, COL_ZB, COL_U, COL_V, COL_B_GATE, COL_C_GATE, COL_D_IN = (
    i * GROUP_W for i in range(8))

bf16 = jnp.bfloat16
f32 = jnp.float32


def _dot(a, b):
    return jnp.dot(a, b, preferred_element_type=f32)


def _rms(x, g):
    ms = jnp.mean(x * x, axis=-1, keepdims=True)
    return x * lax.rsqrt(ms + EPS) * g


def _group_mean(x, avg):
    hi = x.astype(bf16)
    lo = (x - hi.astype(f32)).astype(bf16)
    return _dot(hi, avg) + _dot(lo, avg)


def _head_layer_norm(x, g, b, avg):
    mu = _group_mean(x, avg)
    xc = x - mu
    var = _dot((xc * xc).astype(bf16), avg)
    return xc * lax.rsqrt(var + EPS) * g + b


def _silu(x):
    return x * jax.nn.sigmoid(x)


def _pool_select(sums, lane_group):
    s2, s4, s8, s16 = sums
    return jnp.where(lane_group == 0, s2,
                     jnp.where(lane_group == 1, s4, jnp.where(lane_group == 2, s8, s16)))


def _ffn_rows(x_ref, rows, g_ref, wg_ref, wu_ref, wd_ref, act_ref):
    h = _rms(x_ref[rows, :], g_ref[...]).astype(bf16)
    for lo, hi in zip(FFN_COL_SPLITS[:-1], FFN_COL_SPLITS[1:]):
        cs = slice(lo, hi)
        gate = _dot(h, wg_ref[:, cs])
        up = _dot(h, wu_ref[:, cs])
        act_ref[rows, cs] = (_silu(gate) * up).astype(bf16)
    return x_ref[rows, :] + 0.5 * _dot(act_ref[rows, :], wd_ref[...])


def _sub_tiles():
    return [pl.ds(r, FFN_SUB_TILE) for r in range(0, FFN_TILE, FFN_SUB_TILE)]


class _SideCast:
    def __init__(self, arrays, layer, chunks):
        self.layer, self.chunks = layer, chunks
        self.shapes = [a.shape[1:] for a in arrays]
        assert all(r % (chunks * V7X_BF16_ROWS) == 0 for r, _ in self.shapes)
        self.rows = [r // chunks for r, _ in self.shapes]

    def in_copy(self, a, j, srcs, bufs, sem):
        start = pl.multiple_of(j * self.rows[a], V7X_BF16_ROWS)
        return pltpu.make_async_copy(
            srcs[a].at[self.layer, pl.ds(start, self.rows[a]), :], bufs[a], sem.at[a])

    def out_copy(self, a, j, bufs, dsts, sem):
        start = pl.multiple_of(j * self.rows[a], V7X_BF16_ROWS)
        return pltpu.make_async_copy(
            bufs[a], dsts[a].at[pl.ds(start, self.rows[a]), :], sem.at[a])


def _ffn_kernel(*refs, sample_steps, layer, ple, final_norm, side):
    refs = list(refs)
    take = lambda n: [refs.pop(0) for _ in range(n)]
    xs_ref, xp_ref, g_ref, wg_ref, wu_ref, wd_ref = take(6)
    if ple:
        ps_ref, pp_ref, gp_ref, wpg_ref, wpp_ref, gf_ref = take(6)
        gp_ref = _LayerRow(gp_ref, layer)
    n_side = len(side.shapes) if side else 0
    srcs = take(n_side)
    os_ref, op_ref = take(2)
    dsts = take(n_side)
    (act_ref,) = take(1)
    in_bufs, out_bufs = take(n_side), take(n_side)
    in_sem, out_sem = take(2) if side else (None, None)
    g_ref = _LayerRow(g_ref, layer)

    step = pl.program_id(0)
    is_sample = step < sample_steps
    chunk = step - sample_steps
    has_chunk = jnp.logical_and(chunk >= 0, chunk < side.chunks) if side else None

    if side:
        @pl.when(has_chunk)
        def _():
            for a in range(n_side):
                side.in_copy(a, chunk, srcs, in_bufs, in_sem).start()

    for rows in _sub_tiles():
        op_ref[rows, :] = jnp.where(is_sample, xs_ref[rows, :], xp_ref[rows, :])
        op_ref[rows, :] = _ffn_rows(op_ref, rows, g_ref, wg_ref, wu_ref, wd_ref, act_ref)
        if ple:
            p = jnp.where(is_sample, ps_ref[rows, :], pp_ref[rows, :]).astype(bf16)
            gate = jax.nn.sigmoid(
                _dot(_rms(op_ref[rows, :], gp_ref[...]).astype(bf16), wpg_ref[...]))
            x = op_ref[rows, :] + gate * _dot(p, wpp_ref[...])
            if final_norm:
                x = _rms(x, gf_ref[...])
            op_ref[rows, :] = x

    @pl.when(is_sample)
    def _():
        os_ref[...] = op_ref[...]

    if side:
        @pl.when(jnp.logical_and(chunk >= 1, chunk < side.chunks))
        def _():
            for a in range(n_side):
                side.out_copy(a, chunk - 1, out_bufs, dsts, out_sem).wait()

        @pl.when(has_chunk)
        def _():
            for a in range(n_side):
                side.in_copy(a, chunk, srcs, in_bufs, in_sem).wait()
                out_bufs[a][...] = in_bufs[a][...].astype(bf16)
                side.out_copy(a, chunk, out_bufs, dsts, out_sem).start()

        @pl.when(chunk == side.chunks - 1)
        def _():
            for a in range(n_side):
                side.out_copy(a, chunk, out_bufs, dsts, out_sem).wait()


class _LayerRow:
    def __init__(self, ref, layer):
        self.ref, self.layer = ref, layer

    def __getitem__(self, idx):
        assert idx is Ellipsis
        return self.ref[self.layer:self.layer + 1, :]


def _resident(w, layer=None):
    if layer is None or w.ndim == 2:
        return pl.BlockSpec(w.shape, lambda *_: (0,) * w.ndim, pipeline_mode=pl.Buffered(1))
    return pl.BlockSpec((None,) + w.shape[1:], lambda *_: (layer,) + (0,) * (w.ndim - 1),
                        pipeline_mode=pl.Buffered(1))


def _ffn_call(xs, xp, layer, ffn_w, ple=None, final_norm=False, cast=(), cast_layer=0):
    tile = FFN_TILE
    ns, np_ = xs.shape[0] // tile, xp.shape[0] // tile

    def sample_tile(i):
        return jnp.minimum(i, ns - 1)

    def prompt_tile(i):
        return jnp.maximum(i - ns, 0)

    xs_spec = pl.BlockSpec((tile, D_MODEL), lambda i: (sample_tile(i), 0),
                           pipeline_mode=pl.Buffered(1))
    xp_spec = pl.BlockSpec((tile, D_MODEL), lambda i: (prompt_tile(i), 0))
    in_specs = [xs_spec, xp_spec] + [_resident(w) for w in ffn_w]
    args = [xs, xp, *ffn_w]
    name = "ffn"
    if ple is not None:
        ps, pp, ple_w, gf = ple
        name = "ffn_ple"
        in_specs += ([pl.BlockSpec((None, tile, D_PLE), lambda i: (layer, sample_tile(i), 0),
                                   pipeline_mode=pl.Buffered(1)),
                      pl.BlockSpec((None, tile, D_PLE), lambda i: (layer, prompt_tile(i), 0))]
                     + [_resident(w) for w in ple_w] + [_resident(gf)])
        args += [ps, pp, *ple_w, gf]
    side = _SideCast(cast, cast_layer, np_) if cast else None
    any_spec = pl.BlockSpec(memory_space=pl.ANY)
    out_specs = [xs_spec, xp_spec] + [any_spec] * len(cast)
    out_shape = ([jax.ShapeDtypeStruct(xs.shape, f32), jax.ShapeDtypeStruct(xp.shape, f32)]
                 + [jax.ShapeDtypeStruct(a.shape[1:], bf16) for a in cast])
    scratch = [pltpu.VMEM((tile, D_FF), bf16)]
    if side:
        scratch += [pltpu.VMEM((r, s[1]), f32) for r, s in zip(side.rows, side.shapes)]
        scratch += [pltpu.VMEM((r, s[1]), bf16) for r, s in zip(side.rows, side.shapes)]
        scratch += [pltpu.SemaphoreType.DMA((len(cast),)), pltpu.SemaphoreType.DMA((len(cast),))]
    out = pl.pallas_call(
        functools.partial(_ffn_kernel, sample_steps=ns, layer=layer, ple=ple is not None,
                          final_norm=final_norm, side=side),
        grid=(ns + np_,),
        in_specs=in_specs + [any_spec] * len(cast),
        out_specs=out_specs,
        out_shape=out_shape,
        scratch_shapes=scratch,
        compiler_params=pltpu.CompilerParams(
            dimension_semantics=("arbitrary",), vmem_limit_bytes=V7X_VMEM_LIMIT_BYTES),
        name=name,
    )(*args, *cast)
    return out[0], out[1], list(out[2:])


def _mixer_prompt_kernel(x_ref, gm_ref, wmi_ref, wmo_ref, caw_ref, cab_ref, nag_ref, nab_ref,
                         pw_ref, ps_ref, sng_ref, snb_ref, sw_ref, sb_ref, scw_ref, avg_ref,
                         o_ref, sa_ref, sp_ref, ss_ref,
                         z_all, sh_all, c_all, d_all, y_all, *, layer):
    gm_ref, cab_ref, nag_ref, nab_ref, ps_ref, sng_ref, snb_ref = (
        _LayerRow(r, layer) for r in (gm_ref, cab_ref, nag_ref, nab_ref, ps_ref, sng_ref, snb_ref))
    tm, halo = MIX_TILE, MIX_HALO
    sub, half = V7X_SUBLANES, GROUP_W // 2
    step = pl.program_id(1)
    weights = (gm_ref, wmi_ref, wmo_ref, caw_ref, cab_ref, nag_ref, nab_ref, pw_ref, ps_ref,
               sng_ref, snb_ref, sw_ref, sb_ref, scw_ref, avg_ref)

    @pl.when(step == 0)
    def _():
        z_all[:, 0:halo, :] = jnp.zeros((MIX_SEQS, halo, MIX_IN_W), f32)
        sh_all[:, :, 0:halo, :] = jnp.zeros((MIX_SEQS, sub - 1, halo, 2 * GROUP_W), f32)

    for s in range(MIX_SEQS):
        h = _rms(x_ref[s], gm_ref[...]).astype(bf16)
        z_all[s, halo:halo + tm, :] = _dot(h, wmi_ref[...])
    for s in range(MIX_SEQS):
        _mixer_prompt_tile(step, x_ref.at[s], o_ref.at[s], weights, z_all.at[s], sh_all.at[s],
                           c_all.at[s], d_all.at[s], y_all.at[s])

    @pl.when(step == pl.num_programs(1) - 1)
    def _():
        for s in range(MIX_SEQS):
            z_scr = z_all.at[s]
            sa_ref[s] = z_scr[pl.ds(halo + tm - (CONV_A_WIDTH - 1), CONV_A_WIDTH - 1),
                              COL_A_VAL:COL_A_VAL + GROUP_W]
            sp_ref[s] = z_scr[pl.ds(halo + tm - POOL_STATE, POOL_STATE),
                              COL_ZB:COL_ZB + GROUP_W]
            ss_ref[s] = z_scr[pl.ds(halo + tm - (SHORT_CONV_WIDTH - 1), SHORT_CONV_WIDTH - 1),
                              COL_D_IN:COL_D_IN + GROUP_W]

    for s in range(MIX_SEQS):
        z_scr, sh_scr = z_all.at[s], sh_all.at[s]
        for col in (COL_A_VAL, COL_ZB, COL_D_IN):
            z_scr[0:halo, col:col + GROUP_W] = z_scr[tm:tm + halo, col:col + GROUP_W]
        sh_scr[0:3, 0:halo, :] = sh_scr[0:3, tm:tm + halo, :]
        for lanes in (slice(0, GROUP_W), slice(GROUP_W + half, 2 * GROUP_W)):
            sh_scr[3:sub - 1, 0:halo, lanes] = sh_scr[3:sub - 1, tm:tm + halo, lanes]


def _mixer_prompt_tile(step, x_ref, o_ref, weights, z_scr, sh_scr, c_scr, d_scr, y_scr):
    (gm_ref, wmi_ref, wmo_ref, caw_ref, cab_ref, nag_ref, nab_ref, pw_ref, ps_ref,
     sng_ref, snb_ref, sw_ref, sb_ref, scw_ref, avg_ref) = weights
    tm, halo, rb = MIX_TILE, MIX_HALO, MIX_ROWS
    sub, half = V7X_SUBLANES, GROUP_W // 2

    for r in range(0, tm, rb):
        rows = pl.ds(halo + r, rb)
        a_val = z_scr[rows, COL_A_VAL:COL_A_VAL + GROUP_W]
        a_gate = z_scr[rows, COL_A_GATE:COL_A_GATE + GROUP_W]
        z_scr[rows, COL_A_VAL:COL_A_VAL + GROUP_W] = a_val * jax.nn.sigmoid(a_gate)
        c_gate = z_scr[rows, COL_C_GATE:COL_C_GATE + GROUP_W]
        d_in = z_scr[rows, COL_D_IN:COL_D_IN + GROUP_W]
        z_scr[rows, COL_D_IN:COL_D_IN + GROUP_W] = c_gate * d_in
        for b in range(1, sub):
            late = pl.ds(halo + r - b, rb)
            sh_scr[b - 1, rows, 0:GROUP_W] = z_scr[late, COL_A_VAL:COL_A_VAL + GROUP_W]
            if b < 4:
                sh_scr[b - 1, rows, GROUP_W:2 * GROUP_W] = z_scr[late, COL_ZB:COL_ZB + GROUP_W]
            else:
                sh_scr[b - 1, rows, GROUP_W + half:2 * GROUP_W] = (
                    z_scr[late, COL_ZB + half:COL_ZB + GROUP_W])

    lane_group = lax.broadcasted_iota(jnp.int32, (rb, GROUP_W), 1) // HEAD_DIM
    win = jnp.left_shift(2, lane_group)
    row_iota = lax.broadcasted_iota(jnp.int32, (rb, GROUP_W), 0)
    first_head = lax.broadcasted_iota(jnp.int32, (rb, half), 1) < HEAD_DIM

    for r in range(0, tm, rb):
        def delayed(col, lanes, b, back):
            at = pl.ds(halo + r - back, rb)
            if b == 0:
                return z_scr[at, col + lanes.start:col + lanes.stop]
            base = 0 if col == COL_A_VAL else GROUP_W
            return sh_scr[b - 1, at, base + lanes.start:base + lanes.stop]

        acc = jnp.broadcast_to(cab_ref[...], (rb, GROUP_W))
        for j in range(CONV_A_WIDTH):
            k = CONV_A_WIDTH - 1 - j
            tap = delayed(COL_A_VAL, slice(0, GROUP_W), j % sub, j - j % sub)
            acc = acc + caw_ref[k:k + 1, :] * tap
        c_scr[r:r + rb, :] = acc

        lo, hi = slice(0, half), slice(half, GROUP_W)
        s2 = delayed(COL_ZB, lo, 0, 0) + delayed(COL_ZB, lo, 1, 0)
        s4 = s2 + delayed(COL_ZB, lo, 2, 0) + delayed(COL_ZB, lo, 3, 0)
        s8 = delayed(COL_ZB, hi, 0, 0)
        for b in range(1, sub):
            s8 = s8 + delayed(COL_ZB, hi, b, 0)
        s16 = s8
        for b in range(sub):
            s16 = s16 + delayed(COL_ZB, hi, b, sub)
        sums = jnp.concatenate(
            [jnp.where(first_head, s2, s4), jnp.where(first_head, s8, s16)], axis=1)
        zb = z_scr[pl.ds(halo + r, rb), COL_ZB:COL_ZB + GROUP_W]
        pos = step * tm + r + row_iota
        cnt = jnp.minimum(pos + 1, win).astype(f32)
        d_scr[r:r + rb, :] = (sums / cnt - zb).astype(bf16)

        rows = pl.ds(halo + r, rb)
        q0 = z_scr[pl.ds(halo + r - 2, rb), COL_D_IN:COL_D_IN + GROUP_W]
        q1 = z_scr[pl.ds(halo + r - 1, rb), COL_D_IN:COL_D_IN + GROUP_W]
        q2 = z_scr[rows, COL_D_IN:COL_D_IN + GROUP_W]
        conv = scw_ref[0:1, :] * q0 + scw_ref[1:2, :] * q1 + scw_ref[2:3, :] * q2
        b_gate = z_scr[rows, COL_B_GATE:COL_B_GATE + GROUP_W]
        y_scr[r:r + rb, 3 * GROUP_W:4 * GROUP_W] = (b_gate * conv).astype(bf16)

    avg = avg_ref[...]
    ya = _head_layer_norm(c_scr[...], nag_ref[...], nab_ref[...], avg)
    y_scr[:, 0:GROUP_W] = _silu(ya).astype(bf16)
    y_scr[:, GROUP_W:2 * GROUP_W] = (_dot(d_scr[...], pw_ref[...]) * ps_ref[...]).astype(bf16)

    t_idx = lax.broadcasted_iota(jnp.int32, (CHUNK, N_HEADS * CHUNK), 0)
    s_idx = lax.broadcasted_iota(jnp.int32, (CHUNK, N_HEADS * CHUNK), 1) % CHUNK
    w_cat = jnp.where(s_idx <= t_idx, sw_ref[...], 0.0).astype(bf16)
    head_of_lane = lax.broadcasted_iota(jnp.int32, (CHUNK, GROUP_W), 1) // HEAD_DIM
    for c in range(tm // CHUNK):
        rows = pl.ds(halo + c * CHUNK, CHUNK)
        u = jax.nn.gelu(z_scr[rows, COL_U:COL_U + GROUP_W])
        v = jax.nn.gelu(z_scr[rows, COL_V:COL_V + GROUP_W])
        v = _head_layer_norm(v, sng_ref[...], snb_ref[...], avg).astype(bf16)
        v_stack = jnp.concatenate(
            [jnp.where(head_of_lane == hd, v, jnp.zeros_like(v)) for hd in range(N_HEADS)], axis=0)
        s = _dot(w_cat, v_stack) + sb_ref[...]
        y_scr[c * CHUNK:(c + 1) * CHUNK, 2 * GROUP_W:3 * GROUP_W] = (u * s).astype(bf16)

    o_ref[...] = x_ref[...] + _dot(y_scr[...], wmo_ref[...])


def _mixer_prompt_call(x, batch, seq, layer, wts, avg):
    tm, ns = MIX_TILE, MIX_SEQS
    x = x.reshape(batch, seq, D_MODEL)
    row_spec = pl.BlockSpec((ns, tm, D_MODEL), lambda b, l: (b, l, 0))

    def state_spec(rows):
        return pl.BlockSpec((ns, rows, GROUP_W), lambda b, l: (b, 0, 0))

    in_specs = [row_spec] + [_resident(w, layer) for w in wts] + [_resident(avg)]
    out = pl.pallas_call(
        functools.partial(_mixer_prompt_kernel, layer=layer),
        grid=(batch // ns, seq // tm),
        in_specs=in_specs,
        out_specs=[row_spec, state_spec(CONV_A_WIDTH - 1), state_spec(POOL_STATE),
                   state_spec(SHORT_CONV_WIDTH - 1)],
        out_shape=[jax.ShapeDtypeStruct((batch, seq, D_MODEL), f32),
                   jax.ShapeDtypeStruct((batch, CONV_A_WIDTH - 1, GROUP_W), f32),
                   jax.ShapeDtypeStruct((batch, POOL_STATE, GROUP_W), f32),
                   jax.ShapeDtypeStruct((batch, SHORT_CONV_WIDTH - 1, GROUP_W), f32)],
        scratch_shapes=[pltpu.VMEM((ns, MIX_HALO + tm, MIX_IN_W), f32),
                        pltpu.VMEM((ns, V7X_SUBLANES - 1, MIX_HALO + tm, 2 * GROUP_W), f32),
                        pltpu.VMEM((ns, tm, GROUP_W), f32),
                        pltpu.VMEM((ns, tm, GROUP_W), bf16),
                        pltpu.VMEM((ns, tm, D_MODEL), bf16)],
        compiler_params=pltpu.CompilerParams(
            dimension_semantics=("arbitrary", "arbitrary"),
            vmem_limit_bytes=V7X_VMEM_LIMIT_BYTES),
        name="mixer_prompt",
    )(x, *wts, avg)
    return (out[0].reshape(batch * seq, D_MODEL),) + tuple(out[1:])


def _mixer_sample_kernel(x_ref, gm_ref, wmi_ref, wmo_ref, caw_ref, cab_ref, nag_ref, nab_ref,
                         pw_ref, ps_ref, sng_ref, snb_ref, coef_ref, sb_ref, scw_ref, avg_ref,
                         sta_ref, stp_ref, sts_ref,
                         o_ref, sa_ref, sp_ref, ss_ref, v_ref,
                         z_scr, xa, xp, xs, c_scr, y_scr, *, dec_seq, layer):
    gm_ref, cab_ref, nag_ref, nab_ref, ps_ref, sng_ref, snb_ref = (
        _LayerRow(r, layer) for r in (gm_ref, cab_ref, nag_ref, nab_ref, ps_ref, sng_ref, snb_ref))
    nb, steps = SAMPLE_SEQS, dec_seq
    rows = nb * steps
    hist_a, hist_p, hist_s = CONV_A_WIDTH - 1, POOL_STATE, SHORT_CONV_WIDTH - 1
    avg = avg_ref[...]

    def slabs(col):
        return z_scr[:, col:col + GROUP_W].reshape(steps, nb, GROUP_W)

    x = x_ref[...].reshape(rows, D_MODEL)
    z_scr[...] = _dot(_rms(x, gm_ref[...]).astype(bf16), wmi_ref[...])

    xa[0:hist_a] = sta_ref[...]
    xa[hist_a:hist_a + steps] = slabs(COL_A_VAL) * jax.nn.sigmoid(slabs(COL_A_GATE))
    sa_ref[...] = xa[steps:steps + hist_a]
    for t in range(steps):
        acc = jnp.broadcast_to(cab_ref[...], (nb, GROUP_W))
        for k in range(CONV_A_WIDTH):
            acc = acc + caw_ref[k:k + 1, :] * xa[t + k]
        c_scr[t] = acc
    ya = _head_layer_norm(c_scr[...].reshape(rows, GROUP_W), nag_ref[...], nab_ref[...], avg)
    y_scr[:, 0:GROUP_W] = _silu(ya).astype(bf16)

    xp[0:hist_p] = stp_ref[...]
    xp[hist_p:hist_p + steps] = slabs(COL_ZB)
    sp_ref[...] = xp[steps:steps + hist_p]
    lane_group = lax.broadcasted_iota(jnp.int32, (nb, GROUP_W), 1) // HEAD_DIM
    win = jnp.left_shift(2, lane_group)
    for t in range(steps):
        zb = xp[hist_p + t]
        run = zb
        sums = []
        for j in range(1, max(POOL_WINDOWS)):
            run = run + xp[hist_p + t - j]
            if j + 1 in POOL_WINDOWS:
                sums.append(run)
        cnt = jnp.minimum(PAST_LEN + t + 1, win).astype(f32)
        c_scr[t] = _pool_select(sums, lane_group) / cnt - zb
    d = c_scr[...].reshape(rows, GROUP_W).astype(bf16)
    y_scr[:, GROUP_W:2 * GROUP_W] = (_dot(d, pw_ref[...]) * ps_ref[...]).astype(bf16)

    v = _head_layer_norm(jax.nn.gelu(z_scr[:, COL_V:COL_V + GROUP_W]),
                         sng_ref[...], snb_ref[...], avg)
    v_ref[...] = v.reshape(steps, nb, GROUP_W)
    for t in range(steps):
        s = jnp.broadcast_to(sb_ref[t:t + 1, :], (nb, GROUP_W))
        for j in range(t + 1):
            s = s + coef_ref[j, t:t + 1, :] * v_ref[j]
        u = jax.nn.gelu(z_scr[t * nb:(t + 1) * nb, COL_U:COL_U + GROUP_W])
        y_scr[t * nb:(t + 1) * nb, 2 * GROUP_W:3 * GROUP_W] = (u * s).astype(bf16)

    xs[0:hist_s] = sts_ref[...]
    xs[hist_s:hist_s + steps] = slabs(COL_C_GATE) * slabs(COL_D_IN)
    ss_ref[...] = xs[steps:steps + hist_s]
    for t in range(steps):
        conv = scw_ref[0:1, :] * xs[t]
        for k in range(1, SHORT_CONV_WIDTH):
            conv = conv + scw_ref[k:k + 1, :] * xs[t + k]
        b_gate = z_scr[t * nb:(t + 1) * nb, COL_B_GATE:COL_B_GATE + GROUP_W]
        y_scr[t * nb:(t + 1) * nb, 3 * GROUP_W:4 * GROUP_W] = (b_gate * conv).astype(bf16)

    out = x_ref[...].reshape(rows, D_MODEL) + _dot(y_scr[...], wmo_ref[...])
    o_ref[...] = out.reshape(steps, nb, D_MODEL)


def _mixer_sample_call(x, st_a, st_p, st_s, dec_batch, dec_seq, layer, wts, avg):
    nb = SAMPLE_SEQS
    rows = nb * dec_seq
    row_spec = pl.BlockSpec((dec_seq, nb, D_MODEL), lambda i: (0, i, 0))

    def state_spec(r):
        return pl.BlockSpec((r, nb, GROUP_W), lambda i: (0, i, 0))

    def state_in_spec(r):
        return pl.BlockSpec((None, r, nb, GROUP_W), lambda i: (layer, 0, i, 0))

    def state_shape(r):
        return jax.ShapeDtypeStruct((r, dec_batch, GROUP_W), f32)

    hist_a, hist_p, hist_s = CONV_A_WIDTH - 1, POOL_STATE, SHORT_CONV_WIDTH - 1
    in_specs = ([row_spec] + [_resident(w, layer) for w in wts] + [_resident(avg)]
                + [state_in_spec(hist_a), state_in_spec(hist_p), state_in_spec(hist_s)])
    return pl.pallas_call(
        functools.partial(_mixer_sample_kernel, dec_seq=dec_seq, layer=layer),
        grid=(dec_batch // nb,),
        in_specs=in_specs,
        out_specs=[row_spec, state_spec(hist_a), state_spec(hist_p), state_spec(hist_s),
                   state_spec(dec_seq)],
        out_shape=[jax.ShapeDtypeStruct((dec_seq, dec_batch, D_MODEL), f32),
                   state_shape(hist_a), state_shape(hist_p), state_shape(hist_s),
                   state_shape(dec_seq)],
        scratch_shapes=[pltpu.VMEM((rows, MIX_IN_W), f32),
                        pltpu.VMEM((hist_a + dec_seq, nb, GROUP_W), f32),
                        pltpu.VMEM((hist_p + dec_seq, nb, GROUP_W), f32),
                        pltpu.VMEM((hist_s + dec_seq, nb, GROUP_W), f32),
                        pltpu.VMEM((dec_seq, nb, GROUP_W), f32),
                        pltpu.VMEM((rows, D_MODEL), bf16)],
        compiler_params=pltpu.CompilerParams(
            dimension_semantics=("arbitrary",), vmem_limit_bytes=V7X_VMEM_LIMIT_BYTES),
        name="mixer_sample",
    )(x, *wts, avg, st_a, st_p, st_s)


def _block_diag(w):
    n, g, c, d = w.shape
    eye = jnp.eye(g, dtype=w.dtype)
    return (eye[None, :, None, :, None] * w[:, :, :, None, :]).reshape(n, g * c, g * d)


def kernel(x_prompt, x_sample, p_prompt, p_sample, state_conv_a, state_pool, state_short_conv,
           norm_ffn1, w_ffn1_gate, w_ffn1_up, w_ffn1_down, norm_mix, w_mix_in,
           conv_a_w, conv_a_b, norm_a_g, norm_a_b, pool_w, pool_scale,
           sgu_norm_g, sgu_norm_b, sgu_w, sgu_b, short_conv_w, w_mix_out,
           norm_ffn2, w_ffn2_gate, w_ffn2_up, w_ffn2_down,
           norm_ple, w_ple_gate, w_ple_proj, norm_final):
    depth = w_mix_in.shape[0]
    batch, seq, _ = x_prompt.shape
    dec_batch, dec_seq, _ = x_sample.shape
    assert seq % MIX_TILE == 0 and MIX_TILE % CHUNK == 0 and dec_seq <= CHUNK
    assert batch % MIX_SEQS == 0
    assert dec_batch % SAMPLE_SEQS == 0 and dec_seq == V7X_SUBLANES
    assert (batch * seq) % FFN_TILE == 0 and (dec_batch * dec_seq) % FFN_TILE == 0
    assert all(c % 256 == 0 for c in FFN_COL_SPLITS)

    avg = _block_diag(jnp.full((1, N_HEADS, HEAD_DIM, HEAD_DIM), 1.0 / HEAD_DIM, f32))[0]
    avg = avg.astype(bf16)

    ffn1_mats = [w[0].astype(bf16) for w in (w_ffn1_gate, w_ffn1_up, w_ffn1_down)]
    cast_in_ffn1 = (w_ffn2_gate, w_ffn2_up, w_ffn2_down, w_ple_gate, w_ple_proj,
                    w_mix_in, w_mix_out)
    cast_in_ffn2 = (w_ffn1_gate, w_ffn1_up, w_ffn1_down)
    gf = norm_final.reshape(1, D_MODEL)
    mix_tail = (conv_a_w, conv_a_b, norm_a_g, norm_a_b,
                _block_diag(pool_w).astype(bf16), pool_scale,
                sgu_norm_g, sgu_norm_b)
    sw_prompt = jnp.transpose(sgu_w, (0, 2, 1, 3)).reshape(depth, CHUNK, N_HEADS * CHUNK)
    sb_prompt = jnp.repeat(jnp.swapaxes(sgu_b, 1, 2), HEAD_DIM, axis=2)
    sw_sample = jnp.repeat(jnp.transpose(sgu_w[:, :, :dec_seq, :dec_seq], (0, 3, 2, 1)),
                           HEAD_DIM, axis=3)
    sb_sample = sb_prompt[:, :dec_seq]

    sample_rows = dec_seq * dec_batch
    pp = p_prompt.reshape(depth, batch * seq, D_PLE)
    ps = jnp.swapaxes(p_sample, 1, 2).reshape(depth, sample_rows, D_PLE)
    xp = x_prompt.reshape(batch * seq, D_MODEL)
    xs = jnp.swapaxes(x_sample, 0, 1).reshape(sample_rows, D_MODEL)
    st_a, st_p, st_s = (jnp.swapaxes(st, 1, 2)
                        for st in (state_conv_a, state_pool, state_short_conv))
    prompt_states, sample_states = [], []
    for i in range(depth):
        last = i == depth - 1

        xs, xp, mats = _ffn_call(xs, xp, i, (norm_ffn1, *ffn1_mats),
                                 cast=cast_in_ffn1, cast_layer=i)
        ffn2_mats, ple_mats, mix_mats = mats[0:3], mats[3:5], mats[5:7]
        mix_head = (norm_mix, *mix_mats) + mix_tail
        xp, sa, sp, ss = _mixer_prompt_call(
            xp, batch, seq, i, mix_head + (sw_prompt, sb_prompt, short_conv_w), avg)
        prompt_states.append((sa, sp, ss))
        xs, sa, sp, ss, sv = _mixer_sample_call(
            xs.reshape(dec_seq, dec_batch, D_MODEL), st_a, st_p, st_s, dec_batch, dec_seq, i,
            mix_head + (sw_sample, sb_sample, short_conv_w), avg)
        xs = xs.reshape(sample_rows, D_MODEL)
        sample_states.append((sa, sp, ss, sv))
        xs, xp, ffn1_mats = _ffn_call(
            xs, xp, i, (norm_ffn2, *ffn2_mats), ple=(ps, pp, (norm_ple, *ple_mats), gf),
            final_norm=last, cast=() if last else cast_in_ffn2, cast_layer=i + 1)

    def stack(states, j):
        return jnp.stack([s[j] for s in states])

    def stack_sample(j):
        return jnp.swapaxes(stack(sample_states, j), 1, 2)

    y_sample = jnp.swapaxes(xs.reshape(dec_seq, dec_batch, D_MODEL), 0, 1)
    return (xp.reshape(batch, seq, D_MODEL), y_sample,
            stack(prompt_states, 0), stack_sample(0),
            stack(prompt_states, 1), stack_sample(1),
            stack(prompt_states, 2), stack_sample(2),
            stack_sample(3))
```

```python
import functools

import jax
import jax.numpy as jnp
from jax import lax
from jax.experimental import pallas as pl
from jax.experimental.pallas import tpu as pltpu

D_MODEL = 1024
D_PLE = 256
D_FF = 2816
GROUP_W = 256
N_HEADS = 4
HEAD_DIM = 64
MIX_IN_W = 8 * GROUP_W
CONV_A_WIDTH = 31
POOL_WINDOWS = (2, 4, 8, 16)
POOL_STATE = 15
CHUNK = 128
SHORT_CONV_WIDTH = 3
EPS = 1e-6
PAST_LEN = 16384

V7X_SUBLANES = 8
V7X_BF16_ROWS = 16
V7X_VMEM_BYTES = 64 * 1024 * 1024
V7X_VMEM_LIMIT_BYTES = V7X_VMEM_BYTES - 4 * 1024 * 1024

FFN_TILE = 1024
FFN_SUB_TILE = 512
FFN_COL_SPLITS = (0, 1536, D_FF)
MIX_TILE = 512
MIX_SEQS = 2
MIX_HALO = 32
MIX_ROWS = 128
SAMPLE_SEQS = 64

COL_A_VAL, COL_A_GATE, COL_ZB, COL_U, COL_V, COL_B_GATE, COL_C_GATE, COL_D_IN = (
    i * GROUP_W for i in range(8))

bf16 = jnp.bfloat16
f32 = jnp.float32


def _dot(a, b):
    return jnp.dot(a, b, preferred_element_type=f32)


def _rms(x, g):
    ms = jnp.mean(x * x, axis=-1, keepdims=True)
    return x * lax.rsqrt(ms + EPS) * g


def _group_mean(x, avg):
    hi = x.astype(bf16)
    lo = (x - hi.astype(f32)).astype(bf16)
    return _dot(hi, avg) + _dot(lo, avg)


def _head_layer_norm(x, g, b, avg):
    mu = _group_mean(x, avg)
    xc = x - mu
    var = _dot((xc * xc).astype(bf16), avg)
    return xc * lax.rsqrt(var + EPS) * g + b


def _silu(x):
    return x * jax.nn.sigmoid(x)


def _pool_select(sums, lane_group):
    s2, s4, s8, s16 = sums
    return jnp.where(lane_group == 0, s2,
                     jnp.where(lane_group == 1, s4, jnp.where(lane_group == 2, s8, s16)))


def _ffn_rows(x_ref, rows, g_ref, wg_ref, wu_ref, wd_ref, act_ref):
    h = _rms(x_ref[rows, :], g_ref[...]).astype(bf16)
    for lo, hi in zip(FFN_COL_SPLITS[:-1], FFN_COL_SPLITS[1:]):
        cs = slice(lo, hi)
        gate = _dot(h, wg_ref[:, cs])
        up = _dot(h, wu_ref[:, cs])
        act_ref[rows, cs] = (_silu(gate) * up).astype(bf16)
    return x_ref[rows, :] + 0.5 * _dot(act_ref[rows, :], wd_ref[...])


def _sub_tiles():
    return [pl.ds(r, FFN_SUB_TILE) for r in range(0, FFN_TILE, FFN_SUB_TILE)]


class _SideCast:
    def __init__(self, arrays, layer, chunks):
        self.layer, self.chunks = layer, chunks
        self.shapes = [a.shape[1:] for a in arrays]
        assert all(r % (chunks * V7X_BF16_ROWS) == 0 for r, _ in self.shapes)
        self.rows = [r // chunks for r, _ in self.shapes]

    def in_copy(self, a, j, srcs, bufs, sem):
        start = pl.multiple_of(j * self.rows[a], V7X_BF16_ROWS)
        return pltpu.make_async_copy(
            srcs[a].at[self.layer, pl.ds(start, self.rows[a]), :], bufs[a], sem.at[a])

    def out_copy(self, a, j, bufs, dsts, sem):
        start = pl.multiple_of(j * self.rows[a], V7X_BF16_ROWS)
        return pltpu.make_async_copy(
            bufs[a], dsts[a].at[pl.ds(start, self.rows[a]), :], sem.at[a])


def _ffn_kernel(*refs, sample_steps, layer, ple, final_norm, side):
    refs = list(refs)
    take = lambda n: [refs.pop(0) for _ in range(n)]
    xs_ref, xp_ref, g_ref, wg_ref, wu_ref, wd_ref = take(6)
    if ple:
        ps_ref, pp_ref, gp_ref, wpg_ref, wpp_ref, gf_ref = take(6)
        gp_ref = _LayerRow(gp_ref, layer)
    n_side = len(side.shapes) if side else 0
    srcs = take(n_side)
    os_ref, op_ref = take(2)
    dsts = take(n_side)
    (act_ref,) = take(1)
    in_bufs, out_bufs = take(n_side), take(n_side)
    in_sem, out_sem = take(2) if side else (None, None)
    g_ref = _LayerRow(g_ref, layer)

    step = pl.program_id(0)
    is_sample = step < sample_steps
    chunk = step - sample_steps
    has_chunk = jnp.logical_and(chunk >= 0, chunk < side.chunks) if side else None

    if side:
        @pl.when(has_chunk)
        def _():
            for a in range(n_side):
                side.in_copy(a, chunk, srcs, in_bufs, in_sem).start()

    for rows in _sub_tiles():
        op_ref[rows, :] = jnp.where(is_sample, xs_ref[rows, :], xp_ref[rows, :])
        op_ref[rows, :] = _ffn_rows(op_ref, rows, g_ref, wg_ref, wu_ref, wd_ref, act_ref)
    for rows in _sub_tiles():
        if ple:
            p = jnp.where(is_sample, ps_ref[rows, :], pp_ref[rows, :]).astype(bf16)
            gate = jax.nn.sigmoid(
                _dot(_rms(op_ref[rows, :], gp_ref[...]).astype(bf16), wpg_ref[...]))
            x = op_ref[rows, :] + gate * _dot(p, wpp_ref[...])
            if final_norm:
                x = _rms(x, gf_ref[...])
            op_ref[rows, :] = x

    @pl.when(is_sample)
    def _():
        os_ref[...] = op_ref[...]

    if side:
        @pl.when(jnp.logical_and(chunk >= 1, chunk < side.chunks))
        def _():
            for a in range(n_side):
                side.out_copy(a, chunk - 1, out_bufs, dsts, out_sem).wait()

        @pl.when(has_chunk)
        def _():
            for a in range(n_side):
                side.in_copy(a, chunk, srcs, in_bufs, in_sem).wait()
                out_bufs[a][...] = in_bufs[a][...].astype(bf16)
                side.out_copy(a, chunk, out_bufs, dsts, out_sem).start()

        @pl.when(chunk == side.chunks - 1)
        def _():
            for a in range(n_side):
                side.out_copy(a, chunk, out_bufs, dsts, out_sem).wait()


class _LayerRow:
    def __init__(self, ref, layer):
        self.ref, self.layer = ref, layer

    def __getitem__(self, idx):
        assert idx is Ellipsis
        return self.ref[self.layer:self.layer + 1, :]


def _resident(w, layer=None):
    if layer is None or w.ndim == 2:
        return pl.BlockSpec(w.shape, lambda *_: (0,) * w.ndim, pipeline_mode=pl.Buffered(1))
    return pl.BlockSpec((None,) + w.shape[1:], lambda *_: (layer,) + (0,) * (w.ndim - 1),
                        pipeline_mode=pl.Buffered(1))


def _ffn_call(xs, xp, layer, ffn_w, ple=None, final_norm=False, cast=(), cast_layer=0):
    tile = FFN_TILE
    ns, np_ = xs.shape[0] // tile, xp.shape[0] // tile

    def sample_tile(i):
        return jnp.minimum(i, ns - 1)

    def prompt_tile(i):
        return jnp.maximum(i - ns, 0)

    xs_spec = pl.BlockSpec((tile, D_MODEL), lambda i: (sample_tile(i), 0),
                           pipeline_mode=pl.Buffered(1))
    xp_spec = pl.BlockSpec((tile, D_MODEL), lambda i: (prompt_tile(i), 0))
    in_specs = [xs_spec, xp_spec] + [_resident(w) for w in ffn_w]
    args = [xs, xp, *ffn_w]
    name = "ffn"
    if ple is not None:
        ps, pp, ple_w, gf = ple
        name = "ffn_ple"
        in_specs += ([pl.BlockSpec((None, tile, D_PLE), lambda i: (layer, sample_tile(i), 0),
                                   pipeline_mode=pl.Buffered(1)),
                      pl.BlockSpec((None, tile, D_PLE), lambda i: (layer, prompt_tile(i), 0))]
                     + [_resident(w) for w in ple_w] + [_resident(gf)])
        args += [ps, pp, *ple_w, gf]
    side = _SideCast(cast, cast_layer, np_) if cast else None
    any_spec = pl.BlockSpec(memory_space=pl.ANY)
    out_specs = [xs_spec, xp_spec] + [any_spec] * len(cast)
    out_shape = ([jax.ShapeDtypeStruct(xs.shape, f32), jax.ShapeDtypeStruct(xp.shape, f32)]
                 + [jax.ShapeDtypeStruct(a.shape[1:], bf16) for a in cast])
    scratch = [pltpu.VMEM((tile, D_FF), bf16)]
    if side:
        scratch += [pltpu.VMEM((r, s[1]), f32) for r, s in zip(side.rows, side.shapes)]
        scratch += [pltpu.VMEM((r, s[1]), bf16) for r, s in zip(side.rows, side.shapes)]
        scratch += [pltpu.SemaphoreType.DMA((len(cast),)), pltpu.SemaphoreType.DMA((len(cast),))]
    out = pl.pallas_call(
        functools.partial(_ffn_kernel, sample_steps=ns, layer=layer, ple=ple is not None,
                          final_norm=final_norm, side=side),
        grid=(ns + np_,),
        in_specs=in_specs + [any_spec] * len(cast),
        out_specs=out_specs,
        out_shape=out_shape,
        scratch_shapes=scratch,
        compiler_params=pltpu.CompilerParams(
            dimension_semantics=("arbitrary",), vmem_limit_bytes=V7X_VMEM_LIMIT_BYTES),
        name=name,
    )(*args, *cast)
    return out[0], out[1], list(out[2:])


def _mixer_prompt_kernel(x_ref, gm_ref, wmi_ref, wmo_ref, caw_ref, cab_ref, nag_ref, nab_ref,
                         pw_ref, ps_ref, sng_ref, snb_ref, sw_ref, sb_ref, scw_ref, avg_ref,
                         o_ref, sa_ref, sp_ref, ss_ref,
                         z_all, sh_all, c_all, d_all, y_all, *, layer):
    gm_ref, cab_ref, nag_ref, nab_ref, ps_ref, sng_ref, snb_ref = (
        _LayerRow(r, layer) for r in (gm_ref, cab_ref, nag_ref, nab_ref, ps_ref, sng_ref, snb_ref))
    tm, halo = MIX_TILE, MIX_HALO
    sub, half = V7X_SUBLANES, GROUP_W // 2
    step = pl.program_id(1)
    weights = (gm_ref, wmi_ref, wmo_ref, caw_ref, cab_ref, nag_ref, nab_ref, pw_ref, ps_ref,
               sng_ref, snb_ref, sw_ref, sb_ref, scw_ref, avg_ref)

    @pl.when(step == 0)
    def _():
        z_all[:, 0:halo, :] = jnp.zeros((MIX_SEQS, halo, MIX_IN_W), f32)
        sh_all[:, :, 0:halo, :] = jnp.zeros((MIX_SEQS, sub - 1, halo, 2 * GROUP_W), f32)

    for s in range(MIX_SEQS):
        h = _rms(x_ref[s], gm_ref[...]).astype(bf16)
        z_all[s, halo:halo + tm, :] = _dot(h, wmi_ref[...])
    for s in range(MIX_SEQS):
        _mixer_prompt_tile(step, x_ref.at[s], o_ref.at[s], weights, z_all.at[s], sh_all.at[s],
                           c_all.at[s], d_all.at[s], y_all.at[s])

    @pl.when(step == pl.num_programs(1) - 1)
    def _():
        for s in range(MIX_SEQS):
            z_scr = z_all.at[s]
            sa_ref[s] = z_scr[pl.ds(halo + tm - (CONV_A_WIDTH - 1), CONV_A_WIDTH - 1),
                              COL_A_VAL:COL_A_VAL + GROUP_W]
            sp_ref[s] = z_scr[pl.ds(halo + tm - POOL_STATE, POOL_STATE),
                              COL_ZB:COL_ZB + GROUP_W]
            ss_ref[s] = z_scr[pl.ds(halo + tm - (SHORT_CONV_WIDTH - 1), SHORT_CONV_WIDTH - 1),
                              COL_D_IN:COL_D_IN + GROUP_W]

    for s in range(MIX_SEQS):
        z_scr, sh_scr = z_all.at[s], sh_all.at[s]
        for col in (COL_A_VAL, COL_ZB, COL_D_IN):
            z_scr[0:halo, col:col + GROUP_W] = z_scr[tm:tm + halo, col:col + GROUP_W]
        sh_scr[0:3, 0:halo, :] = sh_scr[0:3, tm:tm + halo, :]
        for lanes in (slice(0, GROUP_W), slice(GROUP_W + half, 2 * GROUP_W)):
            sh_scr[3:sub - 1, 0:halo, lanes] = sh_scr[3:sub - 1, tm:tm + halo, lanes]


def _mixer_prompt_tile(step, x_ref, o_ref, weights, z_scr, sh_scr, c_scr, d_scr, y_scr):
    (gm_ref, wmi_ref, wmo_ref, caw_ref, cab_ref, nag_ref, nab_ref, pw_ref, ps_ref,
     sng_ref, snb_ref, sw_ref, sb_ref, scw_ref, avg_ref) = weights
    tm, halo, rb = MIX_TILE, MIX_HALO, MIX_ROWS
    sub, half = V7X_SUBLANES, GROUP_W // 2

    for r in range(0, tm, rb):
        rows = pl.ds(halo + r, rb)
        a_val = z_scr[rows, COL_A_VAL:COL_A_VAL + GROUP_W]
        a_gate = z_scr[rows, COL_A_GATE:COL_A_GATE + GROUP_W]
        z_scr[rows, COL_A_VAL:COL_A_VAL + GROUP_W] = a_val * jax.nn.sigmoid(a_gate)
        c_gate = z_scr[rows, COL_C_GATE:COL_C_GATE + GROUP_W]
        d_in = z_scr[rows, COL_D_IN:COL_D_IN + GROUP_W]
        z_scr[rows, COL_D_IN:COL_D_IN + GROUP_W] = c_gate * d_in
        for b in range(1, sub):
            late = pl.ds(halo + r - b, rb)
            sh_scr[b - 1, rows, 0:GROUP_W] = z_scr[late, COL_A_VAL:COL_A_VAL + GROUP_W]
            if b < 4:
                sh_scr[b - 1, rows, GROUP_W:2 * GROUP_W] = z_scr[late, COL_ZB:COL_ZB + GROUP_W]
            else:
                sh_scr[b - 1, rows, GROUP_W + half:2 * GROUP_W] = (
                    z_scr[late, COL_ZB + half:COL_ZB + GROUP_W])

    lane_group = lax.broadcasted_iota(jnp.int32, (rb, GROUP_W), 1) // HEAD_DIM
    win = jnp.left_shift(2, lane_group)
    row_iota = lax.broadcasted_iota(jnp.int32, (rb, GROUP_W), 0)
    first_head = lax.broadcasted_iota(jnp.int32, (rb, half), 1) < HEAD_DIM

    for r in range(0, tm, rb):
        def delayed(col, lanes, b, back):
            at = pl.ds(halo + r - back, rb)
            if b == 0:
                return z_scr[at, col + lanes.start:col + lanes.stop]
            base = 0 if col == COL_A_VAL else GROUP_W
            return sh_scr[b - 1, at, base + lanes.start:base + lanes.stop]

        acc = jnp.broadcast_to(cab_ref[...], (rb, GROUP_W))
        for j in range(CONV_A_WIDTH):
            k = CONV_A_WIDTH - 1 - j
            tap = delayed(COL_A_VAL, slice(0, GROUP_W), j % sub, j - j % sub)
            acc = acc + caw_ref[k:k + 1, :] * tap
        c_scr[r:r + rb, :] = acc

        lo, hi = slice(0, half), slice(half, GROUP_W)
        s2 = delayed(COL_ZB, lo, 0, 0) + delayed(COL_ZB, lo, 1, 0)
        s4 = s2 + delayed(COL_ZB, lo, 2, 0) + delayed(COL_ZB, lo, 3, 0)
        s8 = delayed(COL_ZB, hi, 0, 0)
        for b in range(1, sub):
            s8 = s8 + delayed(COL_ZB, hi, b, 0)
        s16 = s8
        for b in range(sub):
            s16 = s16 + delayed(COL_ZB, hi, b, sub)
        sums = jnp.concatenate(
            [jnp.where(first_head, s2, s4), jnp.where(first_head, s8, s16)], axis=1)
        zb = z_scr[pl.ds(halo + r, rb), COL_ZB:COL_ZB + GROUP_W]
        pos = step * tm + r + row_iota
        cnt = jnp.minimum(pos + 1, win).astype(f32)
        d_scr[r:r + rb, :] = (sums / cnt - zb).astype(bf16)

        rows = pl.ds(halo + r, rb)
        q0 = z_scr[pl.ds(halo + r - 2, rb), COL_D_IN:COL_D_IN + GROUP_W]
        q1 = z_scr[pl.ds(halo + r - 1, rb), COL_D_IN:COL_D_IN + GROUP_W]
        q2 = z_scr[rows, COL_D_IN:COL_D_IN + GROUP_W]
        conv = scw_ref[0:1, :] * q0 + scw_ref[1:2, :] * q1 + scw_ref[2:3, :] * q2
        b_gate = z_scr[rows, COL_B_GATE:COL_B_GATE + GROUP_W]
        y_scr[r:r + rb, 3 * GROUP_W:4 * GROUP_W] = (b_gate * conv).astype(bf16)

    avg = avg_ref[...]
    ya = _head_layer_norm(c_scr[...], nag_ref[...], nab_ref[...], avg)
    y_scr[:, 0:GROUP_W] = _silu(ya).astype(bf16)
    y_scr[:, GROUP_W:2 * GROUP_W] = (_dot(d_scr[...], pw_ref[...]) * ps_ref[...]).astype(bf16)

    t_idx = lax.broadcasted_iota(jnp.int32, (CHUNK, N_HEADS * CHUNK), 0)
    s_idx = lax.broadcasted_iota(jnp.int32, (CHUNK, N_HEADS * CHUNK), 1) % CHUNK
    w_cat = jnp.where(s_idx <= t_idx, sw_ref[...], 0.0).astype(bf16)
    head_of_lane = lax.broadcasted_iota(jnp.int32, (CHUNK, GROUP_W), 1) // HEAD_DIM
    for c in range(tm // CHUNK):
        rows = pl.ds(halo + c * CHUNK, CHUNK)
        u = jax.nn.gelu(z_scr[rows, COL_U:COL_U + GROUP_W])
        v = jax.nn.gelu(z_scr[rows, COL_V:COL_V + GROUP_W])
        v = _head_layer_norm(v, sng_ref[...], snb_ref[...], avg).astype(bf16)
        v_stack = jnp.concatenate(
            [jnp.where(head_of_lane == hd, v, jnp.zeros_like(v)) for hd in range(N_HEADS)], axis=0)
        s = _dot(w_cat, v_stack) + sb_ref[...]
        y_scr[c * CHUNK:(c + 1) * CHUNK, 2 * GROUP_W:3 * GROUP_W] = (u * s).astype(bf16)

    o_ref[...] = x_ref[...] + _dot(y_scr[...], wmo_ref[...])


def _mixer_prompt_call(x, batch, seq, layer, wts, avg):
    tm, ns = MIX_TILE, MIX_SEQS
    x = x.reshape(batch, seq, D_MODEL)
    row_spec = pl.BlockSpec((ns, tm, D_MODEL), lambda b, l: (b, l, 0))

    def state_spec(rows):
        return pl.BlockSpec((ns, rows, GROUP_W), lambda b, l: (b, 0, 0))

    in_specs = [row_spec] + [_resident(w, layer) for w in wts] + [_resident(avg)]
    out = pl.pallas_call(
        functools.partial(_mixer_prompt_kernel, layer=layer),
        grid=(batch // ns, seq // tm),
        in_specs=in_specs,
        out_specs=[row_spec, state_spec(CONV_A_WIDTH - 1), state_spec(POOL_STATE),
                   state_spec(SHORT_CONV_WIDTH - 1)],
        out_shape=[jax.ShapeDtypeStruct((batch, seq, D_MODEL), f32),
                   jax.ShapeDtypeStruct((batch, CONV_A_WIDTH - 1, GROUP_W), f32),
                   jax.ShapeDtypeStruct((batch, POOL_STATE, GROUP_W), f32),
                   jax.ShapeDtypeStruct((batch, SHORT_CONV_WIDTH - 1, GROUP_W), f32)],
        scratch_shapes=[pltpu.VMEM((ns, MIX_HALO + tm, MIX_IN_W), f32),
                        pltpu.VMEM((ns, V7X_SUBLANES - 1, MIX_HALO + tm, 2 * GROUP_W), f32),
                        pltpu.VMEM((ns, tm, GROUP_W), f32),
                        pltpu.VMEM((ns, tm, GROUP_W), bf16),
                        pltpu.VMEM((ns, tm, D_MODEL), bf16)],
        compiler_params=pltpu.CompilerParams(
            dimension_semantics=("arbitrary", "arbitrary"),
            vmem_limit_bytes=V7X_VMEM_LIMIT_BYTES),
        name="mixer_prompt",
    )(x, *wts, avg)
    return (out[0].reshape(batch * seq, D_MODEL),) + tuple(out[1:])


def _mixer_sample_kernel(x_ref, gm_ref, wmi_ref, wmo_ref, caw_ref, cab_ref, nag_ref, nab_ref,
                         pw_ref, ps_ref, sng_ref, snb_ref, coef_ref, sb_ref, scw_ref, avg_ref,
                         sta_ref, stp_ref, sts_ref,
                         o_ref, sa_ref, sp_ref, ss_ref, v_ref,
                         z_scr, xa, xp, xs, c_scr, y_scr, *, dec_seq, layer):
    gm_ref, cab_ref, nag_ref, nab_ref, ps_ref, sng_ref, snb_ref = (
        _LayerRow(r, layer) for r in (gm_ref, cab_ref, nag_ref, nab_ref, ps_ref, sng_ref, snb_ref))
    nb, steps = SAMPLE_SEQS, dec_seq
    rows = nb * steps
    hist_a, hist_p, hist_s = CONV_A_WIDTH - 1, POOL_STATE, SHORT_CONV_WIDTH - 1
    avg = avg_ref[...]

    def slabs(col):
        return z_scr[:, col:col + GROUP_W].reshape(steps, nb, GROUP_W)

    x = x_ref[...].reshape(rows, D_MODEL)
    z_scr[...] = _dot(_rms(x, gm_ref[...]).astype(bf16), wmi_ref[...])

    xa[0:hist_a] = sta_ref[...]
    xa[hist_a:hist_a + steps] = slabs(COL_A_VAL) * jax.nn.sigmoid(slabs(COL_A_GATE))
    sa_ref[...] = xa[steps:steps + hist_a]
    for t in range(steps):
        acc = jnp.broadcast_to(cab_ref[...], (nb, GROUP_W))
        for k in range(CONV_A_WIDTH):
            acc = acc + caw_ref[k:k + 1, :] * xa[t + k]
        c_scr[t] = acc
    ya = _head_layer_norm(c_scr[...].reshape(rows, GROUP_W), nag_ref[...], nab_ref[...], avg)
    y_scr[:, 0:GROUP_W] = _silu(ya).astype(bf16)

    xp[0:hist_p] = stp_ref[...]
    xp[hist_p:hist_p + steps] = slabs(COL_ZB)
    sp_ref[...] = xp[steps:steps + hist_p]
    lane_group = lax.broadcasted_iota(jnp.int32, (nb, GROUP_W), 1) // HEAD_DIM
    win = jnp.left_shift(2, lane_group)
    for t in range(steps):
        zb = xp[hist_p + t]
        run = zb
        sums = []
        for j in range(1, max(POOL_WINDOWS)):
            run = run + xp[hist_p + t - j]
            if j + 1 in POOL_WINDOWS:
                sums.append(run)
        cnt = jnp.minimum(PAST_LEN + t + 1, win).astype(f32)
        c_scr[t] = _pool_select(sums, lane_group) / cnt - zb
    d = c_scr[...].reshape(rows, GROUP_W).astype(bf16)
    y_scr[:, GROUP_W:2 * GROUP_W] = (_dot(d, pw_ref[...]) * ps_ref[...]).astype(bf16)

    v = _head_layer_norm(jax.nn.gelu(z_scr[:, COL_V:COL_V + GROUP_W]),
                         sng_ref[...], snb_ref[...], avg)
    v_ref[...] = v.reshape(steps, nb, GROUP_W)
    for t in range(steps):
        s = jnp.broadcast_to(sb_ref[t:t + 1, :], (nb, GROUP_W))
        for j in range(t + 1):
            s = s + coef_ref[j, t:t + 1, :] * v_ref[j]
        u = jax.nn.gelu(z_scr[t * nb:(t + 1) * nb, COL_U:COL_U + GROUP_W])
        y_scr[t * nb:(t + 1) * nb, 2 * GROUP_W:3 * GROUP_W] = (u * s).astype(bf16)

    xs[0:hist_s] = sts_ref[...]
    xs[hist_s:hist_s + steps] = slabs(COL_C_GATE) * slabs(COL_D_IN)
    ss_ref[...] = xs[steps:steps + hist_s]
    for t in range(steps):
        conv = scw_ref[0:1, :] * xs[t]
        for k in range(1, SHORT_CONV_WIDTH):
            conv = conv + scw_ref[k:k + 1, :] * xs[t + k]
        b_gate = z_scr[t * nb:(t + 1) * nb, COL_B_GATE:COL_B_GATE + GROUP_W]
        y_scr[t * nb:(t + 1) * nb, 3 * GROUP_W:4 * GROUP_W] = (b_gate * conv).astype(bf16)

    out = x_ref[...].reshape(rows, D_MODEL) + _dot(y_scr[...], wmo_ref[...])
    o_ref[...] = out.reshape(steps, nb, D_MODEL)


def _mixer_sample_call(x, st_a, st_p, st_s, dec_batch, dec_seq, layer, wts, avg):
    nb = SAMPLE_SEQS
    rows = nb * dec_seq
    row_spec = pl.BlockSpec((dec_seq, nb, D_MODEL), lambda i: (0, i, 0))

    def state_spec(r):
        return pl.BlockSpec((r, nb, GROUP_W), lambda i: (0, i, 0))

    def state_in_spec(r):
        return pl.BlockSpec((None, r, nb, GROUP_W), lambda i: (layer, 0, i, 0))

    def state_shape(r):
        return jax.ShapeDtypeStruct((r, dec_batch, GROUP_W), f32)

    hist_a, hist_p, hist_s = CONV_A_WIDTH - 1, POOL_STATE, SHORT_CONV_WIDTH - 1
    in_specs = ([row_spec] + [_resident(w, layer) for w in wts] + [_resident(avg)]
                + [state_in_spec(hist_a), state_in_spec(hist_p), state_in_spec(hist_s)])
    return pl.pallas_call(
        functools.partial(_mixer_sample_kernel, dec_seq=dec_seq, layer=layer),
        grid=(dec_batch // nb,),
        in_specs=in_specs,
        out_specs=[row_spec, state_spec(hist_a), state_spec(hist_p), state_spec(hist_s),
                   state_spec(dec_seq)],
        out_shape=[jax.ShapeDtypeStruct((dec_seq, dec_batch, D_MODEL), f32),
                   state_shape(hist_a), state_shape(hist_p), state_shape(hist_s),
                   state_shape(dec_seq)],
        scratch_shapes=[pltpu.VMEM((rows, MIX_IN_W), f32),
                        pltpu.VMEM((hist_a + dec_seq, nb, GROUP_W), f32),
                        pltpu.VMEM((hist_p + dec_seq, nb, GROUP_W), f32),
                        pltpu.VMEM((hist_s + dec_seq, nb, GROUP_W), f32),
                        pltpu.VMEM((dec_seq, nb, GROUP_W), f32),
                        pltpu.VMEM((rows, D_MODEL), bf16)],
        compiler_params=pltpu.CompilerParams(
            dimension_semantics=("arbitrary",), vmem_limit_bytes=V7X_VMEM_LIMIT_BYTES),
        name="mixer_sample",
    )(x, *wts, avg, st_a, st_p, st_s)


def _block_diag(w):
    n, g, c, d = w.shape
    eye = jnp.eye(g, dtype=w.dtype)
    return (eye[None, :, None, :, None] * w[:, :, :, None, :]).reshape(n, g * c, g * d)


def kernel(x_prompt, x_sample, p_prompt, p_sample, state_conv_a, state_pool, state_short_conv,
           norm_ffn1, w_ffn1_gate, w_ffn1_up, w_ffn1_down, norm_mix, w_mix_in,
           conv_a_w, conv_a_b, norm_a_g, norm_a_b, pool_w, pool_scale,
           sgu_norm_g, sgu_norm_b, sgu_w, sgu_b, short_conv_w, w_mix_out,
           norm_ffn2, w_ffn2_gate, w_ffn2_up, w_ffn2_down,
           norm_ple, w_ple_gate, w_ple_proj, norm_final):
    depth = w_mix_in.shape[0]
    batch, seq, _ = x_prompt.shape
    dec_batch, dec_seq, _ = x_sample.shape
    assert seq % MIX_TILE == 0 and MIX_TILE % CHUNK == 0 and dec_seq <= CHUNK
    assert batch % MIX_SEQS == 0
    assert dec_batch % SAMPLE_SEQS == 0 and dec_seq == V7X_SUBLANES
    assert (batch * seq) % FFN_TILE == 0 and (dec_batch * dec_seq) % FFN_TILE == 0
    assert all(c % 256 == 0 for c in FFN_COL_SPLITS)

    avg = _block_diag(jnp.full((1, N_HEADS, HEAD_DIM, HEAD_DIM), 1.0 / HEAD_DIM, f32))[0]
    avg = avg.astype(bf16)

    ffn1_mats = [w[0].astype(bf16) for w in (w_ffn1_gate, w_ffn1_up, w_ffn1_down)]
    cast_in_ffn1 = (w_ffn2_gate, w_ffn2_up, w_ffn2_down, w_ple_gate, w_ple_proj,
                    w_mix_in, w_mix_out)
    cast_in_ffn2 = (w_ffn1_gate, w_ffn1_up, w_ffn1_down)
    gf = norm_final.reshape(1, D_MODEL)
    mix_tail = (conv_a_w, conv_a_b, norm_a_g, norm_a_b,
                _block_diag(pool_w).astype(bf16), pool_scale,
                sgu_norm_g, sgu_norm_b)
    sw_prompt = jnp.transpose(sgu_w, (0, 2, 1, 3)).reshape(depth, CHUNK, N_HEADS * CHUNK)
    sb_prompt = jnp.repeat(jnp.swapaxes(sgu_b, 1, 2), HEAD_DIM, axis=2)
    sw_sample = jnp.repeat(jnp.transpose(sgu_w[:, :, :dec_seq, :dec_seq], (0, 3, 2, 1)),
                           HEAD_DIM, axis=3)
    sb_sample = sb_prompt[:, :dec_seq]

    sample_rows = dec_seq * dec_batch
    pp = p_prompt.reshape(depth, batch * seq, D_PLE)
    ps = jnp.swapaxes(p_sample, 1, 2).reshape(depth, sample_rows, D_PLE)
    xp = x_prompt.reshape(batch * seq, D_MODEL)
    xs = jnp.swapaxes(x_sample, 0, 1).reshape(sample_rows, D_MODEL)
    st_a, st_p, st_s = (jnp.swapaxes(st, 1, 2)
                        for st in (state_conv_a, state_pool, state_short_conv))
    prompt_states, sample_states = [], []
    for i in range(depth):
        last = i == depth - 1

        xs, xp, mats = _ffn_call(xs, xp, i, (norm_ffn1, *ffn1_mats),
                                 cast=cast_in_ffn1, cast_layer=i)
        ffn2_mats, ple_mats, mix_mats = mats[0:3], mats[3:5], mats[5:7]
        mix_head = (norm_mix, *mix_mats) + mix_tail
        xp, sa, sp, ss = _mixer_prompt_call(
            xp, batch, seq, i, mix_head + (sw_prompt, sb_prompt, short_conv_w), avg)
        prompt_states.append((sa, sp, ss))
        xs, sa, sp, ss, sv = _mixer_sample_call(
            xs.reshape(dec_seq, dec_batch, D_MODEL), st_a, st_p, st_s, dec_batch, dec_seq, i,
            mix_head + (sw_sample, sb_sample, short_conv_w), avg)
        xs = xs.reshape(sample_rows, D_MODEL)
        sample_states.append((sa, sp, ss, sv))
        xs, xp, ffn1_mats = _ffn_call(
            xs, xp, i, (norm_ffn2, *ffn2_mats), ple=(ps, pp, (norm_ple, *ple_mats), gf),
            final_norm=last, cast=() if last else cast_in_ffn2, cast_layer=i + 1)

    def stack(states, j):
        return jnp.stack([s[j] for s in states])

    def stack_sample(j):
        return jnp.swapaxes(stack(sample_states, j), 1, 2)

    y_sample = jnp.swapaxes(xs.reshape(dec_seq, dec_batch, D_MODEL), 0, 1)
    return (xp.reshape(batch, seq, D_MODEL), y_sample,
            stack(prompt_states, 0), stack_sample(0),
            stack(prompt_states, 1), stack_sample(1),
            stack(prompt_states, 2), stack_sample(2),
            stack_sample(3))
```

```python
import functools

import jax
import jax.numpy as jnp
from jax import lax
from jax.experimental import pallas as pl
from jax.experimental.pallas import tpu as pltpu

D_MODEL = 1024
D_PLE = 256
D_FF = 2816
GROUP_W = 256
N_HEADS = 4
HEAD_DIM = 64
MIX_IN_W = 8 * GROUP_W
CONV_A_WIDTH = 31
POOL_WINDOWS = (2, 4, 8, 16)
POOL_STATE = 15
CHUNK = 128
SHORT_CONV_WIDTH = 3
EPS = 1e-6
PAST_LEN = 16384

V7X_SUBLANES = 8
V7X_BF16_ROWS = 16
V7X_VMEM_BYTES = 64 * 1024 * 1024
V7X_VMEM_LIMIT_BYTES = V7X_VMEM_BYTES - 4 * 1024 * 1024

FFN_TILE = 1024
FFN_SUB_TILE = 256
FFN_COL_SPLITS = (0, 1536, D_FF)
MIX_TILE = 512
MIX_SEQS = 2
MIX_HALO = 32
MIX_ROWS = 128
SAMPLE_SEQS = 64

COL_A_VAL, COL_A_GATE, COL_ZB, COL_U, COL_V, COL_B_GATE, COL_C_GATE, COL_D_IN = (
    i * GROUP_W for i in range(8))

bf16 = jnp.bfloat16
f32 = jnp.float32


def _dot(a, b):
    return jnp.dot(a, b, preferred_element_type=f32)


def _rms(x, g):
    ms = jnp.mean(x * x, axis=-1, keepdims=True)
    return x * lax.rsqrt(ms + EPS) * g


def _group_mean(x, avg):
    hi = x.astype(bf16)
    lo = (x - hi.astype(f32)).astype(bf16)
    return _dot(hi, avg) + _dot(lo, avg)


def _head_layer_norm(x, g, b, avg):
    mu = _group_mean(x, avg)
    xc = x - mu
    var = _dot((xc * xc).astype(bf16), avg)
    return xc * lax.rsqrt(var + EPS) * g + b


def _silu(x):
    return x * jax.nn.sigmoid(x)


def _pool_select(sums, lane_group):
    s2, s4, s8, s16 = sums
    return jnp.where(lane_group == 0, s2,
                     jnp.where(lane_group == 1, s4, jnp.where(lane_group == 2, s8, s16)))


def _ffn_rows(x_ref, rows, g_ref, wg_ref, wu_ref, wd_ref, act_ref):
    h = _rms(x_ref[rows, :], g_ref[...]).astype(bf16)
    for lo, hi in zip(FFN_COL_SPLITS[:-1], FFN_COL_SPLITS[1:]):
        cs = slice(lo, hi)
        gate = _dot(h, wg_ref[:, cs])
        up = _dot(h, wu_ref[:, cs])
        act_ref[rows, cs] = (_silu(gate) * up).astype(bf16)
    return x_ref[rows, :] + 0.5 * _dot(act_ref[rows, :], wd_ref[...])


def _sub_tiles():
    return [pl.ds(r, FFN_SUB_TILE) for r in range(0, FFN_TILE, FFN_SUB_TILE)]


class _SideCast:
    def __init__(self, arrays, layer, chunks):
        self.layer, self.chunks = layer, chunks
        self.shapes = [a.shape[1:] for a in arrays]
        assert all(r % (chunks * V7X_BF16_ROWS) == 0 for r, _ in self.shapes)
        self.rows = [r // chunks for r, _ in self.shapes]

    def in_copy(self, a, j, srcs, bufs, sem):
        start = pl.multiple_of(j * self.rows[a], V7X_BF16_ROWS)
        return pltpu.make_async_copy(
            srcs[a].at[self.layer, pl.ds(start, self.rows[a]), :], bufs[a], sem.at[a])

    def out_copy(self, a, j, bufs, dsts, sem):
        start = pl.multiple_of(j * self.rows[a], V7X_BF16_ROWS)
        return pltpu.make_async_copy(
            bufs[a], dsts[a].at[pl.ds(start, self.rows[a]), :], sem.at[a])


def _ffn_kernel(*refs, sample_steps, layer, ple, final_norm, side):
    refs = list(refs)
    take = lambda n: [refs.pop(0) for _ in range(n)]
    xs_ref, xp_ref, g_ref, wg_ref, wu_ref, wd_ref = take(6)
    if ple:
        ps_ref, pp_ref, gp_ref, wpg_ref, wpp_ref, gf_ref = take(6)
        gp_ref = _LayerRow(gp_ref, layer)
    n_side = len(side.shapes) if side else 0
    srcs = take(n_side)
    os_ref, op_ref = take(2)
    dsts = take(n_side)
    (act_ref,) = take(1)
    in_bufs, out_bufs = take(n_side), take(n_side)
    in_sem, out_sem = take(2) if side else (None, None)
    g_ref = _LayerRow(g_ref, layer)

    step = pl.program_id(0)
    is_sample = step < sample_steps
    chunk = step - sample_steps
    has_chunk = jnp.logical_and(chunk >= 0, chunk < side.chunks) if side else None

    if side:
        @pl.when(has_chunk)
        def _():
            for a in range(n_side):
                side.in_copy(a, chunk, srcs, in_bufs, in_sem).start()

    for rows in _sub_tiles():
        op_ref[rows, :] = jnp.where(is_sample, xs_ref[rows, :], xp_ref[rows, :])
        op_ref[rows, :] = _ffn_rows(op_ref, rows, g_ref, wg_ref, wu_ref, wd_ref, act_ref)
    for rows in _sub_tiles():
        if ple:
            p = jnp.where(is_sample, ps_ref[rows, :], pp_ref[rows, :]).astype(bf16)
            gate = jax.nn.sigmoid(
                _dot(_rms(op_ref[rows, :], gp_ref[...]).astype(bf16), wpg_ref[...]))
            x = op_ref[rows, :] + gate * _dot(p, wpp_ref[...])
            if final_norm:
                x = _rms(x, gf_ref[...])
            op_ref[rows, :] = x

    @pl.when(is_sample)
    def _():
        os_ref[...] = op_ref[...]

    if side:
        @pl.when(jnp.logical_and(chunk >= 1, chunk < side.chunks))
        def _():
            for a in range(n_side):
                side.out_copy(a, chunk - 1, out_bufs, dsts, out_sem).wait()

        @pl.when(has_chunk)
        def _():
            for a in range(n_side):
                side.in_copy(a, chunk, srcs, in_bufs, in_sem).wait()
                out_bufs[a][...] = in_bufs[a][...].astype(bf16)
                side.out_copy(a, chunk, out_bufs, dsts, out_sem).start()

        @pl.when(chunk == side.chunks - 1)
        def _():
            for a in range(n_side):
                side.out_copy(a, chunk, out_bufs, dsts, out_sem).wait()


class _LayerRow:
    def __init__(self, ref, layer):
        self.ref, self.layer = ref, layer

    def __getitem__(self, idx):
        assert idx is Ellipsis
        return self.ref[self.layer:self.layer + 1, :]


def _resident(w, layer=None):
    if layer is None or w.ndim == 2:
        return pl.BlockSpec(w.shape, lambda *_: (0,) * w.ndim, pipeline_mode=pl.Buffered(1))
    return pl.BlockSpec((None,) + w.shape[1:], lambda *_: (layer,) + (0,) * (w.ndim - 1),
                        pipeline_mode=pl.Buffered(1))


def _ffn_call(xs, xp, layer, ffn_w, ple=None, final_norm=False, cast=(), cast_layer=0):
    tile = FFN_TILE
    ns, np_ = xs.shape[0] // tile, xp.shape[0] // tile

    def sample_tile(i):
        return jnp.minimum(i, ns - 1)

    def prompt_tile(i):
        return jnp.maximum(i - ns, 0)

    xs_spec = pl.BlockSpec((tile, D_MODEL), lambda i: (sample_tile(i), 0),
                           pipeline_mode=pl.Buffered(1))
    xp_spec = pl.BlockSpec((tile, D_MODEL), lambda i: (prompt_tile(i), 0))
    in_specs = [xs_spec, xp_spec] + [_resident(w) for w in ffn_w]
    args = [xs, xp, *ffn_w]
    name = "ffn"
    if ple is not None:
        ps, pp, ple_w, gf = ple
        name = "ffn_ple"
        in_specs += ([pl.BlockSpec((None, tile, D_PLE), lambda i: (layer, sample_tile(i), 0),
                                   pipeline_mode=pl.Buffered(1)),
                      pl.BlockSpec((None, tile, D_PLE), lambda i: (layer, prompt_tile(i), 0))]
                     + [_resident(w) for w in ple_w] + [_resident(gf)])
        args += [ps, pp, *ple_w, gf]
    side = _SideCast(cast, cast_layer, np_) if cast else None
    any_spec = pl.BlockSpec(memory_space=pl.ANY)
    out_specs = [xs_spec, xp_spec] + [any_spec] * len(cast)
    out_shape = ([jax.ShapeDtypeStruct(xs.shape, f32), jax.ShapeDtypeStruct(xp.shape, f32)]
                 + [jax.ShapeDtypeStruct(a.shape[1:], bf16) for a in cast])
    scratch = [pltpu.VMEM((tile, D_FF), bf16)]
    if side:
        scratch += [pltpu.VMEM((r, s[1]), f32) for r, s in zip(side.rows, side.shapes)]
        scratch += [pltpu.VMEM((r, s[1]), bf16) for r, s in zip(side.rows, side.shapes)]
        scratch += [pltpu.SemaphoreType.DMA((len(cast),)), pltpu.SemaphoreType.DMA((len(cast),))]
    out = pl.pallas_call(
        functools.partial(_ffn_kernel, sample_steps=ns, layer=layer, ple=ple is not None,
                          final_norm=final_norm, side=side),
        grid=(ns + np_,),
        in_specs=in_specs + [any_spec] * len(cast),
        out_specs=out_specs,
        out_shape=out_shape,
        scratch_shapes=scratch,
        compiler_params=pltpu.CompilerParams(
            dimension_semantics=("arbitrary",), vmem_limit_bytes=V7X_VMEM_LIMIT_BYTES),
        name=name,
    )(*args, *cast)
    return out[0], out[1], list(out[2:])


def _mixer_prompt_kernel(x_ref, gm_ref, wmi_ref, wmo_ref, caw_ref, cab_ref, nag_ref, nab_ref,
                         pw_ref, ps_ref, sng_ref, snb_ref, sw_ref, sb_ref, scw_ref, avg_ref,
                         o_ref, sa_ref, sp_ref, ss_ref,
                         z_all, sh_all, c_all, d_all, y_all, *, layer):
    gm_ref, cab_ref, nag_ref, nab_ref, ps_ref, sng_ref, snb_ref = (
        _LayerRow(r, layer) for r in (gm_ref, cab_ref, nag_ref, nab_ref, ps_ref, sng_ref, snb_ref))
    tm, halo = MIX_TILE, MIX_HALO
    sub, half = V7X_SUBLANES, GROUP_W // 2
    step = pl.program_id(1)
    weights = (gm_ref, wmi_ref, wmo_ref, caw_ref, cab_ref, nag_ref, nab_ref, pw_ref, ps_ref,
               sng_ref, snb_ref, sw_ref, sb_ref, scw_ref, avg_ref)

    @pl.when(step == 0)
    def _():
        z_all[:, 0:halo, :] = jnp.zeros((MIX_SEQS, halo, MIX_IN_W), f32)
        sh_all[:, :, 0:halo, :] = jnp.zeros((MIX_SEQS, sub - 1, halo, 2 * GROUP_W), f32)

    for s in range(MIX_SEQS):
        h = _rms(x_ref[s], gm_ref[...]).astype(bf16)
        z_all[s, halo:halo + tm, :] = _dot(h, wmi_ref[...])
    for s in range(MIX_SEQS):
        _mixer_prompt_tile(step, x_ref.at[s], o_ref.at[s], weights, z_all.at[s], sh_all.at[s],
                           c_all.at[s], d_all.at[s], y_all.at[s])

    @pl.when(step == pl.num_programs(1) - 1)
    def _():
        for s in range(MIX_SEQS):
            z_scr = z_all.at[s]
            sa_ref[s] = z_scr[pl.ds(halo + tm - (CONV_A_WIDTH - 1), CONV_A_WIDTH - 1),
                              COL_A_VAL:COL_A_VAL + GROUP_W]
            sp_ref[s] = z_scr[pl.ds(halo + tm - POOL_STATE, POOL_STATE),
                              COL_ZB:COL_ZB + GROUP_W]
            ss_ref[s] = z_scr[pl.ds(halo + tm - (SHORT_CONV_WIDTH - 1), SHORT_CONV_WIDTH - 1),
                              COL_D_IN:COL_D_IN + GROUP_W]

    for s in range(MIX_SEQS):
        z_scr, sh_scr = z_all.at[s], sh_all.at[s]
        for col in (COL_A_VAL, COL_ZB, COL_D_IN):
            z_scr[0:halo, col:col + GROUP_W] = z_scr[tm:tm + halo, col:col + GROUP_W]
        sh_scr[0:3, 0:halo, :] = sh_scr[0:3, tm:tm + halo, :]
        for lanes in (slice(0, GROUP_W), slice(GROUP_W + half, 2 * GROUP_W)):
            sh_scr[3:sub - 1, 0:halo, lanes] = sh_scr[3:sub - 1, tm:tm + halo, lanes]


def _mixer_prompt_tile(step, x_ref, o_ref, weights, z_scr, sh_scr, c_scr, d_scr, y_scr):
    (gm_ref, wmi_ref, wmo_ref, caw_ref, cab_ref, nag_ref, nab_ref, pw_ref, ps_ref,
     sng_ref, snb_ref, sw_ref, sb_ref, scw_ref, avg_ref) = weights
    tm, halo, rb = MIX_TILE, MIX_HALO, MIX_ROWS
    sub, half = V7X_SUBLANES, GROUP_W // 2

    for r in range(0, tm, rb):
        rows = pl.ds(halo + r, rb)
        a_val = z_scr[rows, COL_A_VAL:COL_A_VAL + GROUP_W]
        a_gate = z_scr[rows, COL_A_GATE:COL_A_GATE + GROUP_W]
        z_scr[rows, COL_A_VAL:COL_A_VAL + GROUP_W] = a_val * jax.nn.sigmoid(a_gate)
        c_gate = z_scr[rows, COL_C_GATE:COL_C_GATE + GROUP_W]
        d_in = z_scr[rows, COL_D_IN:COL_D_IN + GROUP_W]
        z_scr[rows, COL_D_IN:COL_D_IN + GROUP_W] = c_gate * d_in
        for b in range(1, sub):
            late = pl.ds(halo + r - b, rb)
            sh_scr[b - 1, rows, 0:GROUP_W] = z_scr[late, COL_A_VAL:COL_A_VAL + GROUP_W]
            if b < 4:
                sh_scr[b - 1, rows, GROUP_W:2 * GROUP_W] = z_scr[late, COL_ZB:COL_ZB + GROUP_W]
            else:
                sh_scr[b - 1, rows, GROUP_W + half:2 * GROUP_W] = (
                    z_scr[late, COL_ZB + half:COL_ZB + GROUP_W])

    lane_group = lax.broadcasted_iota(jnp.int32, (rb, GROUP_W), 1) // HEAD_DIM
    win = jnp.left_shift(2, lane_group)
    row_iota = lax.broadcasted_iota(jnp.int32, (rb, GROUP_W), 0)
    first_head = lax.broadcasted_iota(jnp.int32, (rb, half), 1) < HEAD_DIM

    for r in range(0, tm, rb):
        def delayed(col, lanes, b, back):
            at = pl.ds(halo + r - back, rb)
            if b == 0:
                return z_scr[at, col + lanes.start:col + lanes.stop]
            base = 0 if col == COL_A_VAL else GROUP_W
            return sh_scr[b - 1, at, base + lanes.start:base + lanes.stop]

        acc = jnp.broadcast_to(cab_ref[...], (rb, GROUP_W))
        for j in range(CONV_A_WIDTH):
            k = CONV_A_WIDTH - 1 - j
            tap = delayed(COL_A_VAL, slice(0, GROUP_W), j % sub, j - j % sub)
            acc = acc + caw_ref[k:k + 1, :] * tap
        c_scr[r:r + rb, :] = acc

        lo, hi = slice(0, half), slice(half, GROUP_W)
        s2 = delayed(COL_ZB, lo, 0, 0) + delayed(COL_ZB, lo, 1, 0)
        s4 = s2 + delayed(COL_ZB, lo, 2, 0) + delayed(COL_ZB, lo, 3, 0)
        s8 = delayed(COL_ZB, hi, 0, 0)
        for b in range(1, sub):
            s8 = s8 + delayed(COL_ZB, hi, b, 0)
        s16 = s8
        for b in range(sub):
            s16 = s16 + delayed(COL_ZB, hi, b, sub)
        sums = jnp.concatenate(
            [jnp.where(first_head, s2, s4), jnp.where(first_head, s8, s16)], axis=1)
        zb = z_scr[pl.ds(halo + r, rb), COL_ZB:COL_ZB + GROUP_W]
        pos = step * tm + r + row_iota
        cnt = jnp.minimum(pos + 1, win).astype(f32)
        d_scr[r:r + rb, :] = (sums / cnt - zb).astype(bf16)

        rows = pl.ds(halo + r, rb)
        q0 = z_scr[pl.ds(halo + r - 2, rb), COL_D_IN:COL_D_IN + GROUP_W]
        q1 = z_scr[pl.ds(halo + r - 1, rb), COL_D_IN:COL_D_IN + GROUP_W]
        q2 = z_scr[rows, COL_D_IN:COL_D_IN + GROUP_W]
        conv = scw_ref[0:1, :] * q0 + scw_ref[1:2, :] * q1 + scw_ref[2:3, :] * q2
        b_gate = z_scr[rows, COL_B_GATE:COL_B_GATE + GROUP_W]
        y_scr[r:r + rb, 3 * GROUP_W:4 * GROUP_W] = (b_gate * conv).astype(bf16)

    avg = avg_ref[...]
    for r in range(0, tm, rb):
        ya = _head_layer_norm(c_scr[r:r + rb, :], nag_ref[...], nab_ref[...], avg)
        y_scr[r:r + rb, 0:GROUP_W] = _silu(ya).astype(bf16)
        y_scr[r:r + rb, GROUP_W:2 * GROUP_W] = (
            _dot(d_scr[r:r + rb, :], pw_ref[...]) * ps_ref[...]).astype(bf16)

    t_idx = lax.broadcasted_iota(jnp.int32, (CHUNK, N_HEADS * CHUNK), 0)
    s_idx = lax.broadcasted_iota(jnp.int32, (CHUNK, N_HEADS * CHUNK), 1) % CHUNK
    w_cat = jnp.where(s_idx <= t_idx, sw_ref[...], 0.0).astype(bf16)
    head_of_lane = lax.broadcasted_iota(jnp.int32, (CHUNK, GROUP_W), 1) // HEAD_DIM
    for c in range(tm // CHUNK):
        rows = pl.ds(halo + c * CHUNK, CHUNK)
        u = jax.nn.gelu(z_scr[rows, COL_U:COL_U + GROUP_W])
        v = jax.nn.gelu(z_scr[rows, COL_V:COL_V + GROUP_W])
        v = _head_layer_norm(v, sng_ref[...], snb_ref[...], avg).astype(bf16)
        v_stack = jnp.concatenate(
            [jnp.where(head_of_lane == hd, v, jnp.zeros_like(v)) for hd in range(N_HEADS)], axis=0)
        s = _dot(w_cat, v_stack) + sb_ref[...]
        y_scr[c * CHUNK:(c + 1) * CHUNK, 2 * GROUP_W:3 * GROUP_W] = (u * s).astype(bf16)

    o_ref[...] = x_ref[...] + _dot(y_scr[...], wmo_ref[...])


def _mixer_prompt_call(x, batch, seq, layer, wts, avg):
    tm, ns = MIX_TILE, MIX_SEQS
    x = x.reshape(batch, seq, D_MODEL)
    row_spec = pl.BlockSpec((ns, tm, D_MODEL), lambda b, l: (b, l, 0))

    def state_spec(rows):
        return pl.BlockSpec((ns, rows, GROUP_W), lambda b, l: (b, 0, 0))

    in_specs = [row_spec] + [_resident(w, layer) for w in wts] + [_resident(avg)]
    out = pl.pallas_call(
        functools.partial(_mixer_prompt_kernel, layer=layer),
        grid=(batch // ns, seq // tm),
        in_specs=in_specs,
        out_specs=[row_spec, state_spec(CONV_A_WIDTH - 1), state_spec(POOL_STATE),
                   state_spec(SHORT_CONV_WIDTH - 1)],
        out_shape=[jax.ShapeDtypeStruct((batch, seq, D_MODEL), f32),
                   jax.ShapeDtypeStruct((batch, CONV_A_WIDTH - 1, GROUP_W), f32),
                   jax.ShapeDtypeStruct((batch, POOL_STATE, GROUP_W), f32),
                   jax.ShapeDtypeStruct((batch, SHORT_CONV_WIDTH - 1, GROUP_W), f32)],
        scratch_shapes=[pltpu.VMEM((ns, MIX_HALO + tm, MIX_IN_W), f32),
                        pltpu.VMEM((ns, V7X_SUBLANES - 1, MIX_HALO + tm, 2 * GROUP_W), f32),
                        pltpu.VMEM((ns, tm, GROUP_W), f32),
                        pltpu.VMEM((ns, tm, GROUP_W), bf16),
                        pltpu.VMEM((ns, tm, D_MODEL), bf16)],
        compiler_params=pltpu.CompilerParams(
            dimension_semantics=("arbitrary", "arbitrary"),
            vmem_limit_bytes=V7X_VMEM_LIMIT_BYTES),
        name="mixer_prompt",
    )(x, *wts, avg)
    return (out[0].reshape(batch * seq, D_MODEL),) + tuple(out[1:])


def _mixer_sample_kernel(x_ref, gm_ref, wmi_ref, wmo_ref, caw_ref, cab_ref, nag_ref, nab_ref,
                         pw_ref, ps_ref, sng_ref, snb_ref, coef_ref, sb_ref, scw_ref, avg_ref,
                         sta_ref, stp_ref, sts_ref,
                         o_ref, sa_ref, sp_ref, ss_ref, v_ref,
                         z_scr, xa, xp, xs, c_scr, y_scr, *, dec_seq, layer):
    gm_ref, cab_ref, nag_ref, nab_ref, ps_ref, sng_ref, snb_ref = (
        _LayerRow(r, layer) for r in (gm_ref, cab_ref, nag_ref, nab_ref, ps_ref, sng_ref, snb_ref))
    nb, steps = SAMPLE_SEQS, dec_seq
    rows = nb * steps
    hist_a, hist_p, hist_s = CONV_A_WIDTH - 1, POOL_STATE, SHORT_CONV_WIDTH - 1
    avg = avg_ref[...]

    def slabs(col):
        return z_scr[:, col:col + GROUP_W].reshape(steps, nb, GROUP_W)

    x = x_ref[...].reshape(rows, D_MODEL)
    z_scr[...] = _dot(_rms(x, gm_ref[...]).astype(bf16), wmi_ref[...])

    xa[0:hist_a] = sta_ref[...]
    xa[hist_a:hist_a + steps] = slabs(COL_A_VAL) * jax.nn.sigmoid(slabs(COL_A_GATE))
    sa_ref[...] = xa[steps:steps + hist_a]
    for t in range(steps):
        acc = jnp.broadcast_to(cab_ref[...], (nb, GROUP_W))
        for k in range(CONV_A_WIDTH):
            acc = acc + caw_ref[k:k + 1, :] * xa[t + k]
        c_scr[t] = acc
    ya = _head_layer_norm(c_scr[...].reshape(rows, GROUP_W), nag_ref[...], nab_ref[...], avg)
    y_scr[:, 0:GROUP_W] = _silu(ya).astype(bf16)

    xp[0:hist_p] = stp_ref[...]
    xp[hist_p:hist_p + steps] = slabs(COL_ZB)
    sp_ref[...] = xp[steps:steps + hist_p]
    lane_group = lax.broadcasted_iota(jnp.int32, (nb, GROUP_W), 1) // HEAD_DIM
    win = jnp.left_shift(2, lane_group)
    for t in range(steps):
        zb = xp[hist_p + t]
        run = zb
        sums = []
        for j in range(1, max(POOL_WINDOWS)):
            run = run + xp[hist_p + t - j]
            if j + 1 in POOL_WINDOWS:
                sums.append(run)
        cnt = jnp.minimum(PAST_LEN + t + 1, win).astype(f32)
        c_scr[t] = _pool_select(sums, lane_group) / cnt - zb
    d = c_scr[...].reshape(rows, GROUP_W).astype(bf16)
    y_scr[:, GROUP_W:2 * GROUP_W] = (_dot(d, pw_ref[...]) * ps_ref[...]).astype(bf16)

    v = _head_layer_norm(jax.nn.gelu(z_scr[:, COL_V:COL_V + GROUP_W]),
                         sng_ref[...], snb_ref[...], avg)
    v_ref[...] = v.reshape(steps, nb, GROUP_W)
    for t in range(steps):
        s = jnp.broadcast_to(sb_ref[t:t + 1, :], (nb, GROUP_W))
        for j in range(t + 1):
            s = s + coef_ref[j, t:t + 1, :] * v_ref[j]
        u = jax.nn.gelu(z_scr[t * nb:(t + 1) * nb, COL_U:COL_U + GROUP_W])
        y_scr[t * nb:(t + 1) * nb, 2 * GROUP_W:3 * GROUP_W] = (u * s).astype(bf16)

    xs[0:hist_s] = sts_ref[...]
    xs[hist_s:hist_s + steps] = slabs(COL_C_GATE) * slabs(COL_D_IN)
    ss_ref[...] = xs[steps:steps + hist_s]
    for t in range(steps):
        conv = scw_ref[0:1, :] * xs[t]
        for k in range(1, SHORT_CONV_WIDTH):
            conv = conv + scw_ref[k:k + 1, :] * xs[t + k]
        b_gate = z_scr[t * nb:(t + 1) * nb, COL_B_GATE:COL_B_GATE + GROUP_W]
        y_scr[t * nb:(t + 1) * nb, 3 * GROUP_W:4 * GROUP_W] = (b_gate * conv).astype(bf16)

    out = x_ref[...].reshape(rows, D_MODEL) + _dot(y_scr[...], wmo_ref[...])
    o_ref[...] = out.reshape(steps, nb, D_MODEL)


def _mixer_sample_call(x, st_a, st_p, st_s, dec_batch, dec_seq, layer, wts, avg):
    nb = SAMPLE_SEQS
    rows = nb * dec_seq
    row_spec = pl.BlockSpec((dec_seq, nb, D_MODEL), lambda i: (0, i, 0))

    def state_spec(r):
        return pl.BlockSpec((r, nb, GROUP_W), lambda i: (0, i, 0))

    def state_in_spec(r):
        return pl.BlockSpec((None, r, nb, GROUP_W), lambda i: (layer, 0, i, 0))

    def state_shape(r):
        return jax.ShapeDtypeStruct((r, dec_batch, GROUP_W), f32)

    hist_a, hist_p, hist_s = CONV_A_WIDTH - 1, POOL_STATE, SHORT_CONV_WIDTH - 1
    in_specs = ([row_spec] + [_resident(w, layer) for w in wts] + [_resident(avg)]
                + [state_in_spec(hist_a), state_in_spec(hist_p), state_in_spec(hist_s)])
    return pl.pallas_call(
        functools.partial(_mixer_sample_kernel, dec_seq=dec_seq, layer=layer),
        grid=(dec_batch // nb,),
        in_specs=in_specs,
        out_specs=[row_spec, state_spec(hist_a), state_spec(hist_p), state_spec(hist_s),
                   state_spec(dec_seq)],
        out_shape=[jax.ShapeDtypeStruct((dec_seq, dec_batch, D_MODEL), f32),
                   state_shape(hist_a), state_shape(hist_p), state_shape(hist_s),
                   state_shape(dec_seq)],
        scratch_shapes=[pltpu.VMEM((rows, MIX_IN_W), f32),
                        pltpu.VMEM((hist_a + dec_seq, nb, GROUP_W), f32),
                        pltpu.VMEM((hist_p + dec_seq, nb, GROUP_W), f32),
                        pltpu.VMEM((hist_s + dec_seq, nb, GROUP_W), f32),
                        pltpu.VMEM((dec_seq, nb, GROUP_W), f32),
                        pltpu.VMEM((rows, D_MODEL), bf16)],
        compiler_params=pltpu.CompilerParams(
            dimension_semantics=("arbitrary",), vmem_limit_bytes=V7X_VMEM_LIMIT_BYTES),
        name="mixer_sample",
    )(x, *wts, avg, st_a, st_p, st_s)


def _block_diag(w):
    n, g, c, d = w.shape
    eye = jnp.eye(g, dtype=w.dtype)
    return (eye[None, :, None, :, None] * w[:, :, :, None, :]).reshape(n, g * c, g * d)


def kernel(x_prompt, x_sample, p_prompt, p_sample, state_conv_a, state_pool, state_short_conv,
           norm_ffn1, w_ffn1_gate, w_ffn1_up, w_ffn1_down, norm_mix, w_mix_in,
           conv_a_w, conv_a_b, norm_a_g, norm_a_b, pool_w, pool_scale,
           sgu_norm_g, sgu_norm_b, sgu_w, sgu_b, short_conv_w, w_mix_out,
           norm_ffn2, w_ffn2_gate, w_ffn2_up, w_ffn2_down,
           norm_ple, w_ple_gate, w_ple_proj, norm_final):
    depth = w_mix_in.shape[0]
    batch, seq, _ = x_prompt.shape
    dec_batch, dec_seq, _ = x_sample.shape
    assert seq % MIX_TILE == 0 and MIX_TILE % CHUNK == 0 and dec_seq <= CHUNK
    assert batch % MIX_SEQS == 0
    assert dec_batch % SAMPLE_SEQS == 0 and dec_seq == V7X_SUBLANES
    assert (batch * seq) % FFN_TILE == 0 and (dec_batch * dec_seq) % FFN_TILE == 0
    assert all(c % 256 == 0 for c in FFN_COL_SPLITS)

    avg = _block_diag(jnp.full((1, N_HEADS, HEAD_DIM, HEAD_DIM), 1.0 / HEAD_DIM, f32))[0]
    avg = avg.astype(bf16)

    ffn1_mats = [w[0].astype(bf16) for w in (w_ffn1_gate, w_ffn1_up, w_ffn1_down)]
    cast_in_ffn1 = (w_ffn2_gate, w_ffn2_up, w_ffn2_down, w_ple_gate, w_ple_proj,
                    w_mix_in, w_mix_out)
    cast_in_ffn2 = (w_ffn1_gate, w_ffn1_up, w_ffn1_down)
    gf = norm_final.reshape(1, D_MODEL)
    mix_tail = (conv_a_w, conv_a_b, norm_a_g, norm_a_b,
                _block_diag(pool_w).astype(bf16), pool_scale,
                sgu_norm_g, sgu_norm_b)
    sw_prompt = jnp.transpose(sgu_w, (0, 2, 1, 3)).reshape(depth, CHUNK, N_HEADS * CHUNK)
    sb_prompt = jnp.repeat(jnp.swapaxes(sgu_b, 1, 2), HEAD_DIM, axis=2)
    sw_sample = jnp.repeat(jnp.transpose(sgu_w[:, :, :dec_seq, :dec_seq], (0, 3, 2, 1)),
                           HEAD_DIM, axis=3)
    sb_sample = sb_prompt[:, :dec_seq]

    sample_rows = dec_seq * dec_batch
    pp = p_prompt.reshape(depth, batch * seq, D_PLE)
    ps = jnp.swapaxes(p_sample, 1, 2).reshape(depth, sample_rows, D_PLE)
    xp = x_prompt.reshape(batch * seq, D_MODEL)
    xs = jnp.swapaxes(x_sample, 0, 1).reshape(sample_rows, D_MODEL)
    st_a, st_p, st_s = (jnp.swapaxes(st, 1, 2)
                        for st in (state_conv_a, state_pool, state_short_conv))
    prompt_states, sample_states = [], []
    for i in range(depth):
        last = i == depth - 1

        xs, xp, mats = _ffn_call(xs, xp, i, (norm_ffn1, *ffn1_mats),
                                 cast=cast_in_ffn1, cast_layer=i)
        ffn2_mats, ple_mats, mix_mats = mats[0:3], mats[3:5], mats[5:7]
        mix_head = (norm_mix, *mix_mats) + mix_tail
        xp, sa, sp, ss = _mixer_prompt_call(
            xp, batch, seq, i, mix_head + (sw_prompt, sb_prompt, short_conv_w), avg)
        prompt_states.append((sa, sp, ss))
        xs, sa, sp, ss, sv = _mixer_sample_call(
            xs.reshape(dec_seq, dec_batch, D_MODEL), st_a, st_p, st_s, dec_batch, dec_seq, i,
            mix_head + (sw_sample, sb_sample, short_conv_w), avg)
        xs = xs.reshape(sample_rows, D_MODEL)
        sample_states.append((sa, sp, ss, sv))
        xs, xp, ffn1_mats = _ffn_call(
            xs, xp, i, (norm_ffn2, *ffn2_mats), ple=(ps, pp, (norm_ple, *ple_mats), gf),
            final_norm=last, cast=() if last else cast_in_ffn2, cast_layer=i + 1)

    def stack(states, j):
        return jnp.stack([s[j] for s in states])

    def stack_sample(j):
        return jnp.swapaxes(stack(sample_states, j), 1, 2)

    y_sample = jnp.swapaxes(xs.reshape(dec_seq, dec_batch, D_MODEL), 0, 1)
    return (xp.reshape(batch, seq, D_MODEL), y_sample,
            stack(prompt_states, 0), stack_sample(0),
            stack(prompt_states, 1), stack_sample(1),
            stack(prompt_states, 2), stack_sample(2),
            stack_sample(3))
```

```python
import functools

import jax
import jax.numpy as jnp
from jax import lax
from jax.experimental import pallas as pl
from jax.experimental.pallas import tpu as pltpu

D_MODEL = 1024
D_PLE = 256
D_FF = 2816
GROUP_W = 256
N_HEADS = 4
HEAD_DIM = 64
MIX_IN_W = 8 * GROUP_W
CONV_A_WIDTH = 31
POOL_WINDOWS = (2, 4, 8, 16)
POOL_STATE = 15
CHUNK = 128
SHORT_CONV_WIDTH = 3
EPS = 1e-6
PAST_LEN = 16384

V7X_SUBLANES = 8
V7X_BF16_ROWS = 16
V7X_VMEM_BYTES = 64 * 1024 * 1024
V7X_VMEM_LIMIT_BYTES = V7X_VMEM_BYTES - 4 * 1024 * 1024

FFN_TILE = 1024
FFN_SUB_TILE = 128
FFN_COL_SPLITS = (0, 1536, D_FF)
MIX_TILE = 512
MIX_SEQS = 2
MIX_HALO = 32
MIX_ROWS = 128
MIX_DOT_ROWS = 512
SAMPLE_SEQS = 64

COL_A_VAL, COL_A_GATE, COL_ZB, COL_U, COL_V, COL_B_GATE, COL_C_GATE, COL_D_IN = (
    i * GROUP_W for i in range(8))

bf16 = jnp.bfloat16
f32 = jnp.float32


def _dot(a, b):
    return jnp.dot(a, b, preferred_element_type=f32)


def _rms(x, g):
    ms = jnp.mean(x * x, axis=-1, keepdims=True)
    return x * lax.rsqrt(ms + EPS) * g


def _group_mean(x, avg):
    hi = x.astype(bf16)
    lo = (x - hi.astype(f32)).astype(bf16)
    return _dot(hi, avg) + _dot(lo, avg)


def _head_layer_norm(x, g, b, avg):
    mu = _group_mean(x, avg)
    xc = x - mu
    var = _dot((xc * xc).astype(bf16), avg)
    return xc * lax.rsqrt(var + EPS) * g + b


def _silu(x):
    return x * jax.nn.sigmoid(x)


def _pool_select(sums, lane_group):
    s2, s4, s8, s16 = sums
    return jnp.where(lane_group == 0, s2,
                     jnp.where(lane_group == 1, s4, jnp.where(lane_group == 2, s8, s16)))


def _ffn_rows(x_ref, rows, g_ref, wg_ref, wu_ref, wd_ref, act_ref):
    h = _rms(x_ref[rows, :], g_ref[...]).astype(bf16)
    for lo, hi in zip(FFN_COL_SPLITS[:-1], FFN_COL_SPLITS[1:]):
        cs = slice(lo, hi)
        gate = _dot(h, wg_ref[:, cs])
        up = _dot(h, wu_ref[:, cs])
        act_ref[rows, cs] = (_silu(gate) * up).astype(bf16)
    return x_ref[rows, :] + 0.5 * _dot(act_ref[rows, :], wd_ref[...])


def _sub_tiles():
    return [pl.ds(r, FFN_SUB_TILE) for r in range(0, FFN_TILE, FFN_SUB_TILE)]


class _SideCast:
    def __init__(self, arrays, layer, chunks):
        self.layer, self.chunks = layer, chunks
        self.shapes = [a.shape[1:] for a in arrays]
        assert all(r % (chunks * V7X_BF16_ROWS) == 0 for r, _ in self.shapes)
        self.rows = [r // chunks for r, _ in self.shapes]

    def in_copy(self, a, j, srcs, bufs, sem):
        start = pl.multiple_of(j * self.rows[a], V7X_BF16_ROWS)
        return pltpu.make_async_copy(
            srcs[a].at[self.layer, pl.ds(start, self.rows[a]), :], bufs[a], sem.at[a])

    def out_copy(self, a, j, bufs, dsts, sem):
        start = pl.multiple_of(j * self.rows[a], V7X_BF16_ROWS)
        return pltpu.make_async_copy(
            bufs[a], dsts[a].at[pl.ds(start, self.rows[a]), :], sem.at[a])


def _ffn_kernel(*refs, sample_steps, layer, ple, final_norm, side):
    refs = list(refs)
    take = lambda n: [refs.pop(0) for _ in range(n)]
    xs_ref, xp_ref, g_ref, wg_ref, wu_ref, wd_ref = take(6)
    if ple:
        ps_ref, pp_ref, gp_ref, wpg_ref, wpp_ref, gf_ref = take(6)
        gp_ref = _LayerRow(gp_ref, layer)
    n_side = len(side.shapes) if side else 0
    srcs = take(n_side)
    os_ref, op_ref = take(2)
    dsts = take(n_side)
    (act_ref,) = take(1)
    in_bufs, out_bufs = take(n_side), take(n_side)
    in_sem, out_sem = take(2) if side else (None, None)
    g_ref = _LayerRow(g_ref, layer)

    step = pl.program_id(0)
    is_sample = step < sample_steps
    chunk = step - sample_steps
    has_chunk = jnp.logical_and(chunk >= 0, chunk < side.chunks) if side else None

    if side:
        @pl.when(has_chunk)
        def _():
            for a in range(n_side):
                side.in_copy(a, chunk, srcs, in_bufs, in_sem).start()

    for rows in _sub_tiles():
        op_ref[rows, :] = jnp.where(is_sample, xs_ref[rows, :], xp_ref[rows, :])
        op_ref[rows, :] = _ffn_rows(op_ref, rows, g_ref, wg_ref, wu_ref, wd_ref, act_ref)
    for rows in _sub_tiles():
        if ple:
            p = jnp.where(is_sample, ps_ref[rows, :], pp_ref[rows, :]).astype(bf16)
            gate = jax.nn.sigmoid(
                _dot(_rms(op_ref[rows, :], gp_ref[...]).astype(bf16), wpg_ref[...]))
            x = op_ref[rows, :] + gate * _dot(p, wpp_ref[...])
            if final_norm:
                x = _rms(x, gf_ref[...])
            op_ref[rows, :] = x

    @pl.when(is_sample)
    def _():
        os_ref[...] = op_ref[...]

    if side:
        @pl.when(jnp.logical_and(chunk >= 1, chunk < side.chunks))
        def _():
            for a in range(n_side):
                side.out_copy(a, chunk - 1, out_bufs, dsts, out_sem).wait()

        @pl.when(has_chunk)
        def _():
            for a in range(n_side):
                side.in_copy(a, chunk, srcs, in_bufs, in_sem).wait()
                out_bufs[a][...] = in_bufs[a][...].astype(bf16)
                side.out_copy(a, chunk, out_bufs, dsts, out_sem).start()

        @pl.when(chunk == side.chunks - 1)
        def _():
            for a in range(n_side):
                side.out_copy(a, chunk, out_bufs, dsts, out_sem).wait()


class _LayerRow:
    def __init__(self, ref, layer):
        self.ref, self.layer = ref, layer

    def __getitem__(self, idx):
        assert idx is Ellipsis
        return self.ref[self.layer:self.layer + 1, :]


def _resident(w, layer=None):
    if layer is None or w.ndim == 2:
        return pl.BlockSpec(w.shape, lambda *_: (0,) * w.ndim, pipeline_mode=pl.Buffered(1))
    return pl.BlockSpec((None,) + w.shape[1:], lambda *_: (layer,) + (0,) * (w.ndim - 1),
                        pipeline_mode=pl.Buffered(1))


def _ffn_call(xs, xp, layer, ffn_w, ple=None, final_norm=False, cast=(), cast_layer=0):
    tile = FFN_TILE
    ns, np_ = xs.shape[0] // tile, xp.shape[0] // tile

    def sample_tile(i):
        return jnp.minimum(i, ns - 1)

    def prompt_tile(i):
        return jnp.maximum(i - ns, 0)

    xs_spec = pl.BlockSpec((tile, D_MODEL), lambda i: (sample_tile(i), 0),
                           pipeline_mode=pl.Buffered(1))
    xp_spec = pl.BlockSpec((tile, D_MODEL), lambda i: (prompt_tile(i), 0))
    in_specs = [xs_spec, xp_spec] + [_resident(w) for w in ffn_w]
    args = [xs, xp, *ffn_w]
    name = "ffn"
    if ple is not None:
        ps, pp, ple_w, gf = ple
        name = "ffn_ple"
        in_specs += ([pl.BlockSpec((None, tile, D_PLE), lambda i: (layer, sample_tile(i), 0),
                                   pipeline_mode=pl.Buffered(1)),
                      pl.BlockSpec((None, tile, D_PLE), lambda i: (layer, prompt_tile(i), 0))]
                     + [_resident(w) for w in ple_w] + [_resident(gf)])
        args += [ps, pp, *ple_w, gf]
    side = _SideCast(cast, cast_layer, np_) if cast else None
    any_spec = pl.BlockSpec(memory_space=pl.ANY)
    out_specs = [xs_spec, xp_spec] + [any_spec] * len(cast)
    out_shape = ([jax.ShapeDtypeStruct(xs.shape, f32), jax.ShapeDtypeStruct(xp.shape, f32)]
                 + [jax.ShapeDtypeStruct(a.shape[1:], bf16) for a in cast])
    scratch = [pltpu.VMEM((tile, D_FF), bf16)]
    if side:
        scratch += [pltpu.VMEM((r, s[1]), f32) for r, s in zip(side.rows, side.shapes)]
        scratch += [pltpu.VMEM((r, s[1]), bf16) for r, s in zip(side.rows, side.shapes)]
        scratch += [pltpu.SemaphoreType.DMA((len(cast),)), pltpu.SemaphoreType.DMA((len(cast),))]
    out = pl.pallas_call(
        functools.partial(_ffn_kernel, sample_steps=ns, layer=layer, ple=ple is not None,
                          final_norm=final_norm, side=side),
        grid=(ns + np_,),
        in_specs=in_specs + [any_spec] * len(cast),
        out_specs=out_specs,
        out_shape=out_shape,
        scratch_shapes=scratch,
        compiler_params=pltpu.CompilerParams(
            dimension_semantics=("arbitrary",), vmem_limit_bytes=V7X_VMEM_LIMIT_BYTES),
        name=name,
    )(*args, *cast)
    return out[0], out[1], list(out[2:])


def _mixer_prompt_kernel(x_ref, gm_ref, wmi_ref, wmo_ref, caw_ref, cab_ref, nag_ref, nab_ref,
                         pw_ref, ps_ref, sng_ref, snb_ref, sw_ref, sb_ref, scw_ref, avg_ref,
                         o_ref, sa_ref, sp_ref, ss_ref,
                         z_all, sh_all, c_all, d_all, y_all, *, layer):
    gm_ref, cab_ref, nag_ref, nab_ref, ps_ref, sng_ref, snb_ref = (
        _LayerRow(r, layer) for r in (gm_ref, cab_ref, nag_ref, nab_ref, ps_ref, sng_ref, snb_ref))
    tm, halo = MIX_TILE, MIX_HALO
    sub, half = V7X_SUBLANES, GROUP_W // 2
    step = pl.program_id(1)
    weights = (gm_ref, wmi_ref, wmo_ref, caw_ref, cab_ref, nag_ref, nab_ref, pw_ref, ps_ref,
               sng_ref, snb_ref, sw_ref, sb_ref, scw_ref, avg_ref)

    @pl.when(step == 0)
    def _():
        z_all[:, 0:halo, :] = jnp.zeros((MIX_SEQS, halo, MIX_IN_W), f32)
        sh_all[:, :, 0:halo, :] = jnp.zeros((MIX_SEQS, sub - 1, halo, 2 * GROUP_W), f32)

    for s in range(MIX_SEQS):
        for r in range(0, tm, MIX_DOT_ROWS):
            h = _rms(x_ref[s, r:r + MIX_DOT_ROWS, :], gm_ref[...]).astype(bf16)
            z_all[s, halo + r:halo + r + MIX_DOT_ROWS, :] = _dot(h, wmi_ref[...])
    for s in range(MIX_SEQS):
        _mixer_prompt_tile(step, x_ref.at[s], o_ref.at[s], weights, z_all.at[s], sh_all.at[s],
                           c_all.at[s], d_all.at[s], y_all.at[s])

    @pl.when(step == pl.num_programs(1) - 1)
    def _():
        for s in range(MIX_SEQS):
            z_scr = z_all.at[s]
            sa_ref[s] = z_scr[pl.ds(halo + tm - (CONV_A_WIDTH - 1), CONV_A_WIDTH - 1),
                              COL_A_VAL:COL_A_VAL + GROUP_W]
            sp_ref[s] = z_scr[pl.ds(halo + tm - POOL_STATE, POOL_STATE),
                              COL_ZB:COL_ZB + GROUP_W]
            ss_ref[s] = z_scr[pl.ds(halo + tm - (SHORT_CONV_WIDTH - 1), SHORT_CONV_WIDTH - 1),
                              COL_D_IN:COL_D_IN + GROUP_W]

    for s in range(MIX_SEQS):
        z_scr, sh_scr = z_all.at[s], sh_all.at[s]
        for col in (COL_A_VAL, COL_ZB, COL_D_IN):
            z_scr[0:halo, col:col + GROUP_W] = z_scr[tm:tm + halo, col:col + GROUP_W]
        sh_scr[0:3, 0:halo, :] = sh_scr[0:3, tm:tm + halo, :]
        for lanes in (slice(0, GROUP_W), slice(GROUP_W + half, 2 * GROUP_W)):
            sh_scr[3:sub - 1, 0:halo, lanes] = sh_scr[3:sub - 1, tm:tm + halo, lanes]


def _mixer_prompt_tile(step, x_ref, o_ref, weights, z_scr, sh_scr, c_scr, d_scr, y_scr):
    (gm_ref, wmi_ref, wmo_ref, caw_ref, cab_ref, nag_ref, nab_ref, pw_ref, ps_ref,
     sng_ref, snb_ref, sw_ref, sb_ref, scw_ref, avg_ref) = weights
    tm, halo, rb = MIX_TILE, MIX_HALO, MIX_ROWS
    sub, half = V7X_SUBLANES, GROUP_W // 2

    for r in range(0, tm, rb):
        rows = pl.ds(halo + r, rb)
        a_val = z_scr[rows, COL_A_VAL:COL_A_VAL + GROUP_W]
        a_gate = z_scr[rows, COL_A_GATE:COL_A_GATE + GROUP_W]
        z_scr[rows, COL_A_VAL:COL_A_VAL + GROUP_W] = a_val * jax.nn.sigmoid(a_gate)
        c_gate = z_scr[rows, COL_C_GATE:COL_C_GATE + GROUP_W]
        d_in = z_scr[rows, COL_D_IN:COL_D_IN + GROUP_W]
        z_scr[rows, COL_D_IN:COL_D_IN + GROUP_W] = c_gate * d_in
        for b in range(1, sub):
            late = pl.ds(halo + r - b, rb)
            sh_scr[b - 1, rows, 0:GROUP_W] = z_scr[late, COL_A_VAL:COL_A_VAL + GROUP_W]
            if b < 4:
                sh_scr[b - 1, rows, GROUP_W:2 * GROUP_W] = z_scr[late, COL_ZB:COL_ZB + GROUP_W]
            else:
                sh_scr[b - 1, rows, GROUP_W + half:2 * GROUP_W] = (
                    z_scr[late, COL_ZB + half:COL_ZB + GROUP_W])

    lane_group = lax.broadcasted_iota(jnp.int32, (rb, GROUP_W), 1) // HEAD_DIM
    win = jnp.left_shift(2, lane_group)
    row_iota = lax.broadcasted_iota(jnp.int32, (rb, GROUP_W), 0)
    first_head = lax.broadcasted_iota(jnp.int32, (rb, half), 1) < HEAD_DIM

    for r in range(0, tm, rb):
        def delayed(col, lanes, b, back):
            at = pl.ds(halo + r - back, rb)
            if b == 0:
                return z_scr[at, col + lanes.start:col + lanes.stop]
            base = 0 if col == COL_A_VAL else GROUP_W
            return sh_scr[b - 1, at, base + lanes.start:base + lanes.stop]

        acc = jnp.broadcast_to(cab_ref[...], (rb, GROUP_W))
        for j in range(CONV_A_WIDTH):
            k = CONV_A_WIDTH - 1 - j
            tap = delayed(COL_A_VAL, slice(0, GROUP_W), j % sub, j - j % sub)
            acc = acc + caw_ref[k:k + 1, :] * tap
        c_scr[r:r + rb, :] = acc

        lo, hi = slice(0, half), slice(half, GROUP_W)
        s2 = delayed(COL_ZB, lo, 0, 0) + delayed(COL_ZB, lo, 1, 0)
        s4 = s2 + delayed(COL_ZB, lo, 2, 0) + delayed(COL_ZB, lo, 3, 0)
        s8 = delayed(COL_ZB, hi, 0, 0)
        for b in range(1, sub):
            s8 = s8 + delayed(COL_ZB, hi, b, 0)
        s16 = s8
        for b in range(sub):
            s16 = s16 + delayed(COL_ZB, hi, b, sub)
        sums = jnp.concatenate(
            [jnp.where(first_head, s2, s4), jnp.where(first_head, s8, s16)], axis=1)
        zb = z_scr[pl.ds(halo + r, rb), COL_ZB:COL_ZB + GROUP_W]
        pos = step * tm + r + row_iota
        cnt = jnp.minimum(pos + 1, win).astype(f32)
        d_scr[r:r + rb, :] = (sums / cnt - zb).astype(bf16)

        rows = pl.ds(halo + r, rb)
        q0 = z_scr[pl.ds(halo + r - 2, rb), COL_D_IN:COL_D_IN + GROUP_W]
        q1 = z_scr[pl.ds(halo + r - 1, rb), COL_D_IN:COL_D_IN + GROUP_W]
        q2 = z_scr[rows, COL_D_IN:COL_D_IN + GROUP_W]
        conv = scw_ref[0:1, :] * q0 + scw_ref[1:2, :] * q1 + scw_ref[2:3, :] * q2
        b_gate = z_scr[rows, COL_B_GATE:COL_B_GATE + GROUP_W]
        y_scr[r:r + rb, 3 * GROUP_W:4 * GROUP_W] = (b_gate * conv).astype(bf16)

    avg = avg_ref[...]
    for r in range(0, tm, rb):
        ya = _head_layer_norm(c_scr[r:r + rb, :], nag_ref[...], nab_ref[...], avg)
        y_scr[r:r + rb, 0:GROUP_W] = _silu(ya).astype(bf16)
        y_scr[r:r + rb, GROUP_W:2 * GROUP_W] = (
            _dot(d_scr[r:r + rb, :], pw_ref[...]) * ps_ref[...]).astype(bf16)

    t_idx = lax.broadcasted_iota(jnp.int32, (CHUNK, N_HEADS * CHUNK), 0)
    s_idx = lax.broadcasted_iota(jnp.int32, (CHUNK, N_HEADS * CHUNK), 1) % CHUNK
    w_cat = jnp.where(s_idx <= t_idx, sw_ref[...], 0.0).astype(bf16)
    head_of_lane = lax.broadcasted_iota(jnp.int32, (CHUNK, GROUP_W), 1) // HEAD_DIM
    for c in range(tm // CHUNK):
        rows = pl.ds(halo + c * CHUNK, CHUNK)
        u = jax.nn.gelu(z_scr[rows, COL_U:COL_U + GROUP_W])
        v = jax.nn.gelu(z_scr[rows, COL_V:COL_V + GROUP_W])
        v = _head_layer_norm(v, sng_ref[...], snb_ref[...], avg).astype(bf16)
        v_stack = jnp.concatenate(
            [jnp.where(head_of_lane == hd, v, jnp.zeros_like(v)) for hd in range(N_HEADS)], axis=0)
        s = _dot(w_cat, v_stack) + sb_ref[...]
        y_scr[c * CHUNK:(c + 1) * CHUNK, 2 * GROUP_W:3 * GROUP_W] = (u * s).astype(bf16)

    for r in range(0, tm, MIX_DOT_ROWS):
        rows = slice(r, r + MIX_DOT_ROWS)
        o_ref[rows, :] = x_ref[rows, :] + _dot(y_scr[rows, :], wmo_ref[...])


def _mixer_prompt_call(x, batch, seq, layer, wts, avg):
    tm, ns = MIX_TILE, MIX_SEQS
    x = x.reshape(batch, seq, D_MODEL)
    row_spec = pl.BlockSpec((ns, tm, D_MODEL), lambda b, l: (b, l, 0))

    def state_spec(rows):
        return pl.BlockSpec((ns, rows, GROUP_W), lambda b, l: (b, 0, 0))

    in_specs = [row_spec] + [_resident(w, layer) for w in wts] + [_resident(avg)]
    out = pl.pallas_call(
        functools.partial(_mixer_prompt_kernel, layer=layer),
        grid=(batch // ns, seq // tm),
        in_specs=in_specs,
        out_specs=[row_spec, state_spec(CONV_A_WIDTH - 1), state_spec(POOL_STATE),
                   state_spec(SHORT_CONV_WIDTH - 1)],
        out_shape=[jax.ShapeDtypeStruct((batch, seq, D_MODEL), f32),
                   jax.ShapeDtypeStruct((batch, CONV_A_WIDTH - 1, GROUP_W), f32),
                   jax.ShapeDtypeStruct((batch, POOL_STATE, GROUP_W), f32),
                   jax.ShapeDtypeStruct((batch, SHORT_CONV_WIDTH - 1, GROUP_W), f32)],
        scratch_shapes=[pltpu.VMEM((ns, MIX_HALO + tm, MIX_IN_W), f32),
                        pltpu.VMEM((ns, V7X_SUBLANES - 1, MIX_HALO + tm, 2 * GROUP_W), f32),
                        pltpu.VMEM((ns, tm, GROUP_W), f32),
                        pltpu.VMEM((ns, tm, GROUP_W), bf16),
                        pltpu.VMEM((ns, tm, D_MODEL), bf16)],
        compiler_params=pltpu.CompilerParams(
            dimension_semantics=("arbitrary", "arbitrary"),
            vmem_limit_bytes=V7X_VMEM_LIMIT_BYTES),
        name="mixer_prompt",
    )(x, *wts, avg)
    return (out[0].reshape(batch * seq, D_MODEL),) + tuple(out[1:])


def _mixer_sample_kernel(x_ref, gm_ref, wmi_ref, wmo_ref, caw_ref, cab_ref, nag_ref, nab_ref,
                         pw_ref, ps_ref, sng_ref, snb_ref, coef_ref, sb_ref, scw_ref, avg_ref,
                         sta_ref, stp_ref, sts_ref,
                         o_ref, sa_ref, sp_ref, ss_ref, v_ref,
                         z_scr, xa, xp, xs, c_scr, y_scr, *, dec_seq, layer):
    gm_ref, cab_ref, nag_ref, nab_ref, ps_ref, sng_ref, snb_ref = (
        _LayerRow(r, layer) for r in (gm_ref, cab_ref, nag_ref, nab_ref, ps_ref, sng_ref, snb_ref))
    nb, steps = SAMPLE_SEQS, dec_seq
    rows = nb * steps
    hist_a, hist_p, hist_s = CONV_A_WIDTH - 1, POOL_STATE, SHORT_CONV_WIDTH - 1
    avg = avg_ref[...]

    def slabs(col):
        return z_scr[:, col:col + GROUP_W].reshape(steps, nb, GROUP_W)

    x = x_ref[...].reshape(rows, D_MODEL)
    z_scr[...] = _dot(_rms(x, gm_ref[...]).astype(bf16), wmi_ref[...])

    xa[0:hist_a] = sta_ref[...]
    xa[hist_a:hist_a + steps] = slabs(COL_A_VAL) * jax.nn.sigmoid(slabs(COL_A_GATE))
    sa_ref[...] = xa[steps:steps + hist_a]
    for t in range(steps):
        acc = jnp.broadcast_to(cab_ref[...], (nb, GROUP_W))
        for k in range(CONV_A_WIDTH):
            acc = acc + caw_ref[k:k + 1, :] * xa[t + k]
        c_scr[t] = acc
    ya = _head_layer_norm(c_scr[...].reshape(rows, GROUP_W), nag_ref[...], nab_ref[...], avg)
    y_scr[:, 0:GROUP_W] = _silu(ya).astype(bf16)

    xp[0:hist_p] = stp_ref[...]
    xp[hist_p:hist_p + steps] = slabs(COL_ZB)
    sp_ref[...] = xp[steps:steps + hist_p]
    lane_group = lax.broadcasted_iota(jnp.int32, (nb, GROUP_W), 1) // HEAD_DIM
    win = jnp.left_shift(2, lane_group)
    for t in range(steps):
        zb = xp[hist_p + t]
        run = zb
        sums = []
        for j in range(1, max(POOL_WINDOWS)):
            run = run + xp[hist_p + t - j]
            if j + 1 in POOL_WINDOWS:
                sums.append(run)
        cnt = jnp.minimum(PAST_LEN + t + 1, win).astype(f32)
        c_scr[t] = _pool_select(sums, lane_group) / cnt - zb
    d = c_scr[...].reshape(rows, GROUP_W).astype(bf16)
    y_scr[:, GROUP_W:2 * GROUP_W] = (_dot(d, pw_ref[...]) * ps_ref[...]).astype(bf16)

    v = _head_layer_norm(jax.nn.gelu(z_scr[:, COL_V:COL_V + GROUP_W]),
                         sng_ref[...], snb_ref[...], avg)
    v_ref[...] = v.reshape(steps, nb, GROUP_W)
    for t in range(steps):
        s = jnp.broadcast_to(sb_ref[t:t + 1, :], (nb, GROUP_W))
        for j in range(t + 1):
            s = s + coef_ref[j, t:t + 1, :] * v_ref[j]
        u = jax.nn.gelu(z_scr[t * nb:(t + 1) * nb, COL_U:COL_U + GROUP_W])
        y_scr[t * nb:(t + 1) * nb, 2 * GROUP_W:3 * GROUP_W] = (u * s).astype(bf16)

    xs[0:hist_s] = sts_ref[...]
    xs[hist_s:hist_s + steps] = slabs(COL_C_GATE) * slabs(COL_D_IN)
    ss_ref[...] = xs[steps:steps + hist_s]
    for t in range(steps):
        conv = scw_ref[0:1, :] * xs[t]
        for k in range(1, SHORT_CONV_WIDTH):
            conv = conv + scw_ref[k:k + 1, :] * xs[t + k]
        b_gate = z_scr[t * nb:(t + 1) * nb, COL_B_GATE:COL_B_GATE + GROUP_W]
        y_scr[t * nb:(t + 1) * nb, 3 * GROUP_W:4 * GROUP_W] = (b_gate * conv).astype(bf16)

    out = x_ref[...].reshape(rows, D_MODEL) + _dot(y_scr[...], wmo_ref[...])
    o_ref[...] = out.reshape(steps, nb, D_MODEL)


def _mixer_sample_call(x, st_a, st_p, st_s, dec_batch, dec_seq, layer, wts, avg):
    nb = SAMPLE_SEQS
    rows = nb * dec_seq
    row_spec = pl.BlockSpec((dec_seq, nb, D_MODEL), lambda i: (0, i, 0))

    def state_spec(r):
        return pl.BlockSpec((r, nb, GROUP_W), lambda i: (0, i, 0))

    def state_in_spec(r):
        return pl.BlockSpec((None, r, nb, GROUP_W), lambda i: (layer, 0, i, 0))

    def state_shape(r):
        return jax.ShapeDtypeStruct((r, dec_batch, GROUP_W), f32)

    hist_a, hist_p, hist_s = CONV_A_WIDTH - 1, POOL_STATE, SHORT_CONV_WIDTH - 1
    in_specs = ([row_spec] + [_resident(w, layer) for w in wts] + [_resident(avg)]
                + [state_in_spec(hist_a), state_in_spec(hist_p), state_in_spec(hist_s)])
    return pl.pallas_call(
        functools.partial(_mixer_sample_kernel, dec_seq=dec_seq, layer=layer),
        grid=(dec_batch // nb,),
        in_specs=in_specs,
        out_specs=[row_spec, state_spec(hist_a), state_spec(hist_p), state_spec(hist_s),
                   state_spec(dec_seq)],
        out_shape=[jax.ShapeDtypeStruct((dec_seq, dec_batch, D_MODEL), f32),
                   state_shape(hist_a), state_shape(hist_p), state_shape(hist_s),
                   state_shape(dec_seq)],
        scratch_shapes=[pltpu.VMEM((rows, MIX_IN_W), f32),
                        pltpu.VMEM((hist_a + dec_seq, nb, GROUP_W), f32),
                        pltpu.VMEM((hist_p + dec_seq, nb, GROUP_W), f32),
                        pltpu.VMEM((hist_s + dec_seq, nb, GROUP_W), f32),
                        pltpu.VMEM((dec_seq, nb, GROUP_W), f32),
                        pltpu.VMEM((rows, D_MODEL), bf16)],
        compiler_params=pltpu.CompilerParams(
            dimension_semantics=("arbitrary",), vmem_limit_bytes=V7X_VMEM_LIMIT_BYTES),
        name="mixer_sample",
    )(x, *wts, avg, st_a, st_p, st_s)


def _block_diag(w):
    n, g, c, d = w.shape
    eye = jnp.eye(g, dtype=w.dtype)
    return (eye[None, :, None, :, None] * w[:, :, :, None, :]).reshape(n, g * c, g * d)


def kernel(x_prompt, x_sample, p_prompt, p_sample, state_conv_a, state_pool, state_short_conv,
           norm_ffn1, w_ffn1_gate, w_ffn1_up, w_ffn1_down, norm_mix, w_mix_in,
           conv_a_w, conv_a_b, norm_a_g, norm_a_b, pool_w, pool_scale,
           sgu_norm_g, sgu_norm_b, sgu_w, sgu_b, short_conv_w, w_mix_out,
           norm_ffn2, w_ffn2_gate, w_ffn2_up, w_ffn2_down,
           norm_ple, w_ple_gate, w_ple_proj, norm_final):
    depth = w_mix_in.shape[0]
    batch, seq, _ = x_prompt.shape
    dec_batch, dec_seq, _ = x_sample.shape
    assert seq % MIX_TILE == 0 and MIX_TILE % CHUNK == 0 and dec_seq <= CHUNK
    assert batch % MIX_SEQS == 0
    assert dec_batch % SAMPLE_SEQS == 0 and dec_seq == V7X_SUBLANES
    assert (batch * seq) % FFN_TILE == 0 and (dec_batch * dec_seq) % FFN_TILE == 0
    assert all(c % 256 == 0 for c in FFN_COL_SPLITS)

    avg = _block_diag(jnp.full((1, N_HEADS, HEAD_DIM, HEAD_DIM), 1.0 / HEAD_DIM, f32))[0]
    avg = avg.astype(bf16)

    ffn1_mats = [w[0].astype(bf16) for w in (w_ffn1_gate, w_ffn1_up, w_ffn1_down)]
    cast_in_ffn1 = (w_ffn2_gate, w_ffn2_up, w_ffn2_down, w_ple_gate, w_ple_proj,
                    w_mix_in, w_mix_out)
    cast_in_ffn2 = (w_ffn1_gate, w_ffn1_up, w_ffn1_down)
    gf = norm_final.reshape(1, D_MODEL)
    mix_tail = (conv_a_w, conv_a_b, norm_a_g, norm_a_b,
                _block_diag(pool_w).astype(bf16), pool_scale,
                sgu_norm_g, sgu_norm_b)
    sw_prompt = jnp.transpose(sgu_w, (0, 2, 1, 3)).reshape(depth, CHUNK, N_HEADS * CHUNK)
    sb_prompt = jnp.repeat(jnp.swapaxes(sgu_b, 1, 2), HEAD_DIM, axis=2)
    sw_sample = jnp.repeat(jnp.transpose(sgu_w[:, :, :dec_seq, :dec_seq], (0, 3, 2, 1)),
                           HEAD_DIM, axis=3)
    sb_sample = sb_prompt[:, :dec_seq]

    sample_rows = dec_seq * dec_batch
    pp = p_prompt.reshape(depth, batch * seq, D_PLE)
    ps = jnp.swapaxes(p_sample, 1, 2).reshape(depth, sample_rows, D_PLE)
    xp = x_prompt.reshape(batch * seq, D_MODEL)
    xs = jnp.swapaxes(x_sample, 0, 1).reshape(sample_rows, D_MODEL)
    st_a, st_p, st_s = (jnp.swapaxes(st, 1, 2)
                        for st in (state_conv_a, state_pool, state_short_conv))
    prompt_states, sample_states = [], []
    for i in range(depth):
        last = i == depth - 1

        xs, xp, mats = _ffn_call(xs, xp, i, (norm_ffn1, *ffn1_mats),
                                 cast=cast_in_ffn1, cast_layer=i)
        ffn2_mats, ple_mats, mix_mats = mats[0:3], mats[3:5], mats[5:7]
        mix_head = (norm_mix, *mix_mats) + mix_tail
        xp, sa, sp, ss = _mixer_prompt_call(
            xp, batch, seq, i, mix_head + (sw_prompt, sb_prompt, short_conv_w), avg)
        prompt_states.append((sa, sp, ss))
        xs, sa, sp, ss, sv = _mixer_sample_call(
            xs.reshape(dec_seq, dec_batch, D_MODEL), st_a, st_p, st_s, dec_batch, dec_seq, i,
            mix_head + (sw_sample, sb_sample, short_conv_w), avg)
        xs = xs.reshape(sample_rows, D_MODEL)
        sample_states.append((sa, sp, ss, sv))
        xs, xp, ffn1_mats = _ffn_call(
            xs, xp, i, (norm_ffn2, *ffn2_mats), ple=(ps, pp, (norm_ple, *ple_mats), gf),
            final_norm=last, cast=() if last else cast_in_ffn2, cast_layer=i + 1)

    def stack(states, j):
        return jnp.stack([s[j] for s in states])

    def stack_sample(j):
        return jnp.swapaxes(stack(sample_states, j), 1, 2)

    y_sample = jnp.swapaxes(xs.reshape(dec_seq, dec_batch, D_MODEL), 0, 1)
    return (xp.reshape(batch, seq, D_MODEL), y_sample,
            stack(prompt_states, 0), stack_sample(0),
            stack(prompt_states, 1), stack_sample(1),
            stack(prompt_states, 2), stack_sample(2),
            stack_sample(3))
```

```python
import functools

import jax
import jax.numpy as jnp
from jax import lax
from jax.experimental import pallas as pl
from jax.experimental.pallas import tpu as pltpu

D_MODEL = 1024
D_PLE = 256
D_FF = 2816
GROUP_W = 256
N_HEADS = 4
HEAD_DIM = 64
MIX_IN_W = 8 * GROUP_W
CONV_A_WIDTH = 31
POOL_WINDOWS = (2, 4, 8, 16)
POOL_STATE = 15
CHUNK = 128
SHORT_CONV_WIDTH = 3
EPS = 1e-6
PAST_LEN = 16384

V7X_SUBLANES = 8
V7X_BF16_ROWS = 16
V7X_VMEM_BYTES = 64 * 1024 * 1024
V7X_VMEM_LIMIT_BYTES = V7X_VMEM_BYTES - 4 * 1024 * 1024

FFN_TILE = 1024
FFN_SUB_TILE = 256
FFN_COL_SPLITS = (0, 1536, D_FF)
MIX_TILE = 256
MIX_SEQS = 4
MIX_HALO = 32
MIX_ROWS = 128
MIX_DOT_ROWS = 256
SAMPLE_SEQS = 64

COL_A_VAL, COL_A_GATE, COL_ZB, COL_U, COL_V, COL_B_GATE, COL_C_GATE, COL_D_IN = (
    i * GROUP_W for i in range(8))

bf16 = jnp.bfloat16
f32 = jnp.float32


def _dot(a, b):
    return jnp.dot(a, b, preferred_element_type=f32)


def _rms(x, g):
    ms = jnp.mean(x * x, axis=-1, keepdims=True)
    return x * lax.rsqrt(ms + EPS) * g


def _group_mean(x, avg):
    hi = x.astype(bf16)
    lo = (x - hi.astype(f32)).astype(bf16)
    return _dot(hi, avg) + _dot(lo, avg)


def _head_layer_norm(x, g, b, avg):
    mu = _group_mean(x, avg)
    xc = x - mu
    var = _dot((xc * xc).astype(bf16), avg)
    return xc * lax.rsqrt(var + EPS) * g + b


def _silu(x):
    return x * jax.nn.sigmoid(x)


def _pool_select(sums, lane_group):
    s2, s4, s8, s16 = sums
    return jnp.where(lane_group == 0, s2,
                     jnp.where(lane_group == 1, s4, jnp.where(lane_group == 2, s8, s16)))


def _ffn_rows(x_ref, rows, g_ref, wg_ref, wu_ref, wd_ref, act_ref):
    h = _rms(x_ref[rows, :], g_ref[...]).astype(bf16)
    for lo, hi in zip(FFN_COL_SPLITS[:-1], FFN_COL_SPLITS[1:]):
        cs = slice(lo, hi)
        gate = _dot(h, wg_ref[:, cs])
        up = _dot(h, wu_ref[:, cs])
        act_ref[rows, cs] = (_silu(gate) * up).astype(bf16)
    return x_ref[rows, :] + 0.5 * _dot(act_ref[rows, :], wd_ref[...])


def _sub_tiles():
    return [pl.ds(r, FFN_SUB_TILE) for r in range(0, FFN_TILE, FFN_SUB_TILE)]


class _SideCast:
    def __init__(self, arrays, layer, chunks):
        self.layer, self.chunks = layer, chunks
        self.shapes = [a.shape[1:] for a in arrays]
        assert all(r % (chunks * V7X_BF16_ROWS) == 0 for r, _ in self.shapes)
        self.rows = [r // chunks for r, _ in self.shapes]

    def in_copy(self, a, j, srcs, bufs, sem):
        start = pl.multiple_of(j * self.rows[a], V7X_BF16_ROWS)
        return pltpu.make_async_copy(
            srcs[a].at[self.layer, pl.ds(start, self.rows[a]), :], bufs[a], sem.at[a])

    def out_copy(self, a, j, bufs, dsts, sem):
        start = pl.multiple_of(j * self.rows[a], V7X_BF16_ROWS)
        return pltpu.make_async_copy(
            bufs[a], dsts[a].at[pl.ds(start, self.rows[a]), :], sem.at[a])


def _ffn_kernel(*refs, sample_steps, layer, ple, final_norm, side):
    refs = list(refs)
    take = lambda n: [refs.pop(0) for _ in range(n)]
    xs_ref, xp_ref, g_ref, wg_ref, wu_ref, wd_ref = take(6)
    if ple:
        ps_ref, pp_ref, gp_ref, wpg_ref, wpp_ref, gf_ref = take(6)
        gp_ref = _LayerRow(gp_ref, layer)
    n_side = len(side.shapes) if side else 0
    srcs = take(n_side)
    os_ref, op_ref = take(2)
    dsts = take(n_side)
    (act_ref,) = take(1)
    in_bufs, out_bufs = take(n_side), take(n_side)
    in_sem, out_sem = take(2) if side else (None, None)
    g_ref = _LayerRow(g_ref, layer)

    step = pl.program_id(0)
    is_sample = step < sample_steps
    chunk = step - sample_steps
    has_chunk = jnp.logical_and(chunk >= 0, chunk < side.chunks) if side else None

    if side:
        @pl.when(has_chunk)
        def _():
            for a in range(n_side):
                side.in_copy(a, chunk, srcs, in_bufs, in_sem).start()

    for rows in _sub_tiles():
        op_ref[rows, :] = jnp.where(is_sample, xs_ref[rows, :], xp_ref[rows, :])
        op_ref[rows, :] = _ffn_rows(op_ref, rows, g_ref, wg_ref, wu_ref, wd_ref, act_ref)
    for rows in _sub_tiles():
        if ple:
            p = jnp.where(is_sample, ps_ref[rows, :], pp_ref[rows, :]).astype(bf16)
            gate = jax.nn.sigmoid(
                _dot(_rms(op_ref[rows, :], gp_ref[...]).astype(bf16), wpg_ref[...]))
            x = op_ref[rows, :] + gate * _dot(p, wpp_ref[...])
            if final_norm:
                x = _rms(x, gf_ref[...])
            op_ref[rows, :] = x

    @pl.when(is_sample)
    def _():
        os_ref[...] = op_ref[...]

    if side:
        @pl.when(jnp.logical_and(chunk >= 1, chunk < side.chunks))
        def _():
            for a in range(n_side):
                side.out_copy(a, chunk - 1, out_bufs, dsts, out_sem).wait()

        @pl.when(has_chunk)
        def _():
            for a in range(n_side):
                side.in_copy(a, chunk, srcs, in_bufs, in_sem).wait()
                out_bufs[a][...] = in_bufs[a][...].astype(bf16)
                side.out_copy(a, chunk, out_bufs, dsts, out_sem).start()

        @pl.when(chunk == side.chunks - 1)
        def _():
            for a in range(n_side):
                side.out_copy(a, chunk, out_bufs, dsts, out_sem).wait()


class _LayerRow:
    def __init__(self, ref, layer):
        self.ref, self.layer = ref, layer

    def __getitem__(self, idx):
        assert idx is Ellipsis
        return self.ref[self.layer:self.layer + 1, :]


def _resident(w, layer=None):
    if layer is None or w.ndim == 2:
        return pl.BlockSpec(w.shape, lambda *_: (0,) * w.ndim, pipeline_mode=pl.Buffered(1))
    return pl.BlockSpec((None,) + w.shape[1:], lambda *_: (layer,) + (0,) * (w.ndim - 1),
                        pipeline_mode=pl.Buffered(1))


def _ffn_call(xs, xp, layer, ffn_w, ple=None, final_norm=False, cast=(), cast_layer=0):
    tile = FFN_TILE
    ns, np_ = xs.shape[0] // tile, xp.shape[0] // tile

    def sample_tile(i):
        return jnp.minimum(i, ns - 1)

    def prompt_tile(i):
        return jnp.maximum(i - ns, 0)

    xs_spec = pl.BlockSpec((tile, D_MODEL), lambda i: (sample_tile(i), 0),
                           pipeline_mode=pl.Buffered(1))
    xp_spec = pl.BlockSpec((tile, D_MODEL), lambda i: (prompt_tile(i), 0))
    in_specs = [xs_spec, xp_spec] + [_resident(w) for w in ffn_w]
    args = [xs, xp, *ffn_w]
    name = "ffn"
    if ple is not None:
        ps, pp, ple_w, gf = ple
        name = "ffn_ple"
        in_specs += ([pl.BlockSpec((None, tile, D_PLE), lambda i: (layer, sample_tile(i), 0),
                                   pipeline_mode=pl.Buffered(1)),
                      pl.BlockSpec((None, tile, D_PLE), lambda i: (layer, prompt_tile(i), 0))]
                     + [_resident(w) for w in ple_w] + [_resident(gf)])
        args += [ps, pp, *ple_w, gf]
    side = _SideCast(cast, cast_layer, np_) if cast else None
    any_spec = pl.BlockSpec(memory_space=pl.ANY)
    out_specs = [xs_spec, xp_spec] + [any_spec] * len(cast)
    out_shape = ([jax.ShapeDtypeStruct(xs.shape, f32), jax.ShapeDtypeStruct(xp.shape, f32)]
                 + [jax.ShapeDtypeStruct(a.shape[1:], bf16) for a in cast])
    scratch = [pltpu.VMEM((tile, D_FF), bf16)]
    if side:
        scratch += [pltpu.VMEM((r, s[1]), f32) for r, s in zip(side.rows, side.shapes)]
        scratch += [pltpu.VMEM((r, s[1]), bf16) for r, s in zip(side.rows, side.shapes)]
        scratch += [pltpu.SemaphoreType.DMA((len(cast),)), pltpu.SemaphoreType.DMA((len(cast),))]
    out = pl.pallas_call(
        functools.partial(_ffn_kernel, sample_steps=ns, layer=layer, ple=ple is not None,
                          final_norm=final_norm, side=side),
        grid=(ns + np_,),
        in_specs=in_specs + [any_spec] * len(cast),
        out_specs=out_specs,
        out_shape=out_shape,
        scratch_shapes=scratch,
        compiler_params=pltpu.CompilerParams(
            dimension_semantics=("arbitrary",), vmem_limit_bytes=V7X_VMEM_LIMIT_BYTES),
        name=name,
    )(*args, *cast)
    return out[0], out[1], list(out[2:])


def _mixer_prompt_kernel(x_ref, gm_ref, wmi_ref, wmo_ref, caw_ref, cab_ref, nag_ref, nab_ref,
                         pw_ref, ps_ref, sng_ref, snb_ref, sw_ref, sb_ref, scw_ref, avg_ref,
                         o_ref, sa_ref, sp_ref, ss_ref,
                         z_all, sh_all, c_all, d_all, y_all, *, layer):
    gm_ref, cab_ref, nag_ref, nab_ref, ps_ref, sng_ref, snb_ref = (
        _LayerRow(r, layer) for r in (gm_ref, cab_ref, nag_ref, nab_ref, ps_ref, sng_ref, snb_ref))
    tm, halo = MIX_TILE, MIX_HALO
    sub, half = V7X_SUBLANES, GROUP_W // 2
    step = pl.program_id(1)
    weights = (gm_ref, wmi_ref, wmo_ref, caw_ref, cab_ref, nag_ref, nab_ref, pw_ref, ps_ref,
               sng_ref, snb_ref, sw_ref, sb_ref, scw_ref, avg_ref)

    @pl.when(step == 0)
    def _():
        z_all[:, 0:halo, :] = jnp.zeros((MIX_SEQS, halo, MIX_IN_W), f32)
        sh_all[:, :, 0:halo, :] = jnp.zeros((MIX_SEQS, sub - 1, halo, 2 * GROUP_W), f32)

    for s in range(MIX_SEQS):
        for r in range(0, tm, MIX_DOT_ROWS):
            h = _rms(x_ref[s, r:r + MIX_DOT_ROWS, :], gm_ref[...]).astype(bf16)
            z_all[s, halo + r:halo + r + MIX_DOT_ROWS, :] = _dot(h, wmi_ref[...])
    for s in range(MIX_SEQS):
        _mixer_prompt_tile(step, x_ref.at[s], o_ref.at[s], weights, z_all.at[s], sh_all.at[s],
                           c_all.at[s], d_all.at[s], y_all.at[s])

    @pl.when(step == pl.num_programs(1) - 1)
    def _():
        for s in range(MIX_SEQS):
            z_scr = z_all.at[s]
            sa_ref[s] = z_scr[pl.ds(halo + tm - (CONV_A_WIDTH - 1), CONV_A_WIDTH - 1),
                              COL_A_VAL:COL_A_VAL + GROUP_W]
            sp_ref[s] = z_scr[pl.ds(halo + tm - POOL_STATE, POOL_STATE),
                              COL_ZB:COL_ZB + GROUP_W]
            ss_ref[s] = z_scr[pl.ds(halo + tm - (SHORT_CONV_WIDTH - 1), SHORT_CONV_WIDTH - 1),
                              COL_D_IN:COL_D_IN + GROUP_W]

    for s in range(MIX_SEQS):
        z_scr, sh_scr = z_all.at[s], sh_all.at[s]
        for col in (COL_A_VAL, COL_ZB, COL_D_IN):
            z_scr[0:halo, col:col + GROUP_W] = z_scr[tm:tm + halo, col:col + GROUP_W]
        sh_scr[0:3, 0:halo, :] = sh_scr[0:3, tm:tm + halo, :]
        for lanes in (slice(0, GROUP_W), slice(GROUP_W + half, 2 * GROUP_W)):
            sh_scr[3:sub - 1, 0:halo, lanes] = sh_scr[3:sub - 1, tm:tm + halo, lanes]


def _mixer_prompt_tile(step, x_ref, o_ref, weights, z_scr, sh_scr, c_scr, d_scr, y_scr):
    (gm_ref, wmi_ref, wmo_ref, caw_ref, cab_ref, nag_ref, nab_ref, pw_ref, ps_ref,
     sng_ref, snb_ref, sw_ref, sb_ref, scw_ref, avg_ref) = weights
    tm, halo, rb = MIX_TILE, MIX_HALO, MIX_ROWS
    sub, half = V7X_SUBLANES, GROUP_W // 2

    for r in range(0, tm, rb):
        rows = pl.ds(halo + r, rb)
        a_val = z_scr[rows, COL_A_VAL:COL_A_VAL + GROUP_W]
        a_gate = z_scr[rows, COL_A_GATE:COL_A_GATE + GROUP_W]
        z_scr[rows, COL_A_VAL:COL_A_VAL + GROUP_W] = a_val * jax.nn.sigmoid(a_gate)
        c_gate = z_scr[rows, COL_C_GATE:COL_C_GATE + GROUP_W]
        d_in = z_scr[rows, COL_D_IN:COL_D_IN + GROUP_W]
        z_scr[rows, COL_D_IN:COL_D_IN + GROUP_W] = c_gate * d_in
        for b in range(1, sub):
            late = pl.ds(halo + r - b, rb)
            sh_scr[b - 1, rows, 0:GROUP_W] = z_scr[late, COL_A_VAL:COL_A_VAL + GROUP_W]
            if b < 4:
                sh_scr[b - 1, rows, GROUP_W:2 * GROUP_W] = z_scr[late, COL_ZB:COL_ZB + GROUP_W]
            else:
                sh_scr[b - 1, rows, GROUP_W + half:2 * GROUP_W] = (
                    z_scr[late, COL_ZB + half:COL_ZB + GROUP_W])

    lane_group = lax.broadcasted_iota(jnp.int32, (rb, GROUP_W), 1) // HEAD_DIM
    win = jnp.left_shift(2, lane_group)
    row_iota = lax.broadcasted_iota(jnp.int32, (rb, GROUP_W), 0)
    first_head = lax.broadcasted_iota(jnp.int32, (rb, half), 1) < HEAD_DIM

    for r in range(0, tm, rb):
        def delayed(col, lanes, b, back):
            at = pl.ds(halo + r - back, rb)
            if b == 0:
                return z_scr[at, col + lanes.start:col + lanes.stop]
            base = 0 if col == COL_A_VAL else GROUP_W
            return sh_scr[b - 1, at, base + lanes.start:base + lanes.stop]

        acc = jnp.broadcast_to(cab_ref[...], (rb, GROUP_W))
        for j in range(CONV_A_WIDTH):
            k = CONV_A_WIDTH - 1 - j
            tap = delayed(COL_A_VAL, slice(0, GROUP_W), j % sub, j - j % sub)
            acc = acc + caw_ref[k:k + 1, :] * tap
        c_scr[r:r + rb, :] = acc

        lo, hi = slice(0, half), slice(half, GROUP_W)
        s2 = delayed(COL_ZB, lo, 0, 0) + delayed(COL_ZB, lo, 1, 0)
        s4 = s2 + delayed(COL_ZB, lo, 2, 0) + delayed(COL_ZB, lo, 3, 0)
        s8 = delayed(COL_ZB, hi, 0, 0)
        for b in range(1, sub):
            s8 = s8 + delayed(COL_ZB, hi, b, 0)
        s16 = s8
        for b in range(sub):
            s16 = s16 + delayed(COL_ZB, hi, b, sub)
        sums = jnp.concatenate(
            [jnp.where(first_head, s2, s4), jnp.where(first_head, s8, s16)], axis=1)
        zb = z_scr[pl.ds(halo + r, rb), COL_ZB:COL_ZB + GROUP_W]
        pos = step * tm + r + row_iota
        cnt = jnp.minimum(pos + 1, win).astype(f32)
        d_scr[r:r + rb, :] = (sums / cnt - zb).astype(bf16)

        rows = pl.ds(halo + r, rb)
        q0 = z_scr[pl.ds(halo + r - 2, rb), COL_D_IN:COL_D_IN + GROUP_W]
        q1 = z_scr[pl.ds(halo + r - 1, rb), COL_D_IN:COL_D_IN + GROUP_W]
        q2 = z_scr[rows, COL_D_IN:COL_D_IN + GROUP_W]
        conv = scw_ref[0:1, :] * q0 + scw_ref[1:2, :] * q1 + scw_ref[2:3, :] * q2
        b_gate = z_scr[rows, COL_B_GATE:COL_B_GATE + GROUP_W]
        y_scr[r:r + rb, 3 * GROUP_W:4 * GROUP_W] = (b_gate * conv).astype(bf16)

    avg = avg_ref[...]
    for r in range(0, tm, rb):
        ya = _head_layer_norm(c_scr[r:r + rb, :], nag_ref[...], nab_ref[...], avg)
        y_scr[r:r + rb, 0:GROUP_W] = _silu(ya).astype(bf16)
        y_scr[r:r + rb, GROUP_W:2 * GROUP_W] = (
            _dot(d_scr[r:r + rb, :], pw_ref[...]) * ps_ref[...]).astype(bf16)

    t_idx = lax.broadcasted_iota(jnp.int32, (CHUNK, N_HEADS * CHUNK), 0)
    s_idx = lax.broadcasted_iota(jnp.int32, (CHUNK, N_HEADS * CHUNK), 1) % CHUNK
    w_cat = jnp.where(s_idx <= t_idx, sw_ref[...], 0.0).astype(bf16)
    head_of_lane = lax.broadcasted_iota(jnp.int32, (CHUNK, GROUP_W), 1) // HEAD_DIM
    for c in range(tm // CHUNK):
        rows = pl.ds(halo + c * CHUNK, CHUNK)
        u = jax.nn.gelu(z_scr[rows, COL_U:COL_U + GROUP_W])
        v = jax.nn.gelu(z_scr[rows, COL_V:COL_V + GROUP_W])
        v = _head_layer_norm(v, sng_ref[...], snb_ref[...], avg).astype(bf16)
        v_stack = jnp.concatenate(
            [jnp.where(head_of_lane == hd, v, jnp.zeros_like(v)) for hd in range(N_HEADS)], axis=0)
        s = _dot(w_cat, v_stack) + sb_ref[...]
        y_scr[c * CHUNK:(c + 1) * CHUNK, 2 * GROUP_W:3 * GROUP_W] = (u * s).astype(bf16)

    for r in range(0, tm, MIX_DOT_ROWS):
        rows = slice(r, r + MIX_DOT_ROWS)
        o_ref[rows, :] = x_ref[rows, :] + _dot(y_scr[rows, :], wmo_ref[...])


def _mixer_prompt_call(x, batch, seq, layer, wts, avg):
    tm, ns = MIX_TILE, MIX_SEQS
    x = x.reshape(batch, seq, D_MODEL)
    row_spec = pl.BlockSpec((ns, tm, D_MODEL), lambda b, l: (b, l, 0))

    def state_spec(rows):
        return pl.BlockSpec((ns, rows, GROUP_W), lambda b, l: (b, 0, 0))

    in_specs = [row_spec] + [_resident(w, layer) for w in wts] + [_resident(avg)]
    out = pl.pallas_call(
        functools.partial(_mixer_prompt_kernel, layer=layer),
        grid=(batch // ns, seq // tm),
        in_specs=in_specs,
        out_specs=[row_spec, state_spec(CONV_A_WIDTH - 1), state_spec(POOL_STATE),
                   state_spec(SHORT_CONV_WIDTH - 1)],
        out_shape=[jax.ShapeDtypeStruct((batch, seq, D_MODEL), f32),
                   jax.ShapeDtypeStruct((batch, CONV_A_WIDTH - 1, GROUP_W), f32),
                   jax.ShapeDtypeStruct((batch, POOL_STATE, GROUP_W), f32),
                   jax.ShapeDtypeStruct((batch, SHORT_CONV_WIDTH - 1, GROUP_W), f32)],
        scratch_shapes=[pltpu.VMEM((ns, MIX_HALO + tm, MIX_IN_W), f32),
                        pltpu.VMEM((ns, V7X_SUBLANES - 1, MIX_HALO + tm, 2 * GROUP_W), f32),
                        pltpu.VMEM((ns, tm, GROUP_W), f32),
                        pltpu.VMEM((ns, tm, GROUP_W), bf16),
                        pltpu.VMEM((ns, tm, D_MODEL), bf16)],
        compiler_params=pltpu.CompilerParams(
            dimension_semantics=("arbitrary", "arbitrary"),
            vmem_limit_bytes=V7X_VMEM_LIMIT_BYTES),
        name="mixer_prompt",
    )(x, *wts, avg)
    return (out[0].reshape(batch * seq, D_MODEL),) + tuple(out[1:])


def _mixer_sample_kernel(x_ref, gm_ref, wmi_ref, wmo_ref, caw_ref, cab_ref, nag_ref, nab_ref,
                         pw_ref, ps_ref, sng_ref, snb_ref, coef_ref, sb_ref, scw_ref, avg_ref,
                         sta_ref, stp_ref, sts_ref,
                         o_ref, sa_ref, sp_ref, ss_ref, v_ref,
                         z_scr, xa, xp, xs, c_scr, y_scr, *, dec_seq, layer):
    gm_ref, cab_ref, nag_ref, nab_ref, ps_ref, sng_ref, snb_ref = (
        _LayerRow(r, layer) for r in (gm_ref, cab_ref, nag_ref, nab_ref, ps_ref, sng_ref, snb_ref))
    nb, steps = SAMPLE_SEQS, dec_seq
    rows = nb * steps
    hist_a, hist_p, hist_s = CONV_A_WIDTH - 1, POOL_STATE, SHORT_CONV_WIDTH - 1
    avg = avg_ref[...]

    def slabs(col):
        return z_scr[:, col:col + GROUP_W].reshape(steps, nb, GROUP_W)

    x = x_ref[...].reshape(rows, D_MODEL)
    z_scr[...] = _dot(_rms(x, gm_ref[...]).astype(bf16), wmi_ref[...])

    xa[0:hist_a] = sta_ref[...]
    xa[hist_a:hist_a + steps] = slabs(COL_A_VAL) * jax.nn.sigmoid(slabs(COL_A_GATE))
    sa_ref[...] = xa[steps:steps + hist_a]
    for t in range(steps):
        acc = jnp.broadcast_to(cab_ref[...], (nb, GROUP_W))
        for k in range(CONV_A_WIDTH):
            acc = acc + caw_ref[k:k + 1, :] * xa[t + k]
        c_scr[t] = acc
    ya = _head_layer_norm(c_scr[...].reshape(rows, GROUP_W), nag_ref[...], nab_ref[...], avg)
    y_scr[:, 0:GROUP_W] = _silu(ya).astype(bf16)

    xp[0:hist_p] = stp_ref[...]
    xp[hist_p:hist_p + steps] = slabs(COL_ZB)
    sp_ref[...] = xp[steps:steps + hist_p]
    lane_group = lax.broadcasted_iota(jnp.int32, (nb, GROUP_W), 1) // HEAD_DIM
    win = jnp.left_shift(2, lane_group)
    for t in range(steps):
        zb = xp[hist_p + t]
        run = zb
        sums = []
        for j in range(1, max(POOL_WINDOWS)):
            run = run + xp[hist_p + t - j]
            if j + 1 in POOL_WINDOWS:
                sums.append(run)
        cnt = jnp.minimum(PAST_LEN + t + 1, win).astype(f32)
        c_scr[t] = _pool_select(sums, lane_group) / cnt - zb
    d = c_scr[...].reshape(rows, GROUP_W).astype(bf16)
    y_scr[:, GROUP_W:2 * GROUP_W] = (_dot(d, pw_ref[...]) * ps_ref[...]).astype(bf16)

    v = _head_layer_norm(jax.nn.gelu(z_scr[:, COL_V:COL_V + GROUP_W]),
                         sng_ref[...], snb_ref[...], avg)
    v_ref[...] = v.reshape(steps, nb, GROUP_W)
    for t in range(steps):
        s = jnp.broadcast_to(sb_ref[t:t + 1, :], (nb, GROUP_W))
        for j in range(t + 1):
            s = s + coef_ref[j, t:t + 1, :] * v_ref[j]
        u = jax.nn.gelu(z_scr[t * nb:(t + 1) * nb, COL_U:COL_U + GROUP_W])
        y_scr[t * nb:(t + 1) * nb, 2 * GROUP_W:3 * GROUP_W] = (u * s).astype(bf16)

    xs[0:hist_s] = sts_ref[...]
    xs[hist_s:hist_s + steps] = slabs(COL_C_GATE) * slabs(COL_D_IN)
    ss_ref[...] = xs[steps:steps + hist_s]
    for t in range(steps):
        conv = scw_ref[0:1, :] * xs[t]
        for k in range(1, SHORT_CONV_WIDTH):
            conv = conv + scw_ref[k:k + 1, :] * xs[t + k]
        b_gate = z_scr[t * nb:(t + 1) * nb, COL_B_GATE:COL_B_GATE + GROUP_W]
        y_scr[t * nb:(t + 1) * nb, 3 * GROUP_W:4 * GROUP_W] = (b_gate * conv).astype(bf16)

    out = x_ref[...].reshape(rows, D_MODEL) + _dot(y_scr[...], wmo_ref[...])
    o_ref[...] = out.reshape(steps, nb, D_MODEL)


def _mixer_sample_call(x, st_a, st_p, st_s, dec_batch, dec_seq, layer, wts, avg):
    nb = SAMPLE_SEQS
    rows = nb * dec_seq
    row_spec = pl.BlockSpec((dec_seq, nb, D_MODEL), lambda i: (0, i, 0))

    def state_spec(r):
        return pl.BlockSpec((r, nb, GROUP_W), lambda i: (0, i, 0))

    def state_in_spec(r):
        return pl.BlockSpec((None, r, nb, GROUP_W), lambda i: (layer, 0, i, 0))

    def state_shape(r):
        return jax.ShapeDtypeStruct((r, dec_batch, GROUP_W), f32)

    hist_a, hist_p, hist_s = CONV_A_WIDTH - 1, POOL_STATE, SHORT_CONV_WIDTH - 1
    in_specs = ([row_spec] + [_resident(w, layer) for w in wts] + [_resident(avg)]
                + [state_in_spec(hist_a), state_in_spec(hist_p), state_in_spec(hist_s)])
    return pl.pallas_call(
        functools.partial(_mixer_sample_kernel, dec_seq=dec_seq, layer=layer),
        grid=(dec_batch // nb,),
        in_specs=in_specs,
        out_specs=[row_spec, state_spec(hist_a), state_spec(hist_p), state_spec(hist_s),
                   state_spec(dec_seq)],
        out_shape=[jax.ShapeDtypeStruct((dec_seq, dec_batch, D_MODEL), f32),
                   state_shape(hist_a), state_shape(hist_p), state_shape(hist_s),
                   state_shape(dec_seq)],
        scratch_shapes=[pltpu.VMEM((rows, MIX_IN_W), f32),
                        pltpu.VMEM((hist_a + dec_seq, nb, GROUP_W), f32),
                        pltpu.VMEM((hist_p + dec_seq, nb, GROUP_W), f32),
                        pltpu.VMEM((hist_s + dec_seq, nb, GROUP_W), f32),
                        pltpu.VMEM((dec_seq, nb, GROUP_W), f32),
                        pltpu.VMEM((rows, D_MODEL), bf16)],
        compiler_params=pltpu.CompilerParams(
            dimension_semantics=("arbitrary",), vmem_limit_bytes=V7X_VMEM_LIMIT_BYTES),
        name="mixer_sample",
    )(x, *wts, avg, st_a, st_p, st_s)


def _block_diag(w):
    n, g, c, d = w.shape
    eye = jnp.eye(g, dtype=w.dtype)
    return (eye[None, :, None, :, None] * w[:, :, :, None, :]).reshape(n, g * c, g * d)


def kernel(x_prompt, x_sample, p_prompt, p_sample, state_conv_a, state_pool, state_short_conv,
           norm_ffn1, w_ffn1_gate, w_ffn1_up, w_ffn1_down, norm_mix, w_mix_in,
           conv_a_w, conv_a_b, norm_a_g, norm_a_b, pool_w, pool_scale,
           sgu_norm_g, sgu_norm_b, sgu_w, sgu_b, short_conv_w, w_mix_out,
           norm_ffn2, w_ffn2_gate, w_ffn2_up, w_ffn2_down,
           norm_ple, w_ple_gate, w_ple_proj, norm_final):
    depth = w_mix_in.shape[0]
    batch, seq, _ = x_prompt.shape
    dec_batch, dec_seq, _ = x_sample.shape
    assert seq % MIX_TILE == 0 and MIX_TILE % CHUNK == 0 and dec_seq <= CHUNK
    assert batch % MIX_SEQS == 0
    assert dec_batch % SAMPLE_SEQS == 0 and dec_seq == V7X_SUBLANES
    assert (batch * seq) % FFN_TILE == 0 and (dec_batch * dec_seq) % FFN_TILE == 0
    assert all(c % 256 == 0 for c in FFN_COL_SPLITS)

    avg = _block_diag(jnp.full((1, N_HEADS, HEAD_DIM, HEAD_DIM), 1.0 / HEAD_DIM, f32))[0]
    avg = avg.astype(bf16)

    ffn1_mats = [w[0].astype(bf16) for w in (w_ffn1_gate, w_ffn1_up, w_ffn1_down)]
    cast_in_ffn1 = (w_ffn2_gate, w_ffn2_up, w_ffn2_down, w_ple_gate, w_ple_proj,
                    w_mix_in, w_mix_out)
    cast_in_ffn2 = (w_ffn1_gate, w_ffn1_up, w_ffn1_down)
    gf = norm_final.reshape(1, D_MODEL)
    mix_tail = (conv_a_w, conv_a_b, norm_a_g, norm_a_b,
                _block_diag(pool_w).astype(bf16), pool_scale,
                sgu_norm_g, sgu_norm_b)
    sw_prompt = jnp.transpose(sgu_w, (0, 2, 1, 3)).reshape(depth, CHUNK, N_HEADS * CHUNK)
    sb_prompt = jnp.repeat(jnp.swapaxes(sgu_b, 1, 2), HEAD_DIM, axis=2)
    sw_sample = jnp.repeat(jnp.transpose(sgu_w[:, :, :dec_seq, :dec_seq], (0, 3, 2, 1)),
                           HEAD_DIM, axis=3)
    sb_sample = sb_prompt[:, :dec_seq]

    sample_rows = dec_seq * dec_batch
    pp = p_prompt.reshape(depth, batch * seq, D_PLE)
    ps = jnp.swapaxes(p_sample, 1, 2).reshape(depth, sample_rows, D_PLE)
    xp = x_prompt.reshape(batch * seq, D_MODEL)
    xs = jnp.swapaxes(x_sample, 0, 1).reshape(sample_rows, D_MODEL)
    st_a, st_p, st_s = (jnp.swapaxes(st, 1, 2)
                        for st in (state_conv_a, state_pool, state_short_conv))
    prompt_states, sample_states = [], []
    for i in range(depth):
        last = i == depth - 1

        xs, xp, mats = _ffn_call(xs, xp, i, (norm_ffn1, *ffn1_mats),
                                 cast=cast_in_ffn1, cast_layer=i)
        ffn2_mats, ple_mats, mix_mats = mats[0:3], mats[3:5], mats[5:7]
        mix_head = (norm_mix, *mix_mats) + mix_tail
        xp, sa, sp, ss = _mixer_prompt_call(
            xp, batch, seq, i, mix_head + (sw_prompt, sb_prompt, short_conv_w), avg)
        prompt_states.append((sa, sp, ss))
        xs, sa, sp, ss, sv = _mixer_sample_call(
            xs.reshape(dec_seq, dec_batch, D_MODEL), st_a, st_p, st_s, dec_batch, dec_seq, i,
            mix_head + (sw_sample, sb_sample, short_conv_w), avg)
        xs = xs.reshape(sample_rows, D_MODEL)
        sample_states.append((sa, sp, ss, sv))
        xs, xp, ffn1_mats = _ffn_call(
            xs, xp, i, (norm_ffn2, *ffn2_mats), ple=(ps, pp, (norm_ple, *ple_mats), gf),
            final_norm=last, cast=() if last else cast_in_ffn2, cast_layer=i + 1)

    def stack(states, j):
        return jnp.stack([s[j] for s in states])

    def stack_sample(j):
        return jnp.swapaxes(stack(sample_states, j), 1, 2)

    y_sample = jnp.swapaxes(xs.reshape(dec_seq, dec_batch, D_MODEL), 0, 1)
    return (xp.reshape(batch, seq, D_MODEL), y_sample,
            stack(prompt_states, 0), stack_sample(0),
            stack(prompt_states, 1), stack_sample(1),
            stack(prompt_states, 2), stack_sample(2),
            stack_sample(3))
```

```python
import functools

import jax
import jax.numpy as jnp
from jax import lax
from jax.experimental import pallas as pl
from jax.experimental.pallas import tpu as pltpu

D_MODEL = 1024
D_PLE = 256
D_FF = 2816
GROUP_W = 256
N_HEADS = 4
HEAD_DIM = 64
MIX_IN_W = 8 * GROUP_W
CONV_A_WIDTH = 31
POOL_WINDOWS = (2, 4, 8, 16)
POOL_STATE = 15
CHUNK = 128
SHORT_CONV_WIDTH = 3
EPS = 1e-6
PAST_LEN = 16384

V7X_SUBLANES = 8
V7X_BF16_ROWS = 16
V7X_VMEM_BYTES = 64 * 1024 * 1024
V7X_VMEM_LIMIT_BYTES = V7X_VMEM_BYTES - 4 * 1024 * 1024

FFN_TILE = 1024
FFN_SUB_TILE = 256
FFN_COL_SPLITS = (0, 1536, D_FF)
MIX_TILE = 512
MIX_SEQS = 2
MIX_HALO = 32
MIX_ROWS = 128
SAMPLE_SEQS = 128

COL_A_VAL, COL_A_GATE, COL_ZB, COL_U, COL_V, COL_B_GATE, COL_C_GATE, COL_D_IN = (
    i * GROUP_W for i in range(8))

bf16 = jnp.bfloat16
f32 = jnp.float32


def _dot(a, b):
    return jnp.dot(a, b, preferred_element_type=f32)


def _rms(x, g):
    ms = jnp.mean(x * x, axis=-1, keepdims=True)
    return x * lax.rsqrt(ms + EPS) * g


def _group_mean(x, avg):
    hi = x.astype(bf16)
    lo = (x - hi.astype(f32)).astype(bf16)
    return _dot(hi, avg) + _dot(lo, avg)


def _head_layer_norm(x, g, b, avg):
    mu = _group_mean(x, avg)
    xc = x - mu
    var = _dot((xc * xc).astype(bf16), avg)
    return xc * lax.rsqrt(var + EPS) * g + b


def _silu(x):
    return x * jax.nn.sigmoid(x)


def _pool_select(sums, lane_group):
    s2, s4, s8, s16 = sums
    return jnp.where(lane_group == 0, s2,
                     jnp.where(lane_group == 1, s4, jnp.where(lane_group == 2, s8, s16)))


def _ffn_rows(x_ref, rows, g_ref, wg_ref, wu_ref, wd_ref, act_ref):
    h = _rms(x_ref[rows, :], g_ref[...]).astype(bf16)
    for lo, hi in zip(FFN_COL_SPLITS[:-1], FFN_COL_SPLITS[1:]):
        cs = slice(lo, hi)
        gate = _dot(h, wg_ref[:, cs])
        up = _dot(h, wu_ref[:, cs])
        act_ref[rows, cs] = (_silu(gate) * up).astype(bf16)
    return x_ref[rows, :] + 0.5 * _dot(act_ref[rows, :], wd_ref[...])


def _sub_tiles():
    return [pl.ds(r, FFN_SUB_TILE) for r in range(0, FFN_TILE, FFN_SUB_TILE)]


class _SideCast:
    def __init__(self, arrays, layer, chunks):
        self.layer, self.chunks = layer, chunks
        self.shapes = [a.shape[1:] for a in arrays]
        assert all(r % (chunks * V7X_BF16_ROWS) == 0 for r, _ in self.shapes)
        self.rows = [r // chunks for r, _ in self.shapes]

    def in_copy(self, a, j, srcs, bufs, sem):
        start = pl.multiple_of(j * self.rows[a], V7X_BF16_ROWS)
        return pltpu.make_async_copy(
            srcs[a].at[self.layer, pl.ds(start, self.rows[a]), :], bufs[a], sem.at[a])

    def out_copy(self, a, j, bufs, dsts, sem):
        start = pl.multiple_of(j * self.rows[a], V7X_BF16_ROWS)
        return pltpu.make_async_copy(
            bufs[a], dsts[a].at[pl.ds(start, self.rows[a]), :], sem.at[a])


def _ffn_kernel(*refs, sample_steps, layer, ple, final_norm, side):
    refs = list(refs)
    take = lambda n: [refs.pop(0) for _ in range(n)]
    xs_ref, xp_ref, g_ref, wg_ref, wu_ref, wd_ref = take(6)
    if ple:
        ps_ref, pp_ref, gp_ref, wpg_ref, wpp_ref, gf_ref = take(6)
        gp_ref = _LayerRow(gp_ref, layer)
    n_side = len(side.shapes) if side else 0
    srcs = take(n_side)
    os_ref, op_ref = take(2)
    dsts = take(n_side)
    (act_ref,) = take(1)
    in_bufs, out_bufs = take(n_side), take(n_side)
    in_sem, out_sem = take(2) if side else (None, None)
    g_ref = _LayerRow(g_ref, layer)

    step = pl.program_id(0)
    is_sample = step < sample_steps
    chunk = step - sample_steps
    has_chunk = jnp.logical_and(chunk >= 0, chunk < side.chunks) if side else None

    if side:
        @pl.when(has_chunk)
        def _():
            for a in range(n_side):
                side.in_copy(a, chunk, srcs, in_bufs, in_sem).start()

    def embed(rows):
        p = jnp.where(is_sample, ps_ref[rows, :], pp_ref[rows, :]).astype(bf16)
        gate = jax.nn.sigmoid(
            _dot(_rms(op_ref[rows, :], gp_ref[...]).astype(bf16), wpg_ref[...]))
        x = op_ref[rows, :] + gate * _dot(p, wpp_ref[...])
        if final_norm:
            x = _rms(x, gf_ref[...])
        op_ref[rows, :] = x

    previous = None
    for rows in _sub_tiles():
        op_ref[rows, :] = jnp.where(is_sample, xs_ref[rows, :], xp_ref[rows, :])
        op_ref[rows, :] = _ffn_rows(op_ref, rows, g_ref, wg_ref, wu_ref, wd_ref, act_ref)
        if ple and previous is not None:
            embed(previous)
        previous = rows
    if ple:
        embed(previous)

    @pl.when(is_sample)
    def _():
        os_ref[...] = op_ref[...]

    if side:
        @pl.when(jnp.logical_and(chunk >= 1, chunk < side.chunks))
        def _():
            for a in range(n_side):
                side.out_copy(a, chunk - 1, out_bufs, dsts, out_sem).wait()

        @pl.when(has_chunk)
        def _():
            for a in range(n_side):
                side.in_copy(a, chunk, srcs, in_bufs, in_sem).wait()
                out_bufs[a][...] = in_bufs[a][...].astype(bf16)
                side.out_copy(a, chunk, out_bufs, dsts, out_sem).start()

        @pl.when(chunk == side.chunks - 1)
        def _():
            for a in range(n_side):
                side.out_copy(a, chunk, out_bufs, dsts, out_sem).wait()


class _LayerRow:
    def __init__(self, ref, layer):
        self.ref, self.layer = ref, layer

    def __getitem__(self, idx):
        assert idx is Ellipsis
        return self.ref[self.layer:self.layer + 1, :]


def _resident(w, layer=None):
    if layer is None or w.ndim == 2:
        return pl.BlockSpec(w.shape, lambda *_: (0,) * w.ndim, pipeline_mode=pl.Buffered(1))
    return pl.BlockSpec((None,) + w.shape[1:], lambda *_: (layer,) + (0,) * (w.ndim - 1),
                        pipeline_mode=pl.Buffered(1))


def _ffn_call(xs, xp, layer, ffn_w, ple=None, final_norm=False, cast=(), cast_layer=0):
    tile = FFN_TILE
    ns, np_ = xs.shape[0] // tile, xp.shape[0] // tile

    def sample_tile(i):
        return jnp.minimum(i, ns - 1)

    def prompt_tile(i):
        return jnp.maximum(i - ns, 0)

    xs_spec = pl.BlockSpec((tile, D_MODEL), lambda i: (sample_tile(i), 0),
                           pipeline_mode=pl.Buffered(1))
    xp_spec = pl.BlockSpec((tile, D_MODEL), lambda i: (prompt_tile(i), 0))
    in_specs = [xs_spec, xp_spec] + [_resident(w) for w in ffn_w]
    args = [xs, xp, *ffn_w]
    name = "ffn"
    if ple is not None:
        ps, pp, ple_w, gf = ple
        name = "ffn_ple"
        in_specs += ([pl.BlockSpec((None, tile, D_PLE), lambda i: (layer, sample_tile(i), 0),
                                   pipeline_mode=pl.Buffered(1)),
                      pl.BlockSpec((None, tile, D_PLE), lambda i: (layer, prompt_tile(i), 0))]
                     + [_resident(w) for w in ple_w] + [_resident(gf)])
        args += [ps, pp, *ple_w, gf]
    side = _SideCast(cast, cast_layer, np_) if cast else None
    any_spec = pl.BlockSpec(memory_space=pl.ANY)
    out_specs = [xs_spec, xp_spec] + [any_spec] * len(cast)
    out_shape = ([jax.ShapeDtypeStruct(xs.shape, f32), jax.ShapeDtypeStruct(xp.shape, f32)]
                 + [jax.ShapeDtypeStruct(a.shape[1:], bf16) for a in cast])
    scratch = [pltpu.VMEM((tile, D_FF), bf16)]
    if side:
        scratch += [pltpu.VMEM((r, s[1]), f32) for r, s in zip(side.rows, side.shapes)]
        scratch += [pltpu.VMEM((r, s[1]), bf16) for r, s in zip(side.rows, side.shapes)]
        scratch += [pltpu.SemaphoreType.DMA((len(cast),)), pltpu.SemaphoreType.DMA((len(cast),))]
    out = pl.pallas_call(
        functools.partial(_ffn_kernel, sample_steps=ns, layer=layer, ple=ple is not None,
                          final_norm=final_norm, side=side),
        grid=(ns + np_,),
        in_specs=in_specs + [any_spec] * len(cast),
        out_specs=out_specs,
        out_shape=out_shape,
        scratch_shapes=scratch,
        compiler_params=pltpu.CompilerParams(
            dimension_semantics=("arbitrary",), vmem_limit_bytes=V7X_VMEM_LIMIT_BYTES),
        name=name,
    )(*args, *cast)
    return out[0], out[1], list(out[2:])


def _mixer_prompt_kernel(x_ref, gm_ref, wmi_ref, wmo_ref, caw_ref, cab_ref, nag_ref, nab_ref,
                         pw_ref, ps_ref, sng_ref, snb_ref, sw_ref, sb_ref, scw_ref, avg_ref,
                         o_ref, sa_ref, sp_ref, ss_ref,
                         z_all, sh_all, c_all, d_all, y_all, *, layer):
    gm_ref, cab_ref, nag_ref, nab_ref, ps_ref, sng_ref, snb_ref = (
        _LayerRow(r, layer) for r in (gm_ref, cab_ref, nag_ref, nab_ref, ps_ref, sng_ref, snb_ref))
    tm, halo = MIX_TILE, MIX_HALO
    sub, half = V7X_SUBLANES, GROUP_W // 2
    step = pl.program_id(1)
    weights = (gm_ref, wmi_ref, wmo_ref, caw_ref, cab_ref, nag_ref, nab_ref, pw_ref, ps_ref,
               sng_ref, snb_ref, sw_ref, sb_ref, scw_ref, avg_ref)

    @pl.when(step == 0)
    def _():
        z_all[:, 0:halo, :] = jnp.zeros((MIX_SEQS, halo, MIX_IN_W), f32)
        sh_all[:, :, 0:halo, :] = jnp.zeros((MIX_SEQS, sub - 1, halo, 2 * GROUP_W), f32)

    for s in range(MIX_SEQS):
        h = _rms(x_ref[s], gm_ref[...]).astype(bf16)
        z_all[s, halo:halo + tm, :] = _dot(h, wmi_ref[...])
    for s in range(MIX_SEQS):
        _mixer_prompt_tile(step, x_ref.at[s], o_ref.at[s], weights, z_all.at[s], sh_all.at[s],
                           c_all.at[s], d_all.at[s], y_all.at[s])

    @pl.when(step == pl.num_programs(1) - 1)
    def _():
        for s in range(MIX_SEQS):
            z_scr = z_all.at[s]
            sa_ref[s] = z_scr[pl.ds(halo + tm - (CONV_A_WIDTH - 1), CONV_A_WIDTH - 1),
                              COL_A_VAL:COL_A_VAL + GROUP_W]
            sp_ref[s] = z_scr[pl.ds(halo + tm - POOL_STATE, POOL_STATE),
                              COL_ZB:COL_ZB + GROUP_W]
            ss_ref[s] = z_scr[pl.ds(halo + tm - (SHORT_CONV_WIDTH - 1), SHORT_CONV_WIDTH - 1),
                              COL_D_IN:COL_D_IN + GROUP_W]

    for s in range(MIX_SEQS):
        z_scr, sh_scr = z_all.at[s], sh_all.at[s]
        for col in (COL_A_VAL, COL_ZB, COL_D_IN):
            z_scr[0:halo, col:col + GROUP_W] = z_scr[tm:tm + halo, col:col + GROUP_W]
        sh_scr[0:3, 0:halo, :] = sh_scr[0:3, tm:tm + halo, :]
        for lanes in (slice(0, GROUP_W), slice(GROUP_W + half, 2 * GROUP_W)):
            sh_scr[3:sub - 1, 0:halo, lanes] = sh_scr[3:sub - 1, tm:tm + halo, lanes]


def _mixer_prompt_tile(step, x_ref, o_ref, weights, z_scr, sh_scr, c_scr, d_scr, y_scr):
    (gm_ref, wmi_ref, wmo_ref, caw_ref, cab_ref, nag_ref, nab_ref, pw_ref, ps_ref,
     sng_ref, snb_ref, sw_ref, sb_ref, scw_ref, avg_ref) = weights
    tm, halo, rb = MIX_TILE, MIX_HALO, MIX_ROWS
    sub, half = V7X_SUBLANES, GROUP_W // 2

    for r in range(0, tm, rb):
        rows = pl.ds(halo + r, rb)
        a_val = z_scr[rows, COL_A_VAL:COL_A_VAL + GROUP_W]
        a_gate = z_scr[rows, COL_A_GATE:COL_A_GATE + GROUP_W]
        z_scr[rows, COL_A_VAL:COL_A_VAL + GROUP_W] = a_val * jax.nn.sigmoid(a_gate)
        c_gate = z_scr[rows, COL_C_GATE:COL_C_GATE + GROUP_W]
        d_in = z_scr[rows, COL_D_IN:COL_D_IN + GROUP_W]
        z_scr[rows, COL_D_IN:COL_D_IN + GROUP_W] = c_gate * d_in
        for b in range(1, sub):
            late = pl.ds(halo + r - b, rb)
            sh_scr[b - 1, rows, 0:GROUP_W] = z_scr[late, COL_A_VAL:COL_A_VAL + GROUP_W]
            if b < 4:
                sh_scr[b - 1, rows, GROUP_W:2 * GROUP_W] = z_scr[late, COL_ZB:COL_ZB + GROUP_W]
            else:
                sh_scr[b - 1, rows, GROUP_W + half:2 * GROUP_W] = (
                    z_scr[late, COL_ZB + half:COL_ZB + GROUP_W])

    lane_group = lax.broadcasted_iota(jnp.int32, (rb, GROUP_W), 1) // HEAD_DIM
    win = jnp.left_shift(2, lane_group)
    row_iota = lax.broadcasted_iota(jnp.int32, (rb, GROUP_W), 0)
    first_head = lax.broadcasted_iota(jnp.int32, (rb, half), 1) < HEAD_DIM

    for r in range(0, tm, rb):
        def delayed(col, lanes, b, back):
            at = pl.ds(halo + r - back, rb)
            if b == 0:
                return z_scr[at, col + lanes.start:col + lanes.stop]
            base = 0 if col == COL_A_VAL else GROUP_W
            return sh_scr[b - 1, at, base + lanes.start:base + lanes.stop]

        acc = jnp.broadcast_to(cab_ref[...], (rb, GROUP_W))
        for j in range(CONV_A_WIDTH):
            k = CONV_A_WIDTH - 1 - j
            tap = delayed(COL_A_VAL, slice(0, GROUP_W), j % sub, j - j % sub)
            acc = acc + caw_ref[k:k + 1, :] * tap
        c_scr[r:r + rb, :] = acc

        lo, hi = slice(0, half), slice(half, GROUP_W)
        s2 = delayed(COL_ZB, lo, 0, 0) + delayed(COL_ZB, lo, 1, 0)
        s4 = s2 + delayed(COL_ZB, lo, 2, 0) + delayed(COL_ZB, lo, 3, 0)
        s8 = delayed(COL_ZB, hi, 0, 0)
        for b in range(1, sub):
            s8 = s8 + delayed(COL_ZB, hi, b, 0)
        s16 = s8
        for b in range(sub):
            s16 = s16 + delayed(COL_ZB, hi, b, sub)
        sums = jnp.concatenate(
            [jnp.where(first_head, s2, s4), jnp.where(first_head, s8, s16)], axis=1)
        zb = z_scr[pl.ds(halo + r, rb), COL_ZB:COL_ZB + GROUP_W]
        pos = step * tm + r + row_iota
        cnt = jnp.minimum(pos + 1, win).astype(f32)
        d_scr[r:r + rb, :] = (sums / cnt - zb).astype(bf16)

        rows = pl.ds(halo + r, rb)
        q0 = z_scr[pl.ds(halo + r - 2, rb), COL_D_IN:COL_D_IN + GROUP_W]
        q1 = z_scr[pl.ds(halo + r - 1, rb), COL_D_IN:COL_D_IN + GROUP_W]
        q2 = z_scr[rows, COL_D_IN:COL_D_IN + GROUP_W]
        conv = scw_ref[0:1, :] * q0 + scw_ref[1:2, :] * q1 + scw_ref[2:3, :] * q2
        b_gate = z_scr[rows, COL_B_GATE:COL_B_GATE + GROUP_W]
        y_scr[r:r + rb, 3 * GROUP_W:4 * GROUP_W] = (b_gate * conv).astype(bf16)

    avg = avg_ref[...]
    for r in range(0, tm, rb):
        ya = _head_layer_norm(c_scr[r:r + rb, :], nag_ref[...], nab_ref[...], avg)
        y_scr[r:r + rb, 0:GROUP_W] = _silu(ya).astype(bf16)
        y_scr[r:r + rb, GROUP_W:2 * GROUP_W] = (
            _dot(d_scr[r:r + rb, :], pw_ref[...]) * ps_ref[...]).astype(bf16)

    t_idx = lax.broadcasted_iota(jnp.int32, (CHUNK, N_HEADS * CHUNK), 0)
    s_idx = lax.broadcasted_iota(jnp.int32, (CHUNK, N_HEADS * CHUNK), 1) % CHUNK
    w_cat = jnp.where(s_idx <= t_idx, sw_ref[...], 0.0).astype(bf16)
    head_of_lane = lax.broadcasted_iota(jnp.int32, (CHUNK, GROUP_W), 1) // HEAD_DIM
    for c in range(tm // CHUNK):
        rows = pl.ds(halo + c * CHUNK, CHUNK)
        u = jax.nn.gelu(z_scr[rows, COL_U:COL_U + GROUP_W])
        v = jax.nn.gelu(z_scr[rows, COL_V:COL_V + GROUP_W])
        v = _head_layer_norm(v, sng_ref[...], snb_ref[...], avg).astype(bf16)
        v_stack = jnp.concatenate(
            [jnp.where(head_of_lane == hd, v, jnp.zeros_like(v)) for hd in range(N_HEADS)], axis=0)
        s = _dot(w_cat, v_stack) + sb_ref[...]
        y_scr[c * CHUNK:(c + 1) * CHUNK, 2 * GROUP_W:3 * GROUP_W] = (u * s).astype(bf16)

    o_ref[...] = x_ref[...] + _dot(y_scr[...], wmo_ref[...])


def _mixer_prompt_call(x, batch, seq, layer, wts, avg):
    tm, ns = MIX_TILE, MIX_SEQS
    x = x.reshape(batch, seq, D_MODEL)
    row_spec = pl.BlockSpec((ns, tm, D_MODEL), lambda b, l: (b, l, 0))

    def state_spec(rows):
        return pl.BlockSpec((ns, rows, GROUP_W), lambda b, l: (b, 0, 0))

    in_specs = [row_spec] + [_resident(w, layer) for w in wts] + [_resident(avg)]
    out = pl.pallas_call(
        functools.partial(_mixer_prompt_kernel, layer=layer),
        grid=(batch // ns, seq // tm),
        in_specs=in_specs,
        out_specs=[row_spec, state_spec(CONV_A_WIDTH - 1), state_spec(POOL_STATE),
                   state_spec(SHORT_CONV_WIDTH - 1)],
        out_shape=[jax.ShapeDtypeStruct((batch, seq, D_MODEL), f32),
                   jax.ShapeDtypeStruct((batch, CONV_A_WIDTH - 1, GROUP_W), f32),
                   jax.ShapeDtypeStruct((batch, POOL_STATE, GROUP_W), f32),
                   jax.ShapeDtypeStruct((batch, SHORT_CONV_WIDTH - 1, GROUP_W), f32)],
        scratch_shapes=[pltpu.VMEM((ns, MIX_HALO + tm, MIX_IN_W), f32),
                        pltpu.VMEM((ns, V7X_SUBLANES - 1, MIX_HALO + tm, 2 * GROUP_W), f32),
                        pltpu.VMEM((ns, tm, GROUP_W), f32),
                        pltpu.VMEM((ns, tm, GROUP_W), bf16),
                        pltpu.VMEM((ns, tm, D_MODEL), bf16)],
        compiler_params=pltpu.CompilerParams(
            dimension_semantics=("arbitrary", "arbitrary"),
            vmem_limit_bytes=V7X_VMEM_LIMIT_BYTES),
        name="mixer_prompt",
    )(x, *wts, avg)
    return (out[0].reshape(batch * seq, D_MODEL),) + tuple(out[1:])


def _mixer_sample_kernel(x_ref, gm_ref, wmi_ref, wmo_ref, caw_ref, cab_ref, nag_ref, nab_ref,
                         pw_ref, ps_ref, sng_ref, snb_ref, coef_ref, sb_ref, scw_ref, avg_ref,
                         sta_ref, stp_ref, sts_ref,
                         o_ref, sa_ref, sp_ref, ss_ref, v_ref,
                         z_scr, xa, xp, xs, c_scr, y_scr, *, dec_seq, layer):
    gm_ref, cab_ref, nag_ref, nab_ref, ps_ref, sng_ref, snb_ref = (
        _LayerRow(r, layer) for r in (gm_ref, cab_ref, nag_ref, nab_ref, ps_ref, sng_ref, snb_ref))
    nb, steps = SAMPLE_SEQS, dec_seq
    rows = nb * steps
    hist_a, hist_p, hist_s = CONV_A_WIDTH - 1, POOL_STATE, SHORT_CONV_WIDTH - 1
    avg = avg_ref[...]

    def slabs(col):
        return z_scr[:, col:col + GROUP_W].reshape(steps, nb, GROUP_W)

    x = x_ref[...].reshape(rows, D_MODEL)
    z_scr[...] = _dot(_rms(x, gm_ref[...]).astype(bf16), wmi_ref[...])

    xa[0:hist_a] = sta_ref[...]
    xa[hist_a:hist_a + steps] = slabs(COL_A_VAL) * jax.nn.sigmoid(slabs(COL_A_GATE))
    sa_ref[...] = xa[steps:steps + hist_a]
    for t in range(steps):
        acc = jnp.broadcast_to(cab_ref[...], (nb, GROUP_W))
        for k in range(CONV_A_WIDTH):
            acc = acc + caw_ref[k:k + 1, :] * xa[t + k]
        c_scr[t] = acc
    ya = _head_layer_norm(c_scr[...].reshape(rows, GROUP_W), nag_ref[...], nab_ref[...], avg)
    y_scr[:, 0:GROUP_W] = _silu(ya).astype(bf16)

    xp[0:hist_p] = stp_ref[...]
    xp[hist_p:hist_p + steps] = slabs(COL_ZB)
    sp_ref[...] = xp[steps:steps + hist_p]
    lane_group = lax.broadcasted_iota(jnp.int32, (nb, GROUP_W), 1) // HEAD_DIM
    win = jnp.left_shift(2, lane_group)
    for t in range(steps):
        zb = xp[hist_p + t]
        run = zb
        sums = []
        for j in range(1, max(POOL_WINDOWS)):
            run = run + xp[hist_p + t - j]
            if j + 1 in POOL_WINDOWS:
                sums.append(run)
        cnt = jnp.minimum(PAST_LEN + t + 1, win).astype(f32)
        c_scr[t] = _pool_select(sums, lane_group) / cnt - zb
    d = c_scr[...].reshape(rows, GROUP_W).astype(bf16)
    y_scr[:, GROUP_W:2 * GROUP_W] = (_dot(d, pw_ref[...]) * ps_ref[...]).astype(bf16)

    v = _head_layer_norm(jax.nn.gelu(z_scr[:, COL_V:COL_V + GROUP_W]),
                         sng_ref[...], snb_ref[...], avg)
    v_ref[...] = v.reshape(steps, nb, GROUP_W)
    for t in range(steps):
        s = jnp.broadcast_to(sb_ref[t:t + 1, :], (nb, GROUP_W))
        for j in range(t + 1):
            s = s + coef_ref[j, t:t + 1, :] * v_ref[j]
        u = jax.nn.gelu(z_scr[t * nb:(t + 1) * nb, COL_U:COL_U + GROUP_W])
        y_scr[t * nb:(t + 1) * nb, 2 * GROUP_W:3 * GROUP_W] = (u * s).astype(bf16)

    xs[0:hist_s] = sts_ref[...]
    xs[hist_s:hist_s + steps] = slabs(COL_C_GATE) * slabs(COL_D_IN)
    ss_ref[...] = xs[steps:steps + hist_s]
    for t in range(steps):
        conv = scw_ref[0:1, :] * xs[t]
        for k in range(1, SHORT_CONV_WIDTH):
            conv = conv + scw_ref[k:k + 1, :] * xs[t + k]
        b_gate = z_scr[t * nb:(t + 1) * nb, COL_B_GATE:COL_B_GATE + GROUP_W]
        y_scr[t * nb:(t + 1) * nb, 3 * GROUP_W:4 * GROUP_W] = (b_gate * conv).astype(bf16)

    out = x_ref[...].reshape(rows, D_MODEL) + _dot(y_scr[...], wmo_ref[...])
    o_ref[...] = out.reshape(steps, nb, D_MODEL)


def _mixer_sample_call(x, st_a, st_p, st_s, dec_batch, dec_seq, layer, wts, avg):
    nb = SAMPLE_SEQS
    rows = nb * dec_seq
    row_spec = pl.BlockSpec((dec_seq, nb, D_MODEL), lambda i: (0, i, 0))

    def state_spec(r):
        return pl.BlockSpec((r, nb, GROUP_W), lambda i: (0, i, 0))

    def state_in_spec(r):
        return pl.BlockSpec((None, r, nb, GROUP_W), lambda i: (layer, 0, i, 0))

    def state_shape(r):
        return jax.ShapeDtypeStruct((r, dec_batch, GROUP_W), f32)

    hist_a, hist_p, hist_s = CONV_A_WIDTH - 1, POOL_STATE, SHORT_CONV_WIDTH - 1
    in_specs = ([row_spec] + [_resident(w, layer) for w in wts] + [_resident(avg)]
                + [state_in_spec(hist_a), state_in_spec(hist_p), state_in_spec(hist_s)])
    return pl.pallas_call(
        functools.partial(_mixer_sample_kernel, dec_seq=dec_seq, layer=layer),
        grid=(dec_batch // nb,),
        in_specs=in_specs,
        out_specs=[row_spec, state_spec(hist_a), state_spec(hist_p), state_spec(hist_s),
                   state_spec(dec_seq)],
        out_shape=[jax.ShapeDtypeStruct((dec_seq, dec_batch, D_MODEL), f32),
                   state_shape(hist_a), state_shape(hist_p), state_shape(hist_s),
                   state_shape(dec_seq)],
        scratch_shapes=[pltpu.VMEM((rows, MIX_IN_W), f32),
                        pltpu.VMEM((hist_a + dec_seq, nb, GROUP_W), f32),
                        pltpu.VMEM((hist_p + dec_seq, nb, GROUP_W), f32),
                        pltpu.VMEM((hist_s + dec_seq, nb, GROUP_W), f32),
                        pltpu.VMEM((dec_seq, nb, GROUP_W), f32),
                        pltpu.VMEM((rows, D_MODEL), bf16)],
        compiler_params=pltpu.CompilerParams(
            dimension_semantics=("arbitrary",), vmem_limit_bytes=V7X_VMEM_LIMIT_BYTES),
        name="mixer_sample",
    )(x, *wts, avg, st_a, st_p, st_s)


def _block_diag(w):
    n, g, c, d = w.shape
    eye = jnp.eye(g, dtype=w.dtype)
    return (eye[None, :, None, :, None] * w[:, :, :, None, :]).reshape(n, g * c, g * d)


def kernel(x_prompt, x_sample, p_prompt, p_sample, state_conv_a, state_pool, state_short_conv,
           norm_ffn1, w_ffn1_gate, w_ffn1_up, w_ffn1_down, norm_mix, w_mix_in,
           conv_a_w, conv_a_b, norm_a_g, norm_a_b, pool_w, pool_scale,
           sgu_norm_g, sgu_norm_b, sgu_w, sgu_b, short_conv_w, w_mix_out,
           norm_ffn2, w_ffn2_gate, w_ffn2_up, w_ffn2_down,
           norm_ple, w_ple_gate, w_ple_proj, norm_final):
    depth = w_mix_in.shape[0]
    batch, seq, _ = x_prompt.shape
    dec_batch, dec_seq, _ = x_sample.shape
    assert seq % MIX_TILE == 0 and MIX_TILE % CHUNK == 0 and dec_seq <= CHUNK
    assert batch % MIX_SEQS == 0
    assert dec_batch % SAMPLE_SEQS == 0 and dec_seq == V7X_SUBLANES
    assert (batch * seq) % FFN_TILE == 0 and (dec_batch * dec_seq) % FFN_TILE == 0
    assert all(c % 256 == 0 for c in FFN_COL_SPLITS)

    avg = _block_diag(jnp.full((1, N_HEADS, HEAD_DIM, HEAD_DIM), 1.0 / HEAD_DIM, f32))[0]
    avg = avg.astype(bf16)

    ffn1_mats = [w[0].astype(bf16) for w in (w_ffn1_gate, w_ffn1_up, w_ffn1_down)]
    cast_in_ffn1 = (w_ffn2_gate, w_ffn2_up, w_ffn2_down, w_ple_gate, w_ple_proj,
                    w_mix_in, w_mix_out)
    cast_in_ffn2 = (w_ffn1_gate, w_ffn1_up, w_ffn1_down)
    gf = norm_final.reshape(1, D_MODEL)
    mix_tail = (conv_a_w, conv_a_b, norm_a_g, norm_a_b,
                _block_diag(pool_w).astype(bf16), pool_scale,
                sgu_norm_g, sgu_norm_b)
    sw_prompt = jnp.transpose(sgu_w, (0, 2, 1, 3)).reshape(depth, CHUNK, N_HEADS * CHUNK)
    sb_prompt = jnp.repeat(jnp.swapaxes(sgu_b, 1, 2), HEAD_DIM, axis=2)
    sw_sample = jnp.repeat(jnp.transpose(sgu_w[:, :, :dec_seq, :dec_seq], (0, 3, 2, 1)),
                           HEAD_DIM, axis=3)
    sb_sample = sb_prompt[:, :dec_seq]

    sample_rows = dec_seq * dec_batch
    pp = p_prompt.reshape(depth, batch * seq, D_PLE)
    ps = jnp.swapaxes(p_sample, 1, 2).reshape(depth, sample_rows, D_PLE)
    xp = x_prompt.reshape(batch * seq, D_MODEL)
    xs = jnp.swapaxes(x_sample, 0, 1).reshape(sample_rows, D_MODEL)
    st_a, st_p, st_s = (jnp.swapaxes(st, 1, 2)
                        for st in (state_conv_a, state_pool, state_short_conv))
    prompt_states, sample_states = [], []
    for i in range(depth):
        last = i == depth - 1

        xs, xp, mats = _ffn_call(xs, xp, i, (norm_ffn1, *ffn1_mats),
                                 cast=cast_in_ffn1, cast_layer=i)
        ffn2_mats, ple_mats, mix_mats = mats[0:3], mats[3:5], mats[5:7]
        mix_head = (norm_mix, *mix_mats) + mix_tail
        xp, sa, sp, ss = _mixer_prompt_call(
            xp, batch, seq, i, mix_head + (sw_prompt, sb_prompt, short_conv_w), avg)
        prompt_states.append((sa, sp, ss))
        xs, sa, sp, ss, sv = _mixer_sample_call(
            xs.reshape(dec_seq, dec_batch, D_MODEL), st_a, st_p, st_s, dec_batch, dec_seq, i,
            mix_head + (sw_sample, sb_sample, short_conv_w), avg)
        xs = xs.reshape(sample_rows, D_MODEL)
        sample_states.append((sa, sp, ss, sv))
        xs, xp, ffn1_mats = _ffn_call(
            xs, xp, i, (norm_ffn2, *ffn2_mats), ple=(ps, pp, (norm_ple, *ple_mats), gf),
            final_norm=last, cast=() if last else cast_in_ffn2, cast_layer=i + 1)

    def stack(states, j):
        return jnp.stack([s[j] for s in states])

    def stack_sample(j):
        return jnp.swapaxes(stack(sample_states, j), 1, 2)

    y_sample = jnp.swapaxes(xs.reshape(dec_seq, dec_batch, D_MODEL), 0, 1)
    return (xp.reshape(batch, seq, D_MODEL), y_sample,
            stack(prompt_states, 0), stack_sample(0),
            stack(prompt_states, 1), stack_sample(1),
            stack(prompt_states, 2), stack_sample(2),
            stack_sample(3))
```
